```python
import math
import jax, jax.numpy as jnp
from jax import lax
import numpy as np

D_MODEL = 1024
BATCH = 8
SEQ = 2048
DEPTH = 1
DEC_BATCH = 128
DEC_SEQ = 8
PAST_LEN = 2048
PAGE_SIZE = 128

POOL_WIDTH = D_MODEL // 2
POOL_WINDOWS = (2, 4, 8, 16)
N_POOL_GROUPS = len(POOL_WINDOWS)
POOL_GROUP_DIM = POOL_WIDTH // N_POOL_GROUPS
POOL_STATE = max(POOL_WINDOWS) - 1
N_HEADS = 4
HEAD_DIM = 64
V_HEAD_DIM = 2 * HEAD_DIM
QK_WIDTH = N_HEADS * 2 * HEAD_DIM
ATTN_WIDTH = N_HEADS * V_HEAD_DIM
PROJ_WIDTH = POOL_WIDTH + 2 * QK_WIDTH + ATTN_WIDTH
D_FF = 2816
CONV_WIDTH = 3
N_BUCKETS = 32
MAX_DISTANCE = 128
QBLOCK = 128
N_MOD = 6
EPS = 1e-6

kernel_name = "hymba_pool_diffattn_convffn_step"


def rmsnorm(x, g):
    xf = x.astype(jnp.float32)
    y = xf * lax.rsqrt(jnp.mean(xf * xf, axis=-1, keepdims=True) + EPS)
    return (y * g.astype(jnp.float32)).astype(x.dtype)


def rel_bucket(dist):
    max_exact = N_BUCKETS // 2
    d = jnp.maximum(dist, 1).astype(jnp.float32)
    large = max_exact + (jnp.log(d / max_exact) / math.log(MAX_DISTANCE / max_exact)
                         * (N_BUCKETS - max_exact)).astype(jnp.int32)
    large = jnp.minimum(large, N_BUCKETS - 1)
    return jnp.where(dist < max_exact, dist, large)


def diff_attend(q, k, v, q_pos, k_pos, rel_bias, lam):
    s = jnp.einsum('bqhcd,bkhcd->bhcqk', q, k,
                   preferred_element_type=jnp.float32) * (HEAD_DIM ** -0.5)
    dist = q_pos[:, None] - k_pos[None, :]
    bias = rel_bias.astype(jnp.float32)[rel_bucket(jnp.maximum(dist, 0))]
    s = s + jnp.transpose(bias, (2, 0, 1))[None, :, None]
    s = jnp.where((dist >= 0)[None, None, None], s, -jnp.inf)
    p = jax.nn.softmax(s, axis=-1)
    a = p[:, :, 0] - lam * p[:, :, 1]
    return jnp.einsum('bhqk,bkhd->bqhd', a.astype(v.dtype), v)


def causal_attention(q, k_all, v_all, pos0, rel_bias, lam):
    B, T = q.shape[0], q.shape[1]
    k_pos = jnp.arange(k_all.shape[1], dtype=jnp.int32)
    q_pos = pos0 + jnp.arange(T, dtype=jnp.int32)
    if T % QBLOCK == 0 and T > QBLOCK:
        def block(i):
            qb = lax.dynamic_slice_in_dim(q, i * QBLOCK, QBLOCK, axis=1)
            pb = lax.dynamic_slice_in_dim(q_pos, i * QBLOCK, QBLOCK)
            return diff_attend(qb, k_all, v_all, pb, k_pos, rel_bias, lam)
        o = lax.map(block, jnp.arange(T // QBLOCK))
        return jnp.moveaxis(o, 0, 1).reshape(B, T, N_HEADS, V_HEAD_DIM)
    return diff_attend(q, k_all, v_all, q_pos, k_pos, rel_bias, lam)


def pool_mix(u, prefix, pos0, w_pool, pool_scale):
    B, T = u.shape[0], u.shape[1]
    ext = jnp.concatenate([prefix, u], axis=1)
    extf = ext.astype(jnp.float32)
    cs = jnp.concatenate([jnp.zeros((B, 1, POOL_WIDTH), jnp.float32),
                          jnp.cumsum(extf, axis=1)], axis=1)
    pos = pos0 + jnp.arange(T, dtype=jnp.int32)
    outs = []
    for g, w in enumerate(POOL_WINDOWS):
        sl = slice(g * POOL_GROUP_DIM, (g + 1) * POOL_GROUP_DIM)
        hi = cs[:, POOL_STATE + 1:, sl]
        lo = cs[:, POOL_STATE + 1 - w:POOL_STATE + 1 - w + T, sl]
        cnt = jnp.minimum(pos + 1, w).astype(jnp.float32)[None, :, None]
        outs.append((hi - lo) / cnt - extf[:, POOL_STATE:, sl])
    d = jnp.stack(outs, axis=2).astype(u.dtype)
    y = jnp.einsum('btgc,gcd->btgd', d, w_pool).reshape(B, T, POOL_WIDTH)
    return y * pool_scale, ext[:, -POOL_STATE:]


def causal_dwconv(u, prefix, w, b):
    T = u.shape[1]
    ext = jnp.concatenate([prefix, u], axis=1)
    y = b + w[0] * ext[:, 0:T]
    for j in range(1, CONV_WIDTH):
        y = y + w[j] * ext[:, j:j + T]
    return y, ext[:, -(CONV_WIDTH - 1):]


def layer(x, c, pos0, pool_prefix, conv_prefix, k_past, v_past, lam_init,
          w_ada, b_ada, g_pre_mix, g_post_mix, g_pre_ffn, g_post_ffn, w_in, w_out,
          w_pool, pool_scale, lam_q1, lam_k1, lam_q2, lam_k2, g_head, rel_bias,
          w_up, conv_w, conv_b, w_down):
    B, T = x.shape[0], x.shape[1]
    mod = jax.nn.silu(c) @ w_ada + b_ada
    sh1, sc1, gt1, sh2, sc2, gt2 = [m[:, None] for m in jnp.split(mod, N_MOD, axis=-1)]
    h = rmsnorm(x, g_pre_mix) * (1 + sc1) + sh1
    z = h @ w_in
    u_pool, q, k, v = jnp.split(z, [POOL_WIDTH, POOL_WIDTH + QK_WIDTH,
                                    POOL_WIDTH + 2 * QK_WIDTH], axis=-1)
    q = q.reshape(B, T, N_HEADS, 2, HEAD_DIM)
    k = k.reshape(B, T, N_HEADS, 2, HEAD_DIM)
    v = v.reshape(B, T, N_HEADS, V_HEAD_DIM)
    y_pool, pool_state = pool_mix(u_pool, pool_prefix, pos0, w_pool, pool_scale)
    lam = (jnp.exp(jnp.sum(lam_q1.astype(jnp.float32) * lam_k1.astype(jnp.float32)))
           - jnp.exp(jnp.sum(lam_q2.astype(jnp.float32) * lam_k2.astype(jnp.float32)))
           + lam_init)
    k_all = k if k_past is None else jnp.concatenate([k_past, k], axis=1)
    v_all = v if v_past is None else jnp.concatenate([v_past, v], axis=1)
    o = causal_attention(q, k_all, v_all, pos0, rel_bias, lam)
    o = rmsnorm(o, g_head) * (1.0 - lam_init)
    mix = jnp.concatenate([y_pool, o.reshape(B, T, ATTN_WIDTH)], axis=-1) @ w_out
    x = x + gt1 * rmsnorm(mix, g_post_mix)
    h2 = rmsnorm(x, g_pre_ffn) * (1 + sc2) + sh2
    up, conv_state = causal_dwconv(h2 @ w_up, conv_prefix, conv_w, conv_b)
    gate, val = jnp.split(up, 2, axis=-1)
    f = (jax.nn.gelu(gate, approximate=True) * val) @ w_down
    x = x + gt2 * rmsnorm(f, g_post_ffn)
    return x, k, v, pool_state, conv_state


def setup_inputs(seed: int = 0) -> dict:
    key = jax.random.key(seed)
    ks = jax.random.split(key, 32)
    f32 = jnp.float32
    n_pages = PAST_LEN // PAGE_SIZE
    n_phys = (DEC_BATCH * n_pages * 5) // 4
    nrm = lambda k, shape, s: jax.random.normal(k, shape, f32) * s
    page_table = jax.random.permutation(ks[6], n_phys)[:DEC_BATCH * n_pages]
    page_table = page_table.reshape(DEC_BATCH, n_pages).astype(jnp.int32)
    return {
        "x_prompt": nrm(ks[0], (BATCH, SEQ, D_MODEL), 1.0),
        "x_sample": nrm(ks[1], (DEC_BATCH, DEC_SEQ, D_MODEL), 1.0),
        "c_prompt": nrm(ks[2], (BATCH, D_MODEL), 1.0),
        "c_sample": nrm(ks[3], (DEC_BATCH, D_MODEL), 1.0),
        "cache_k": nrm(ks[4], (DEPTH, n_phys, PAGE_SIZE, N_HEADS, 2, HEAD_DIM), 1.0),
        "cache_v": nrm(ks[5], (DEPTH, n_phys, PAGE_SIZE, N_HEADS, V_HEAD_DIM), 1.0),
        "page_table": page_table,
        "state_pool": nrm(ks[7], (DEPTH, DEC_BATCH, POOL_STATE, POOL_WIDTH), 1.0),
        "state_conv": nrm(ks[8], (DEPTH, DEC_BATCH, CONV_WIDTH - 1, 2 * D_FF), 1.0),
        "w_ada": nrm(ks[9], (DEPTH, D_MODEL, N_MOD * D_MODEL), 0.5 * D_MODEL ** -0.5),
        "b_ada": nrm(ks[10], (DEPTH, N_MOD * D_MODEL), 0.02),
        "g_pre_mix": 1.0 + nrm(ks[11], (DEPTH, D_MODEL), 0.02),
        "g_post_mix": 1.0 + nrm(ks[12], (DEPTH, D_MODEL), 0.02),
        "g_pre_ffn": 1.0 + nrm(ks[13], (DEPTH, D_MODEL), 0.02),
        "g_post_ffn": 1.0 + nrm(ks[14], (DEPTH, D_MODEL), 0.02),
        "w_in": nrm(ks[15], (DEPTH, D_MODEL, PROJ_WIDTH), D_MODEL ** -0.5),
        "w_out": nrm(ks[16], (DEPTH, POOL_WIDTH + ATTN_WIDTH, D_MODEL), (POOL_WIDTH + ATTN_WIDTH) ** -0.5),
        "w_pool": nrm(ks[17], (DEPTH, N_POOL_GROUPS, POOL_GROUP_DIM, POOL_GROUP_DIM), POOL_GROUP_DIM ** -0.5),
        "pool_scale": 1.0 + nrm(ks[18], (DEPTH, POOL_WIDTH), 0.1),
        "lam_q1": nrm(ks[19], (DEPTH, HEAD_DIM), 0.1),
        "lam_k1": nrm(ks[20], (DEPTH, HEAD_DIM), 0.1),
        "lam_q2": nrm(ks[21], (DEPTH, HEAD_DIM), 0.1),
        "lam_k2": nrm(ks[22], (DEPTH, HEAD_DIM), 0.1),
        "g_head": 1.0 + nrm(ks[23], (DEPTH, V_HEAD_DIM), 0.02),
        "rel_bias": nrm(ks[24], (N_BUCKETS, N_HEADS), 0.5),
        "w_up": nrm(ks[25], (DEPTH, D_MODEL, 2 * D_FF), D_MODEL ** -0.5),
        "conv_w": nrm(ks[26], (DEPTH, CONV_WIDTH, 2 * D_FF), CONV_WIDTH ** -0.5),
        "conv_b": nrm(ks[27], (DEPTH, 2 * D_FF), 0.02),
        "w_down": nrm(ks[28], (DEPTH, D_FF, D_MODEL), D_FF ** -0.5),
    }


def reference(x_prompt, x_sample, c_prompt, c_sample, cache_k, cache_v, page_table,
              state_pool, state_conv, w_ada, b_ada, g_pre_mix, g_post_mix, g_pre_ffn,
              g_post_ffn, w_in, w_out, w_pool, pool_scale, lam_q1, lam_k1, lam_q2, lam_k2,
              g_head, rel_bias, w_up, conv_w, conv_b, w_down):
    n_dec = page_table.shape[0]
    past_len = page_table.shape[1] * cache_k.shape[2]
    xp, xs = x_prompt, x_sample
    kp_l, vp_l, pp_l, cp_l = [], [], [], []
    ks_l, vs_l, ps_l, cs_l = [], [], [], []
    for l in range(DEPTH):
        lam_init = 0.8 - 0.6 * math.exp(-0.3 * l)
        params = (w_ada[l], b_ada[l], g_pre_mix[l], g_post_mix[l], g_pre_ffn[l], g_post_ffn[l],
                  w_in[l], w_out[l], w_pool[l], pool_scale[l], lam_q1[l], lam_k1[l],
                  lam_q2[l], lam_k2[l], g_head[l], rel_bias, w_up[l], conv_w[l], conv_b[l],
                  w_down[l])
        pool0 = jnp.zeros((xp.shape[0], POOL_STATE, POOL_WIDTH), xp.dtype)
        conv0 = jnp.zeros((xp.shape[0], CONV_WIDTH - 1, 2 * D_FF), xp.dtype)
        xp, kp, vp, pp, cp = layer(xp, c_prompt, 0, pool0, conv0, None, None, lam_init, *params)
        k_past = cache_k[l][page_table].reshape(n_dec, past_len, N_HEADS, 2, HEAD_DIM)
        v_past = cache_v[l][page_table].reshape(n_dec, past_len, N_HEADS, V_HEAD_DIM)
        xs, ksn, vsn, psn, csn = layer(xs, c_sample, past_len, state_pool[l], state_conv[l],
                                       k_past, v_past, lam_init, *params)
        kp_l.append(kp); vp_l.append(vp); pp_l.append(pp); cp_l.append(cp)
        ks_l.append(ksn); vs_l.append(vsn); ps_l.append(psn); cs_l.append(csn)
    return (xp, xs,
            jnp.stack(kp_l), jnp.stack(vp_l), jnp.stack(pp_l), jnp.stack(cp_l),
            jnp.stack(ks_l), jnp.stack(vs_l), jnp.stack(ps_l), jnp.stack(cs_l))
```

```python
import functools
import math

import jax
import jax.numpy as jnp
from jax import lax
from jax.experimental import pallas as pl
from jax.experimental.pallas import tpu as pltpu

F32 = jnp.float32
BF16 = jnp.bfloat16

D_MODEL = 1024
POOL_WIDTH = 512
POOL_WINDOWS = (2, 4, 8, 16)
POOL_GROUP_DIM = 128
POOL_STATE = 15
POOL_HALO = 16
N_HEADS = 4
HEAD_DIM = 64
V_HEAD_DIM = 128
QK_WIDTH = 512
ATTN_WIDTH = 512
PROJ_WIDTH = 2048
D_FF = 2816
CONV_WIDTH = 3
CONV_HALO = 8
N_BUCKETS = 32
MAX_EXACT = N_BUCKETS // 2
MAX_DISTANCE = 128
N_MOD = 6
EPS = 1e-6
LAM_INIT = 0.8 - 0.6 * math.exp(-0.3 * 0)
NEG_BIG = -1e30

LANES = 128
MXU_TILE = 256
VMEM_LIMIT = 56 * 1024 * 1024

ATTN_TQ = 256
ATTN_TK = 256
ROW_TILE = 256
FF_CHUNK = 256


def _const_spec(shape):
    nd = len(shape)
    return pl.BlockSpec(shape, lambda *_: (0,) * nd, pipeline_mode=pl.Buffered(1))


def _rms(x, g):
    return x * lax.rsqrt(jnp.mean(x * x, axis=-1, keepdims=True) + EPS) * g


def _bucket(dist):
    d = jnp.maximum(dist, 1).astype(F32)
    large = MAX_EXACT + (jnp.log(d / MAX_EXACT) / math.log(MAX_DISTANCE / MAX_EXACT)
                         * (N_BUCKETS - MAX_EXACT)).astype(jnp.int32)
    large = jnp.minimum(large, N_BUCKETS - 1)
    return jnp.where(dist < MAX_EXACT, dist, large)


def _setup_kernel(rb_ref, lq1_ref, lk1_ref, lq2_ref, lk2_ref,
                  lam_ref, bp_ref, b31_ref, bs_ref, b31s_ref):
    lam = (jnp.exp(jnp.sum(lq1_ref[...] * lk1_ref[...], axis=-1, keepdims=True))
           - jnp.exp(jnp.sum(lq2_ref[...] * lk2_ref[...], axis=-1, keepdims=True)) + LAM_INIT)
    lam_ref[...] = jnp.broadcast_to(lam, lam_ref.shape)

    def lookup(bucket, h):
        out = jnp.full(bucket.shape, rb_ref[N_BUCKETS - 1, h], F32)
        for b in range(N_BUCKETS - 1):
            out = jnp.where(bucket == b, rb_ref[b, h], out)
        return out

    r = lax.broadcasted_iota(jnp.int32, (ATTN_TQ, ATTN_TK), 0)
    c = lax.broadcasted_iota(jnp.int32, (ATTN_TQ, ATTN_TK), 1)
    for typ in range(2):
        dist = typ * ATTN_TK + r - c
        bucket = _bucket(jnp.maximum(dist, 0))
        for h in range(N_HEADS):
            bp_ref[h, typ] = jnp.where(dist >= 0, lookup(bucket, h), NEG_BIG)
    for h in range(N_HEADS):
        b31_ref[h] = jnp.full(b31_ref.shape[1:], rb_ref[N_BUCKETS - 1, h], F32)

    rows = 2 * 8
    t = lax.broadcasted_iota(jnp.int32, (rows, 2 * LANES), 0) % 8
    col = lax.broadcasted_iota(jnp.int32, (rows, 2 * LANES), 1)
    dist = jnp.where(col < LANES, LANES + t - col, t - (col - LANES))
    bucket = _bucket(jnp.maximum(dist, 0))
    for h in range(N_HEADS):
        bs_ref[h * rows:(h + 1) * rows, :] = jnp.where(dist >= 0, lookup(bucket, h), NEG_BIG)
        b31s_ref[h * rows:(h + 1) * rows, :] = jnp.full((rows, LANES), rb_ref[N_BUCKETS - 1, h], F32)


def _setup(rel_bias, lq1, lk1, lq2, lk2):
    vspec = pl.BlockSpec(memory_space=pltpu.VMEM)
    return pl.pallas_call(
        _setup_kernel,
        out_shape=(jax.ShapeDtypeStruct((8, LANES), F32),
                   jax.ShapeDtypeStruct((N_HEADS, 2, ATTN_TQ, ATTN_TK), F32),
                   jax.ShapeDtypeStruct((N_HEADS, 8, LANES), F32),
                   jax.ShapeDtypeStruct((N_HEADS * 16, 2 * LANES), F32),
                   jax.ShapeDtypeStruct((N_HEADS * 16, LANES), F32)),
        in_specs=[pl.BlockSpec(memory_space=pltpu.SMEM), vspec, vspec, vspec, vspec],
        out_specs=(vspec, vspec, vspec, vspec, vspec),
        name="setup",
    )(rel_bias, lq1, lk1, lq2, lk2)


def _mod_kernel(c_ref, w_ref, b_ref, o_ref):
    c = c_ref[...]
    s = (c * jax.nn.sigmoid(c)).astype(BF16)
    o_ref[...] = jnp.dot(s, w_ref[...].astype(BF16), preferred_element_type=F32) + b_ref[...]


def _modulation(c_all, w_ada, b_ada):
    n = c_all.shape[0]
    tn = 1024
    return pl.pallas_call(
        _mod_kernel,
        out_shape=jax.ShapeDtypeStruct((n, N_MOD * D_MODEL), F32),
        grid=(N_MOD * D_MODEL // tn,),
        in_specs=[pl.BlockSpec((n, D_MODEL), lambda j: (0, 0)),
                  pl.BlockSpec((D_MODEL, tn), lambda j: (0, j)),
                  pl.BlockSpec((1, tn), lambda j: (0, j))],
        out_specs=pl.BlockSpec((n, tn), lambda j: (0, j)),
        compiler_params=pltpu.CompilerParams(dimension_semantics=("arbitrary",),
                                             vmem_limit_bytes=VMEM_LIMIT),
        name="modulation",
    )(c_all, w_ada, b_ada.reshape(1, -1))


def _inproj_kernel(*refs, carry, pos0, bt, tt):
    if carry:
        (x_ref, mod_ref, g_ref, w_in_ref, w_pool_ref, ps_ref, wkt_ref,
         q_ref, k_ref, v_ref, kb_ref, vb_ref, yp_ref, st_ref, ext_ref, carry_ref) = refs
    else:
        (x_ref, mod_ref, g_ref, w_in_ref, w_pool_ref, ps_ref, prefix_ref,
         q_ref, k_ref, v_ref, yp_ref, st_ref, ext_ref) = refs
    m = bt * tt
    x = x_ref[...]
    h = _rms(x, g_ref[...]) * (1.0 + mod_ref[:, 1:2, :]) + mod_ref[:, 0:1, :]
    h2 = h.reshape(m, D_MODEL).astype(BF16)

    def proj(off, width):
        return jnp.dot(h2, w_in_ref[:, off:off + width], preferred_element_type=F32)

    zq = proj(POOL_WIDTH, QK_WIDTH)
    q_ref[...] = (zq * (HEAD_DIM ** -0.5)).astype(BF16).reshape(bt, tt, QK_WIDTH)
    zv = proj(POOL_WIDTH + 2 * QK_WIDTH, ATTN_WIDTH)
    v_ref[...] = zv.reshape(bt, tt, ATTN_WIDTH)
    if carry:
        zkt = lax.dot_general(wkt_ref[...], h2, (((1,), (1,)), ((), ())), preferred_element_type=F32)
        k_ref[0] = zkt
        kb_ref[0, :, 0] = zkt.astype(BF16).reshape(N_HEADS, 2 * HEAD_DIM, tt)
        vb_ref[...] = zv.astype(BF16).reshape(bt, tt, ATTN_WIDTH)
    else:
        k_ref[...] = proj(POOL_WIDTH + QK_WIDTH, QK_WIDTH).reshape(bt, tt, QK_WIDTH)

    u3 = proj(0, POOL_WIDTH).reshape(bt, tt, POOL_WIDTH)
    if carry:
        @pl.when(pl.program_id(1) == 0)
        def _():
            carry_ref[...] = jnp.zeros_like(carry_ref)
        ext_ref[:, 0:POOL_HALO, :] = carry_ref[...]
    else:
        ext_ref[:, 0:POOL_HALO, :] = prefix_ref[...]
    ext_ref[:, POOL_HALO:, :] = u3
    if carry:
        carry_ref[...] = u3[:, tt - POOL_HALO:, :]
        t_base = pl.program_id(1) * tt
    else:
        t_base = 0
    st_ref[...] = ext_ref[:, tt:tt + POOL_HALO, :]

    pos = pos0 + t_base + lax.broadcasted_iota(jnp.int32, (bt, tt, POOL_GROUP_DIM), 1)
    for g, w in enumerate(POOL_WINDOWS):
        sl = slice(g * POOL_GROUP_DIM, (g + 1) * POOL_GROUP_DIM)
        tok = ext_ref[:, POOL_HALO:, sl]
        acc = tok
        for j in range(1, w):
            acc = acc + ext_ref[:, POOL_HALO - j:POOL_HALO - j + tt, sl]
        cnt = jnp.minimum(pos + 1, w).astype(F32)
        d = (acc / cnt - tok).reshape(m, POOL_GROUP_DIM).astype(BF16)
        y = jnp.dot(d, w_pool_ref[g], preferred_element_type=F32) * ps_ref[:, sl]
        yp_ref[:, :, sl] = y.astype(BF16).reshape(bt, tt, POOL_GROUP_DIM)


def _inproj(x, mod, g_pre, w_in, w_kt, w_pool, pool_scale, prefix, *, pos0):
    nb, t, _ = x.shape
    carry = prefix is None
    if carry:
        bt, tt = 1, ROW_TILE
        grid = (nb, t // tt)
        tile = lambda w: pl.BlockSpec((bt, tt, w), lambda b, i: (b, i, 0))
        per_b = lambda r, w: pl.BlockSpec((bt, r, w), lambda b, i: (b, 0, 0))
        sem = ("arbitrary", "arbitrary")
    else:
        bt, tt = ROW_TILE // t, t
        grid = (nb // bt,)
        tile = lambda w: pl.BlockSpec((bt, tt, w), lambda i: (i, 0, 0))
        per_b = lambda r, w: pl.BlockSpec((bt, r, w), lambda i: (i, 0, 0))
        sem = ("arbitrary",)
    in_specs = [tile(D_MODEL), per_b(N_MOD, D_MODEL), _const_spec((1, 1, D_MODEL)),
                _const_spec((D_MODEL, PROJ_WIDTH)),
                _const_spec((len(POOL_WINDOWS), POOL_GROUP_DIM, POOL_GROUP_DIM)),
                _const_spec((1, POOL_WIDTH))]
    args = [x, mod, g_pre.reshape(1, 1, D_MODEL), w_in, w_pool, pool_scale.reshape(1, POOL_WIDTH)]
    scratch = [pltpu.VMEM((bt, POOL_HALO + tt, POOL_WIDTH), F32)]
    st_shape = jax.ShapeDtypeStruct((nb, POOL_HALO, POOL_WIDTH), F32)
    act = lambda w, dt: jax.ShapeDtypeStruct((nb, t, w), dt)
    if carry:
        assert tt == ATTN_TK
        scratch.append(pltpu.VMEM((1, POOL_HALO, POOL_WIDTH), F32))
        in_specs.append(_const_spec((QK_WIDTH, D_MODEL)))
        args.append(w_kt)
        out_shape = (act(QK_WIDTH, BF16),
                     jax.ShapeDtypeStruct((nb, QK_WIDTH, t), F32),
                     act(ATTN_WIDTH, F32),
                     jax.ShapeDtypeStruct((nb, N_HEADS, t // tt, 2 * HEAD_DIM, tt), BF16),
                     act(ATTN_WIDTH, BF16), act(POOL_WIDTH, BF16), st_shape)
        out_specs = (tile(QK_WIDTH),
                     pl.BlockSpec((1, QK_WIDTH, tt), lambda b, i: (b, 0, i)),
                     tile(ATTN_WIDTH),
                     pl.BlockSpec((1, N_HEADS, 1, 2 * HEAD_DIM, tt), lambda b, i: (b, 0, i, 0, 0)),
                     tile(ATTN_WIDTH), tile(POOL_WIDTH), per_b(POOL_HALO, POOL_WIDTH))
    else:
        in_specs.append(per_b(POOL_HALO, POOL_WIDTH))
        args.append(prefix)
        out_shape = (act(QK_WIDTH, BF16), act(QK_WIDTH, F32), act(ATTN_WIDTH, F32),
                     act(POOL_WIDTH, BF16), st_shape)
        out_specs = (tile(QK_WIDTH), tile(QK_WIDTH), tile(ATTN_WIDTH), tile(POOL_WIDTH),
                     per_b(POOL_HALO, POOL_WIDTH))
    return pl.pallas_call(
        functools.partial(_inproj_kernel, carry=carry, pos0=pos0, bt=bt, tt=tt),
        out_shape=out_shape, grid=grid, in_specs=in_specs, out_specs=out_specs,
        scratch_shapes=scratch,
        compiler_params=pltpu.CompilerParams(dimension_semantics=sem, vmem_limit_bytes=VMEM_LIMIT),
        name="inproj_prompt" if carry else "inproj_sample",
    )(*args)


def _attn_prompt_kernel(q_ref, kb_ref, vb_ref, bp_ref, b31_ref, lam_ref, gh_ref, o_ref,
                        m_ref, l_ref, acc_ref):
    tq, tk = ATTN_TQ, ATTN_TK
    qi = pl.program_id(2)
    q = q_ref[0].astype(F32)
    lane = lax.broadcasted_iota(jnp.int32, (tq, 2 * HEAD_DIM), 1)
    qs = jnp.concatenate([jnp.where(lane < HEAD_DIM, q, 0.0),
                          jnp.where(lane >= HEAD_DIM, q, 0.0)], axis=0).astype(BF16)
    m_ref[...] = jnp.full(m_ref.shape, NEG_BIG, F32)
    l_ref[...] = jnp.zeros_like(l_ref)
    acc_ref[...] = jnp.zeros_like(acc_ref)

    def step(j, bias):
        start = pl.multiple_of(j * tk, tk)
        v = vb_ref[0, pl.ds(start, tk), :]
        s = jnp.dot(qs, kb_ref[0, 0, j], preferred_element_type=F32) + bias
        m_old = m_ref[...]
        m_new = jnp.maximum(m_old, jnp.max(s, axis=-1, keepdims=True))
        alpha = jnp.exp(m_old - m_new)
        p = jnp.exp(s - m_new)
        l_ref[...] = alpha * l_ref[...] + jnp.sum(p, axis=-1, keepdims=True)
        acc_ref[...] = alpha * acc_ref[...] + jnp.dot(p.astype(BF16), v, preferred_element_type=F32)
        m_ref[...] = m_new

    far_bias = b31_ref[0, 0:1, 0:1]

    def far_body(j, carry):
        step(j, far_bias)
        return carry

    lax.fori_loop(0, jnp.maximum(qi - 1, 0), far_body, 0)

    @pl.when(qi >= 1)
    def _():
        b1 = bp_ref[0, 1]
        step(qi - 1, jnp.concatenate([b1, b1], axis=0))

    b0 = bp_ref[0, 0]
    step(qi, jnp.concatenate([b0, b0], axis=0))

    acc = acc_ref[...]
    l = l_ref[...]
    lam = lam_ref[0:1, 0:1]
    o = acc[:tq] / l[:tq] - lam * (acc[tq:] / l[tq:])
    o_ref[0] = (_rms(o, gh_ref[...]) * (1.0 - LAM_INIT)).astype(BF16)


def _attn_prompt(q, kb, vb, bias_p, b31, lam, g_head):
    nb, t, _ = q.shape
    tq = ATTN_TQ
    hw = V_HEAD_DIM
    return pl.pallas_call(
        _attn_prompt_kernel,
        out_shape=jax.ShapeDtypeStruct((nb, t, ATTN_WIDTH), BF16),
        grid=(nb, N_HEADS, t // tq),
        in_specs=[pl.BlockSpec((1, tq, hw), lambda b, h, i: (b, i, h)),
                  pl.BlockSpec((1, 1, t // ATTN_TK, 2 * HEAD_DIM, ATTN_TK), lambda b, h, i: (b, h, 0, 0, 0)),
                  pl.BlockSpec((1, t, hw), lambda b, h, i: (b, 0, h)),
                  pl.BlockSpec((1, 2, ATTN_TQ, ATTN_TK), lambda b, h, i: (h, 0, 0, 0)),
                  pl.BlockSpec((1, 8, LANES), lambda b, h, i: (h, 0, 0)),
                  pl.BlockSpec((8, LANES), lambda b, h, i: (0, 0)),
                  pl.BlockSpec((1, hw), lambda b, h, i: (0, 0))],
        out_specs=pl.BlockSpec((1, tq, hw), lambda b, h, i: (b, i, h)),
        scratch_shapes=[pltpu.VMEM((2 * tq, 1), F32), pltpu.VMEM((2 * tq, 1), F32),
                        pltpu.VMEM((2 * tq, hw), F32)],
        compiler_params=pltpu.CompilerParams(
            dimension_semantics=("arbitrary", "arbitrary", "arbitrary"), vmem_limit_bytes=VMEM_LIMIT),
        name="attn_prompt",
    )(q, kb, vb, bias_p, b31, lam, g_head.reshape(1, hw))


def _attn_sample_kernel(pt_ref, q_ref, kn_ref, vn_ref, bs_ref, b31s_ref, lam_ref, gh_ref, *rest,
                        n_pages, page):
    k_pages = rest[:n_pages]
    v_pages = rest[n_pages:2 * n_pages]
    o_ref, kbf_ref, vbf_ref = rest[2 * n_pages:]
    past = n_pages * page
    tnew = q_ref.shape[1]
    n_maps = 2 * N_HEADS

    q = q_ref[0].astype(F32)
    qt = jnp.concatenate([q] * n_maps, axis=0)
    rg = lax.broadcasted_iota(jnp.int32, qt.shape, 0) // tnew
    cg = lax.broadcasted_iota(jnp.int32, qt.shape, 1) // HEAD_DIM
    qbd = jnp.where(rg == cg, qt, 0.0).astype(BF16)

    for j in range(n_pages):
        kbf_ref[:, j * page:(j + 1) * page] = k_pages[j][0].astype(BF16)
        for h in range(N_HEADS):
            vbf_ref[j * page:(j + 1) * page, h * V_HEAD_DIM:(h + 1) * V_HEAD_DIM] = (
                v_pages[j][0, pl.ds(h, page, stride=N_HEADS), :].astype(BF16))
    pad = jnp.zeros((LANES - tnew, QK_WIDTH), F32)
    k_new = jnp.concatenate([kn_ref[0], pad], axis=0).astype(BF16)
    vbf_ref[past:, :] = jnp.concatenate([vn_ref[0], pad], axis=0).astype(BF16)

    s_past = jnp.dot(qbd, kbf_ref[...], preferred_element_type=F32)
    s_new = lax.dot_general(qbd, k_new, (((1,), (1,)), ((), ())), preferred_element_type=F32)
    near = past - page
    s = jnp.concatenate([s_past[:, :near] + b31s_ref[:, 0:1],
                         jnp.concatenate([s_past[:, near:], s_new], axis=1) + bs_ref[...]], axis=1)
    mx = jnp.max(s, axis=-1, keepdims=True)
    p = jnp.exp(s - mx)
    pn = p / jnp.sum(p, axis=-1, keepdims=True)
    lam = lam_ref[0:1, 0:1]
    a = jnp.concatenate(
        [pn[2 * h * tnew:(2 * h + 1) * tnew] - lam * pn[(2 * h + 1) * tnew:(2 * h + 2) * tnew]
         for h in range(N_HEADS)], axis=0).astype(BF16)
    r = jnp.dot(a, vbf_ref[...], preferred_element_type=F32)
    outs = []
    for h in range(N_HEADS):
        o = r[h * tnew:(h + 1) * tnew, h * V_HEAD_DIM:(h + 1) * V_HEAD_DIM]
        outs.append(_rms(o, gh_ref[...]) * (1.0 - LAM_INIT))
    o_ref[0] = jnp.concatenate(outs, axis=1).astype(BF16)


def _attn_sample(page_table, q, k_new, v_new, cache_k, cache_v, bias_s, b31s, lam, g_head):
    nb, tnew, _ = q.shape
    n_pages = page_table.shape[1]
    n_phys, page = cache_k.shape[0], cache_k.shape[1]
    ck = jnp.transpose(cache_k, (0, 2, 3, 4, 1)).reshape(n_phys, QK_WIDTH, page)
    cv = cache_v.reshape(n_phys, page * N_HEADS, V_HEAD_DIM)
    tile = lambda w: pl.BlockSpec((1, tnew, w), lambda b, pt: (b, 0, 0))
    const = lambda shape: pl.BlockSpec(shape, lambda b, pt: (0,) * len(shape))

    def page_spec(j, shape):
        return pl.BlockSpec((1,) + shape, lambda b, pt: (pt[b, j], 0, 0))

    in_specs = ([tile(QK_WIDTH), tile(QK_WIDTH), tile(ATTN_WIDTH),
                 const(bias_s.shape), const(b31s.shape), const(lam.shape), const((1, V_HEAD_DIM))]
                + [page_spec(j, (QK_WIDTH, page)) for j in range(n_pages)]
                + [page_spec(j, (page * N_HEADS, V_HEAD_DIM)) for j in range(n_pages)])
    total = n_pages * page + LANES
    return pl.pallas_call(
        functools.partial(_attn_sample_kernel, n_pages=n_pages, page=page),
        out_shape=jax.ShapeDtypeStruct((nb, tnew, ATTN_WIDTH), BF16),
        grid_spec=pltpu.PrefetchScalarGridSpec(
            num_scalar_prefetch=1, grid=(nb,), in_specs=in_specs,
            out_specs=pl.BlockSpec((1, tnew, ATTN_WIDTH), lambda b, pt: (b, 0, 0)),
            scratch_shapes=[pltpu.VMEM((QK_WIDTH, n_pages * page), BF16),
                            pltpu.VMEM((total, ATTN_WIDTH), BF16)]),
        compiler_params=pltpu.CompilerParams(dimension_semantics=("arbitrary",),
                                             vmem_limit_bytes=VMEM_LIMIT),
        name="attn_sample",
    )(page_table, q, k_new, v_new, bias_s, b31s, lam, g_head.reshape(1, V_HEAD_DIM),
      *([ck] * n_pages), *([cv] * n_pages))


def _ffn_kernel(*refs, carry, bt, tt):
    if carry:
        (x_ref, yp_ref, at_ref, mod_ref, gpm_ref, gpf_ref, gpo_ref, w_out_ref, w_up_ref, cw_ref, cb_ref,
         w_down_ref, y_ref, cs_ref, buf_ref, carry_ref) = refs
    else:
        (x_ref, yp_ref, at_ref, mod_ref, gpm_ref, gpf_ref, gpo_ref, w_out_ref, w_up_ref, cw_ref, cb_ref,
         w_down_ref, prefix_ref, y_ref, cs_ref, buf_ref) = refs
    m = bt * tt
    x = x_ref[...]
    mix_in = jnp.concatenate([yp_ref[...], at_ref[...]], axis=-1).reshape(m, D_MODEL)
    mix = jnp.dot(mix_in, w_out_ref[...], preferred_element_type=F32).reshape(bt, tt, D_MODEL)
    x1 = x + mod_ref[:, 2:3, :] * _rms(mix, gpm_ref[...])
    h = _rms(x1, gpf_ref[...]) * (1.0 + mod_ref[:, 4:5, :]) + mod_ref[:, 3:4, :]
    h2 = h.reshape(m, D_MODEL).astype(BF16)

    if carry:
        @pl.when(pl.program_id(1) == 0)
        def _():
            carry_ref[...] = jnp.zeros_like(carry_ref)
        halo_ref = carry_ref
    else:
        halo_ref = prefix_ref

    def conv_part(slot, off):
        sl = slice(off, off + FF_CHUNK)
        u3 = jnp.dot(h2, w_up_ref[:, sl], preferred_element_type=F32).reshape(bt, tt, FF_CHUNK)
        buf_ref[slot, :, 0:CONV_HALO, :] = halo_ref[:, :, sl]
        buf_ref[slot, :, CONV_HALO:, :] = u3
        last = u3[:, tt - CONV_HALO:, :]
        if carry:
            carry_ref[:, :, sl] = last
        cs_ref[:, :, sl] = last
        y = (cb_ref[:, :, sl]
             + cw_ref[0:1, :, sl] * buf_ref[slot, :, CONV_HALO - 2:CONV_HALO - 2 + tt, :]
             + cw_ref[1:2, :, sl] * buf_ref[slot, :, CONV_HALO - 1:CONV_HALO - 1 + tt, :]
             + cw_ref[2:3, :, sl] * u3)
        return y.reshape(m, FF_CHUNK)

    f = jnp.zeros((m, D_MODEL), F32)
    for c in range(D_FF // FF_CHUNK):
        slot = 2 * (c % 2)
        gate = conv_part(slot, c * FF_CHUNK)
        val = conv_part(slot + 1, D_FF + c * FF_CHUNK)
        act = (jax.nn.gelu(gate, approximate=True) * val).astype(BF16)
        f = f + jnp.dot(act, w_down_ref[c * FF_CHUNK:(c + 1) * FF_CHUNK, :], preferred_element_type=F32)
    y_ref[...] = x1 + mod_ref[:, 5:6, :] * _rms(f.reshape(bt, tt, D_MODEL), gpo_ref[...])


def _ffn(x, yp, attn, mod, g_post_mix, g_pre_ffn, g_post_ffn, w_out, w_up, conv_w, conv_b, w_down, prefix):
    nb, t, _ = x.shape
    carry = prefix is None
    if carry:
        bt, tt = 1, ROW_TILE
        grid = (nb, t // tt)
        tile = lambda w: pl.BlockSpec((bt, tt, w), lambda b, i: (b, i, 0))
        per_b = lambda r, w: pl.BlockSpec((bt, r, w), lambda b, i: (b, 0, 0))
        sem = ("arbitrary", "arbitrary")
    else:
        bt, tt = ROW_TILE // t, t
        grid = (nb // bt,)
        tile = lambda w: pl.BlockSpec((bt, tt, w), lambda i: (i, 0, 0))
        per_b = lambda r, w: pl.BlockSpec((bt, r, w), lambda i: (i, 0, 0))
        sem = ("arbitrary",)
    gspec = _const_spec((1, 1, D_MODEL))
    in_specs = [tile(D_MODEL), tile(POOL_WIDTH), tile(ATTN_WIDTH), per_b(N_MOD, D_MODEL),
                gspec, gspec, gspec,
                _const_spec((D_MODEL, D_MODEL)), _const_spec((D_MODEL, 2 * D_FF)),
                _const_spec((CONV_WIDTH, 1, 2 * D_FF)), _const_spec((1, 1, 2 * D_FF)),
                _const_spec((D_FF, D_MODEL))]
    args = [x, yp, attn, mod, g_post_mix.reshape(1, 1, -1), g_pre_ffn.reshape(1, 1, -1),
            g_post_ffn.reshape(1, 1, -1), w_out, w_up, conv_w.reshape(CONV_WIDTH, 1, -1),
            conv_b.reshape(1, 1, -1), w_down]
    scratch = [pltpu.VMEM((4, bt, CONV_HALO + tt, FF_CHUNK), F32)]
    if carry:
        scratch.append(pltpu.VMEM((1, CONV_HALO, 2 * D_FF), F32))
    else:
        in_specs.append(per_b(CONV_HALO, 2 * D_FF))
        args.append(prefix)
    return pl.pallas_call(
        functools.partial(_ffn_kernel, carry=carry, bt=bt, tt=tt),
        out_shape=(jax.ShapeDtypeStruct((nb, t, D_MODEL), F32),
                   jax.ShapeDtypeStruct((nb, CONV_HALO, 2 * D_FF), F32)),
        grid=grid, in_specs=in_specs,
        out_specs=(tile(D_MODEL), per_b(CONV_HALO, 2 * D_FF)),
        scratch_shapes=scratch,
        compiler_params=pltpu.CompilerParams(dimension_semantics=sem, vmem_limit_bytes=VMEM_LIMIT),
        name="ffn_prompt" if carry else "ffn_sample",
    )(*args)


def kernel(x_prompt, x_sample, c_prompt, c_sample, cache_k, cache_v, page_table, state_pool, state_conv,
           w_ada, b_ada, g_pre_mix, g_post_mix, g_pre_ffn, g_post_ffn, w_in, w_out, w_pool, pool_scale,
           lam_q1, lam_k1, lam_q2, lam_k2, g_head, rel_bias, w_up, conv_w, conv_b, w_down):
    nbp, seq, _ = x_prompt.shape
    nbs, dec_seq, _ = x_sample.shape
    past_len = page_table.shape[1] * cache_k.shape[2]
    l = 0

    lam, bias_p, b31, bias_s, b31s = _setup(rel_bias, lam_q1[l:l + 1], lam_k1[l:l + 1],
                                             lam_q2[l:l + 1], lam_k2[l:l + 1])
    mod = _modulation(jnp.concatenate([c_prompt, c_sample], axis=0), w_ada[l], b_ada[l])
    mod = mod.reshape(nbp + nbs, N_MOD, D_MODEL)
    mod_p, mod_s = mod[:nbp], mod[nbp:]

    w_in_b = w_in[l].astype(BF16)
    w_kt_b = w_in[l][:, POOL_WIDTH + QK_WIDTH:POOL_WIDTH + 2 * QK_WIDTH].T.astype(BF16)
    w_pool_b = w_pool[l].astype(BF16)
    w_out_b = w_out[l].astype(BF16)
    w_up_b = w_up[l].astype(BF16)
    w_down_b = w_down[l].astype(BF16)

    pool_prefix = jnp.pad(state_pool[l], ((0, 0), (POOL_HALO - POOL_STATE, 0), (0, 0)))
    conv_prefix = jnp.pad(state_conv[l], ((0, 0), (CONV_HALO - (CONV_WIDTH - 1), 0), (0, 0)))

    outs = []
    for (x, md, pprefix, cprefix, pos0) in ((x_prompt, mod_p, None, None, 0),
                                            (x_sample, mod_s, pool_prefix, conv_prefix, past_len)):
        nb, t = x.shape[0], x.shape[1]
        res = _inproj(x, md, g_pre_mix[l], w_in_b, w_kt_b, w_pool_b, pool_scale[l], pprefix, pos0=pos0)
        if pprefix is None:
            q, kt, v, kb, vb, yp, pstate = res
            attn = _attn_prompt(q, kb, vb, bias_p, b31, lam, g_head[l])
            k = jnp.transpose(kt.reshape(nb, N_HEADS, 2, HEAD_DIM, t), (0, 4, 1, 2, 3))
        else:
            q, k, v, yp, pstate = res
            attn = _attn_sample(page_table, q, k, v, cache_k[l], cache_v[l], bias_s, b31s, lam, g_head[l])
        y, cstate = _ffn(x, yp, attn, md, g_post_mix[l], g_pre_ffn[l], g_post_ffn[l],
                         w_out_b, w_up_b, conv_w[l], conv_b[l], w_down_b, cprefix)
        outs.append((y,
                     k.reshape(1, nb, t, N_HEADS, 2, HEAD_DIM),
                     v.reshape(1, nb, t, N_HEADS, V_HEAD_DIM),
                     pstate[None, :, POOL_HALO - POOL_STATE:, :],
                     cstate[None, :, CONV_HALO - (CONV_WIDTH - 1):, :]))
    (yp_, kp, vp, pp, cp), (ys_, ks, vs, ps, cs) = outs
    return (yp_, ys_, kp, vp, pp, cp, ks, vs, ps, cs)
```

```python
import functools
import math

import jax
import jax.numpy as jnp
from jax import lax
from jax.experimental import pallas as pl
from jax.experimental.pallas import tpu as pltpu

F32 = jnp.float32
BF16 = jnp.bfloat16

D_MODEL = 1024
POOL_WIDTH = 512
POOL_WINDOWS = (2, 4, 8, 16)
POOL_GROUP_DIM = 128
POOL_STATE = 15
POOL_HALO = 16
N_HEADS = 4
HEAD_DIM = 64
V_HEAD_DIM = 128
QK_WIDTH = 512
ATTN_WIDTH = 512
PROJ_WIDTH = 2048
D_FF = 2816
CONV_WIDTH = 3
CONV_HALO = 8
N_BUCKETS = 32
MAX_EXACT = N_BUCKETS // 2
MAX_DISTANCE = 128
N_MOD = 6
EPS = 1e-6
LAM_INIT = 0.8 - 0.6 * math.exp(-0.3 * 0)
NEG_BIG = -1e30

LANES = 128
MXU_TILE = 256
VMEM_LIMIT = 56 * 1024 * 1024

ATTN_TQ = 256
ATTN_TK = 256
ROW_TILE = 256
FF_CHUNK = 256


def _const_spec(shape):
    nd = len(shape)
    return pl.BlockSpec(shape, lambda *_: (0,) * nd, pipeline_mode=pl.Buffered(1))


def _rms(x, g):
    return x * lax.rsqrt(jnp.mean(x * x, axis=-1, keepdims=True) + EPS) * g


def _bucket(dist):
    d = jnp.maximum(dist, 1).astype(F32)
    large = MAX_EXACT + (jnp.log(d / MAX_EXACT) / math.log(MAX_DISTANCE / MAX_EXACT)
                         * (N_BUCKETS - MAX_EXACT)).astype(jnp.int32)
    large = jnp.minimum(large, N_BUCKETS - 1)
    return jnp.where(dist < MAX_EXACT, dist, large)


def _setup_kernel(rb_ref, lq1_ref, lk1_ref, lq2_ref, lk2_ref,
                  lam_ref, bp_ref, bs_ref, b31s_ref):
    lam = (jnp.exp(jnp.sum(lq1_ref[...] * lk1_ref[...], axis=-1, keepdims=True))
           - jnp.exp(jnp.sum(lq2_ref[...] * lk2_ref[...], axis=-1, keepdims=True)) + LAM_INIT)
    lam_ref[...] = jnp.broadcast_to(lam, lam_ref.shape)

    def lookup(bucket, h):
        out = jnp.full(bucket.shape, rb_ref[N_BUCKETS - 1, h], F32)
        for b in range(N_BUCKETS - 1):
            out = jnp.where(bucket == b, rb_ref[b, h], out)
        return out

    kk = lax.broadcasted_iota(jnp.int32, (ATTN_TK, ATTN_TQ), 0)
    r = lax.broadcasted_iota(jnp.int32, (ATTN_TK, ATTN_TQ), 1)
    for typ in range(2):
        dist = typ * ATTN_TK + r - kk
        bucket = _bucket(jnp.maximum(dist, 0))
        for h in range(N_HEADS):
            bp_ref[h, typ] = jnp.where(dist >= 0, lookup(bucket, h) - rb_ref[N_BUCKETS - 1, h], NEG_BIG)

    rows = 2 * 8
    t = lax.broadcasted_iota(jnp.int32, (rows, 2 * LANES), 0) % 8
    col = lax.broadcasted_iota(jnp.int32, (rows, 2 * LANES), 1)
    dist = jnp.where(col < LANES, LANES + t - col, t - (col - LANES))
    bucket = _bucket(jnp.maximum(dist, 0))
    for h in range(N_HEADS):
        bs_ref[h * rows:(h + 1) * rows, :] = jnp.where(dist >= 0, lookup(bucket, h), NEG_BIG)
        b31s_ref[h * rows:(h + 1) * rows, :] = jnp.full((rows, LANES), rb_ref[N_BUCKETS - 1, h], F32)


def _setup(rel_bias, lq1, lk1, lq2, lk2):
    vspec = pl.BlockSpec(memory_space=pltpu.VMEM)
    return pl.pallas_call(
        _setup_kernel,
        out_shape=(jax.ShapeDtypeStruct((8, LANES), F32),
                   jax.ShapeDtypeStruct((N_HEADS, 2, ATTN_TK, ATTN_TQ), F32),
                   jax.ShapeDtypeStruct((N_HEADS * 16, 2 * LANES), F32),
                   jax.ShapeDtypeStruct((N_HEADS * 16, LANES), F32)),
        in_specs=[pl.BlockSpec(memory_space=pltpu.SMEM), vspec, vspec, vspec, vspec],
        out_specs=(vspec, vspec, vspec, vspec),
        name="setup",
    )(rel_bias, lq1, lk1, lq2, lk2)


def _mod_kernel(c_ref, w_ref, b_ref, o_ref):
    c = c_ref[...]
    s = (c * jax.nn.sigmoid(c)).astype(BF16)
    o_ref[...] = jnp.dot(s, w_ref[...].astype(BF16), preferred_element_type=F32) + b_ref[...]


def _modulation(c_all, w_ada, b_ada):
    n = c_all.shape[0]
    tn = 1024
    return pl.pallas_call(
        _mod_kernel,
        out_shape=jax.ShapeDtypeStruct((n, N_MOD * D_MODEL), F32),
        grid=(N_MOD * D_MODEL // tn,),
        in_specs=[pl.BlockSpec((n, D_MODEL), lambda j: (0, 0)),
                  pl.BlockSpec((D_MODEL, tn), lambda j: (0, j)),
                  pl.BlockSpec((1, tn), lambda j: (0, j))],
        out_specs=pl.BlockSpec((n, tn), lambda j: (0, j)),
        compiler_params=pltpu.CompilerParams(dimension_semantics=("arbitrary",),
                                             vmem_limit_bytes=VMEM_LIMIT),
        name="modulation",
    )(c_all, w_ada, b_ada.reshape(1, -1))


def _inproj_kernel(*refs, carry, pos0, bt, tt):
    if carry:
        (x_ref, mod_ref, g_ref, w_in_ref, w_pool_ref, ps_ref, wkt_ref,
         q_ref, k_ref, v_ref, kb_ref, vb_ref, yp_ref, st_ref, ext_ref, carry_ref) = refs
    else:
        (x_ref, mod_ref, g_ref, w_in_ref, w_pool_ref, ps_ref, prefix_ref,
         q_ref, k_ref, v_ref, yp_ref, st_ref, ext_ref) = refs
    m = bt * tt
    x = x_ref[...]
    h = _rms(x, g_ref[...]) * (1.0 + mod_ref[:, 1:2, :]) + mod_ref[:, 0:1, :]
    h2 = h.reshape(m, D_MODEL).astype(BF16)

    def proj(off, width):
        return jnp.dot(h2, w_in_ref[:, off:off + width], preferred_element_type=F32)

    zq = proj(POOL_WIDTH, QK_WIDTH)
    q_ref[...] = (zq * (HEAD_DIM ** -0.5)).astype(BF16).reshape(bt, tt, QK_WIDTH)
    zv = proj(POOL_WIDTH + 2 * QK_WIDTH, ATTN_WIDTH)
    v_ref[...] = zv.reshape(bt, tt, ATTN_WIDTH)
    if carry:
        k_ref[0] = lax.dot_general(wkt_ref[...], h2, (((1,), (1,)), ((), ())), preferred_element_type=F32)
        kb_ref[...] = proj(POOL_WIDTH + QK_WIDTH, QK_WIDTH).astype(BF16).reshape(bt, tt, QK_WIDTH)
        vb_ref[...] = zv.astype(BF16).reshape(bt, tt, ATTN_WIDTH)
    else:
        k_ref[...] = proj(POOL_WIDTH + QK_WIDTH, QK_WIDTH).reshape(bt, tt, QK_WIDTH)

    u3 = proj(0, POOL_WIDTH).reshape(bt, tt, POOL_WIDTH)
    if carry:
        @pl.when(pl.program_id(1) == 0)
        def _():
            carry_ref[...] = jnp.zeros_like(carry_ref)
        ext_ref[:, 0:POOL_HALO, :] = carry_ref[...]
    else:
        ext_ref[:, 0:POOL_HALO, :] = prefix_ref[...]
    ext_ref[:, POOL_HALO:, :] = u3
    if carry:
        carry_ref[...] = u3[:, tt - POOL_HALO:, :]
        t_base = pl.program_id(1) * tt
    else:
        t_base = 0
    st_ref[...] = ext_ref[:, tt:tt + POOL_HALO, :]

    pos = pos0 + t_base + lax.broadcasted_iota(jnp.int32, (bt, tt, POOL_GROUP_DIM), 1)
    for g, w in enumerate(POOL_WINDOWS):
        sl = slice(g * POOL_GROUP_DIM, (g + 1) * POOL_GROUP_DIM)
        tok = ext_ref[:, POOL_HALO:, sl]
        acc = tok
        for j in range(1, w):
            acc = acc + ext_ref[:, POOL_HALO - j:POOL_HALO - j + tt, sl]
        cnt = jnp.minimum(pos + 1, w).astype(F32)
        d = (acc / cnt - tok).reshape(m, POOL_GROUP_DIM).astype(BF16)
        y = jnp.dot(d, w_pool_ref[g], preferred_element_type=F32) * ps_ref[:, sl]
        yp_ref[:, :, sl] = y.astype(BF16).reshape(bt, tt, POOL_GROUP_DIM)


def _inproj(x, mod, g_pre, w_in, w_kt, w_pool, pool_scale, prefix, *, pos0):
    nb, t, _ = x.shape
    carry = prefix is None
    if carry:
        bt, tt = 1, ROW_TILE
        grid = (nb, t // tt)
        tile = lambda w: pl.BlockSpec((bt, tt, w), lambda b, i: (b, i, 0))
        per_b = lambda r, w: pl.BlockSpec((bt, r, w), lambda b, i: (b, 0, 0))
        sem = ("arbitrary", "arbitrary")
    else:
        bt, tt = ROW_TILE // t, t
        grid = (nb // bt,)
        tile = lambda w: pl.BlockSpec((bt, tt, w), lambda i: (i, 0, 0))
        per_b = lambda r, w: pl.BlockSpec((bt, r, w), lambda i: (i, 0, 0))
        sem = ("arbitrary",)
    in_specs = [tile(D_MODEL), per_b(N_MOD, D_MODEL), _const_spec((1, 1, D_MODEL)),
                _const_spec((D_MODEL, PROJ_WIDTH)),
                _const_spec((len(POOL_WINDOWS), POOL_GROUP_DIM, POOL_GROUP_DIM)),
                _const_spec((1, POOL_WIDTH))]
    args = [x, mod, g_pre.reshape(1, 1, D_MODEL), w_in, w_pool, pool_scale.reshape(1, POOL_WIDTH)]
    scratch = [pltpu.VMEM((bt, POOL_HALO + tt, POOL_WIDTH), F32)]
    st_shape = jax.ShapeDtypeStruct((nb, POOL_HALO, POOL_WIDTH), F32)
    act = lambda w, dt: jax.ShapeDtypeStruct((nb, t, w), dt)
    if carry:
        scratch.append(pltpu.VMEM((1, POOL_HALO, POOL_WIDTH), F32))
        in_specs.append(_const_spec((QK_WIDTH, D_MODEL)))
        args.append(w_kt)
        out_shape = (act(QK_WIDTH, BF16),
                     jax.ShapeDtypeStruct((nb, QK_WIDTH, t), F32),
                     act(ATTN_WIDTH, F32), act(QK_WIDTH, BF16),
                     act(ATTN_WIDTH, BF16), act(POOL_WIDTH, BF16), st_shape)
        out_specs = (tile(QK_WIDTH),
                     pl.BlockSpec((1, QK_WIDTH, tt), lambda b, i: (b, 0, i)),
                     tile(ATTN_WIDTH), tile(QK_WIDTH),
                     tile(ATTN_WIDTH), tile(POOL_WIDTH), per_b(POOL_HALO, POOL_WIDTH))
    else:
        in_specs.append(per_b(POOL_HALO, POOL_WIDTH))
        args.append(prefix)
        out_shape = (act(QK_WIDTH, BF16), act(QK_WIDTH, F32), act(ATTN_WIDTH, F32),
                     act(POOL_WIDTH, BF16), st_shape)
        out_specs = (tile(QK_WIDTH), tile(QK_WIDTH), tile(ATTN_WIDTH), tile(POOL_WIDTH),
                     per_b(POOL_HALO, POOL_WIDTH))
    return pl.pallas_call(
        functools.partial(_inproj_kernel, carry=carry, pos0=pos0, bt=bt, tt=tt),
        out_shape=out_shape, grid=grid, in_specs=in_specs, out_specs=out_specs,
        scratch_shapes=scratch,
        compiler_params=pltpu.CompilerParams(dimension_semantics=sem, vmem_limit_bytes=VMEM_LIMIT),
        name="inproj_prompt" if carry else "inproj_sample",
    )(*args)


def _attn_prompt_kernel(q_ref, kb_ref, vb_ref, bp_ref, lam_ref, gh_ref, o_ref,
                        w_ref, m_ref, l_ref, acc_ref):
    tq, tk = ATTN_TQ, ATTN_TK
    nt = q_ref.shape[1] // tq
    zero = jnp.zeros((HEAD_DIM, tq), BF16)
    for i in range(nt):
        qt = q_ref[0, i * tq:(i + 1) * tq, :].astype(F32).T.astype(BF16)
        w_ref[i] = jnp.concatenate([jnp.concatenate([qt[:HEAD_DIM], zero], axis=1),
                                    jnp.concatenate([zero, qt[HEAD_DIM:]], axis=1)], axis=0)
    lam = lam_ref[0:1, 0:1]
    for j in range(nt):
        k = kb_ref[0, j * tk:(j + 1) * tk, :]
        vt = vb_ref[0, j * tk:(j + 1) * tk, :].astype(F32).T.astype(BF16)
        for i in range(j, nt):
            st = jnp.dot(k, w_ref[i], preferred_element_type=F32)
            if i - j < 2:
                b = bp_ref[0, i - j]
                st = st + jnp.concatenate([b, b], axis=1)
            mx = jnp.max(st, axis=0, keepdims=True)
            if j == 0:
                p = jnp.exp(st - mx)
                l_ref[i] = jnp.sum(p, axis=0, keepdims=True)
                acc_ref[i] = jnp.dot(vt, p.astype(BF16), preferred_element_type=F32)
                m_ref[i] = mx
            else:
                m_old = m_ref[i]
                m_new = jnp.maximum(m_old, mx)
                alpha = jnp.exp(m_old - m_new)
                p = jnp.exp(st - m_new)
                l_ref[i] = alpha * l_ref[i] + jnp.sum(p, axis=0, keepdims=True)
                acc_ref[i] = alpha * acc_ref[i] + jnp.dot(vt, p.astype(BF16), preferred_element_type=F32)
                m_ref[i] = m_new
        on = acc_ref[j] / l_ref[j]
        ot = on[:, :tq] - lam * on[:, tq:]
        ot = ot * lax.rsqrt(jnp.mean(ot * ot, axis=0, keepdims=True) + EPS)
        o_ref[0, j * tq:(j + 1) * tq, :] = (ot.T * gh_ref[...] * (1.0 - LAM_INIT)).astype(BF16)


def _attn_prompt(q, kb, vb, bias_p, lam, g_head):
    nb, t, _ = q.shape
    tq = ATTN_TQ
    nt = t // tq
    hw = V_HEAD_DIM
    head = pl.BlockSpec((1, t, hw), lambda b, h: (b, 0, h))
    return pl.pallas_call(
        _attn_prompt_kernel,
        out_shape=jax.ShapeDtypeStruct((nb, t, ATTN_WIDTH), BF16),
        grid=(nb, N_HEADS),
        in_specs=[head, head, head,
                  pl.BlockSpec((1, 2, ATTN_TK, ATTN_TQ), lambda b, h: (h, 0, 0, 0)),
                  pl.BlockSpec((8, LANES), lambda b, h: (0, 0)),
                  pl.BlockSpec((1, hw), lambda b, h: (0, 0))],
        out_specs=head,
        scratch_shapes=[pltpu.VMEM((nt, 2 * HEAD_DIM, 2 * tq), BF16),
                        pltpu.VMEM((nt, 1, 2 * tq), F32), pltpu.VMEM((nt, 1, 2 * tq), F32),
                        pltpu.VMEM((nt, hw, 2 * tq), F32)],
        compiler_params=pltpu.CompilerParams(
            dimension_semantics=("arbitrary", "arbitrary"), vmem_limit_bytes=VMEM_LIMIT),
        name="attn_prompt",
    )(q, kb, vb, bias_p, lam, g_head.reshape(1, hw))


def _attn_sample_kernel(pt_ref, q_ref, kn_ref, vn_ref, bs_ref, b31s_ref, lam_ref, gh_ref, *rest,
                        n_pages, page):
    k_pages = rest[:n_pages]
    v_pages = rest[n_pages:2 * n_pages]
    o_ref, kbf_ref, vbf_ref = rest[2 * n_pages:]
    past = n_pages * page
    tnew = q_ref.shape[1]
    n_maps = 2 * N_HEADS

    q = q_ref[0].astype(F32)
    qt = jnp.concatenate([q] * n_maps, axis=0)
    rg = lax.broadcasted_iota(jnp.int32, qt.shape, 0) // tnew
    cg = lax.broadcasted_iota(jnp.int32, qt.shape, 1) // HEAD_DIM
    qbd = jnp.where(rg == cg, qt, 0.0).astype(BF16)

    for j in range(n_pages):
        kbf_ref[:, j * page:(j + 1) * page] = k_pages[j][0].astype(BF16)
        for h in range(N_HEADS):
            vbf_ref[j * page:(j + 1) * page, h * V_HEAD_DIM:(h + 1) * V_HEAD_DIM] = (
                v_pages[j][0, pl.ds(h, page, stride=N_HEADS), :].astype(BF16))
    pad = jnp.zeros((LANES - tnew, QK_WIDTH), F32)
    k_new = jnp.concatenate([kn_ref[0], pad], axis=0).astype(BF16)
    vbf_ref[past:, :] = jnp.concatenate([vn_ref[0], pad], axis=0).astype(BF16)

    s_past = jnp.dot(qbd, kbf_ref[...], preferred_element_type=F32)
    s_new = lax.dot_general(qbd, k_new, (((1,), (1,)), ((), ())), preferred_element_type=F32)
    near = past - page
    s = jnp.concatenate([s_past[:, :near] + b31s_ref[:, 0:1],
                         jnp.concatenate([s_past[:, near:], s_new], axis=1) + bs_ref[...]], axis=1)
    mx = jnp.max(s, axis=-1, keepdims=True)
    p = jnp.exp(s - mx)
    pn = p / jnp.sum(p, axis=-1, keepdims=True)
    lam = lam_ref[0:1, 0:1]
    a = jnp.concatenate(
        [pn[2 * h * tnew:(2 * h + 1) * tnew] - lam * pn[(2 * h + 1) * tnew:(2 * h + 2) * tnew]
         for h in range(N_HEADS)], axis=0).astype(BF16)
    r = jnp.dot(a, vbf_ref[...], preferred_element_type=F32)
    outs = []
    for h in range(N_HEADS):
        o = r[h * tnew:(h + 1) * tnew, h * V_HEAD_DIM:(h + 1) * V_HEAD_DIM]
        outs.append(_rms(o, gh_ref[...]) * (1.0 - LAM_INIT))
    o_ref[0] = jnp.concatenate(outs, axis=1).astype(BF16)


def _attn_sample(page_table, q, k_new, v_new, cache_k, cache_v, bias_s, b31s, lam, g_head):
    nb, tnew, _ = q.shape
    n_pages = page_table.shape[1]
    n_phys, page = cache_k.shape[0], cache_k.shape[1]
    ck = jnp.transpose(cache_k, (0, 2, 3, 4, 1)).reshape(n_phys, QK_WIDTH, page)
    cv = cache_v.reshape(n_phys, page * N_HEADS, V_HEAD_DIM)
    tile = lambda w: pl.BlockSpec((1, tnew, w), lambda b, pt: (b, 0, 0))
    const = lambda shape: pl.BlockSpec(shape, lambda b, pt: (0,) * len(shape))

    def page_spec(j, shape):
        return pl.BlockSpec((1,) + shape, lambda b, pt: (pt[b, j], 0, 0))

    in_specs = ([tile(QK_WIDTH), tile(QK_WIDTH), tile(ATTN_WIDTH),
                 const(bias_s.shape), const(b31s.shape), const(lam.shape), const((1, V_HEAD_DIM))]
                + [page_spec(j, (QK_WIDTH, page)) for j in range(n_pages)]
                + [page_spec(j, (page * N_HEADS, V_HEAD_DIM)) for j in range(n_pages)])
    total = n_pages * page + LANES
    return pl.pallas_call(
        functools.partial(_attn_sample_kernel, n_pages=n_pages, page=page),
        out_shape=jax.ShapeDtypeStruct((nb, tnew, ATTN_WIDTH), BF16),
        grid_spec=pltpu.PrefetchScalarGridSpec(
            num_scalar_prefetch=1, grid=(nb,), in_specs=in_specs,
            out_specs=pl.BlockSpec((1, tnew, ATTN_WIDTH), lambda b, pt: (b, 0, 0)),
            scratch_shapes=[pltpu.VMEM((QK_WIDTH, n_pages * page), BF16),
                            pltpu.VMEM((total, ATTN_WIDTH), BF16)]),
        compiler_params=pltpu.CompilerParams(dimension_semantics=("arbitrary",),
                                             vmem_limit_bytes=VMEM_LIMIT),
        name="attn_sample",
    )(page_table, q, k_new, v_new, bias_s, b31s, lam, g_head.reshape(1, V_HEAD_DIM),
      *([ck] * n_pages), *([cv] * n_pages))


def _ffn_kernel(*refs, carry, bt, tt):
    if carry:
        (x_ref, yp_ref, at_ref, mod_ref, gpm_ref, gpf_ref, gpo_ref, w_out_ref, w_up_ref, cw_ref, cb_ref,
         w_down_ref, y_ref, cs_ref, buf_ref, carry_ref) = refs
    else:
        (x_ref, yp_ref, at_ref, mod_ref, gpm_ref, gpf_ref, gpo_ref, w_out_ref, w_up_ref, cw_ref, cb_ref,
         w_down_ref, prefix_ref, y_ref, cs_ref, buf_ref) = refs
    m = bt * tt
    x = x_ref[...]
    mix_in = jnp.concatenate([yp_ref[...], at_ref[...]], axis=-1).reshape(m, D_MODEL)
    mix = jnp.dot(mix_in, w_out_ref[...], preferred_element_type=F32).reshape(bt, tt, D_MODEL)
    x1 = x + mod_ref[:, 2:3, :] * _rms(mix, gpm_ref[...])
    h = _rms(x1, gpf_ref[...]) * (1.0 + mod_ref[:, 4:5, :]) + mod_ref[:, 3:4, :]
    h2 = h.reshape(m, D_MODEL).astype(BF16)

    if carry:
        @pl.when(pl.program_id(1) == 0)
        def _():
            carry_ref[...] = jnp.zeros_like(carry_ref)
        halo_ref = carry_ref
    else:
        halo_ref = prefix_ref

    def conv_part(slot, off):
        sl = slice(off, off + FF_CHUNK)
        u3 = jnp.dot(h2, w_up_ref[:, sl], preferred_element_type=F32).reshape(bt, tt, FF_CHUNK)
        buf_ref[slot, :, 0:CONV_HALO, :] = halo_ref[:, :, sl]
        buf_ref[slot, :, CONV_HALO:, :] = u3
        last = u3[:, tt - CONV_HALO:, :]
        if carry:
            carry_ref[:, :, sl] = last
        cs_ref[:, :, sl] = last
        y = (cb_ref[:, :, sl]
             + cw_ref[0:1, :, sl] * buf_ref[slot, :, CONV_HALO - 2:CONV_HALO - 2 + tt, :]
             + cw_ref[1:2, :, sl] * buf_ref[slot, :, CONV_HALO - 1:CONV_HALO - 1 + tt, :]
             + cw_ref[2:3, :, sl] * u3)
        return y.reshape(m, FF_CHUNK)

    f = jnp.zeros((m, D_MODEL), F32)
    for c in range(D_FF // FF_CHUNK):
        slot = 2 * (c % 2)
        gate = conv_part(slot, c * FF_CHUNK)
        val = conv_part(slot + 1, D_FF + c * FF_CHUNK)
        act = (jax.nn.gelu(gate, approximate=True) * val).astype(BF16)
        f = f + jnp.dot(act, w_down_ref[c * FF_CHUNK:(c + 1) * FF_CHUNK, :], preferred_element_type=F32)
    y_ref[...] = x1 + mod_ref[:, 5:6, :] * _rms(f.reshape(bt, tt, D_MODEL), gpo_ref[...])


def _ffn(x, yp, attn, mod, g_post_mix, g_pre_ffn, g_post_ffn, w_out, w_up, conv_w, conv_b, w_down, prefix):
    nb, t, _ = x.shape
    carry = prefix is None
    if carry:
        bt, tt = 1, ROW_TILE
        grid = (nb, t // tt)
        tile = lambda w: pl.BlockSpec((bt, tt, w), lambda b, i: (b, i, 0))
        per_b = lambda r, w: pl.BlockSpec((bt, r, w), lambda b, i: (b, 0, 0))
        sem = ("arbitrary", "arbitrary")
    else:
        bt, tt = ROW_TILE // t, t
        grid = (nb // bt,)
        tile = lambda w: pl.BlockSpec((bt, tt, w), lambda i: (i, 0, 0))
        per_b = lambda r, w: pl.BlockSpec((bt, r, w), lambda i: (i, 0, 0))
        sem = ("arbitrary",)
    gspec = _const_spec((1, 1, D_MODEL))
    in_specs = [tile(D_MODEL), tile(POOL_WIDTH), tile(ATTN_WIDTH), per_b(N_MOD, D_MODEL),
                gspec, gspec, gspec,
                _const_spec((D_MODEL, D_MODEL)), _const_spec((D_MODEL, 2 * D_FF)),
                _const_spec((CONV_WIDTH, 1, 2 * D_FF)), _const_spec((1, 1, 2 * D_FF)),
                _const_spec((D_FF, D_MODEL))]
    args = [x, yp, attn, mod, g_post_mix.reshape(1, 1, -1), g_pre_ffn.reshape(1, 1, -1),
            g_post_ffn.reshape(1, 1, -1), w_out, w_up, conv_w.reshape(CONV_WIDTH, 1, -1),
            conv_b.reshape(1, 1, -1), w_down]
    scratch = [pltpu.VMEM((4, bt, CONV_HALO + tt, FF_CHUNK), F32)]
    if carry:
        scratch.append(pltpu.VMEM((1, CONV_HALO, 2 * D_FF), F32))
    else:
        in_specs.append(per_b(CONV_HALO, 2 * D_FF))
        args.append(prefix)
    return pl.pallas_call(
        functools.partial(_ffn_kernel, carry=carry, bt=bt, tt=tt),
        out_shape=(jax.ShapeDtypeStruct((nb, t, D_MODEL), F32),
                   jax.ShapeDtypeStruct((nb, CONV_HALO, 2 * D_FF), F32)),
        grid=grid, in_specs=in_specs,
        out_specs=(tile(D_MODEL), per_b(CONV_HALO, 2 * D_FF)),
        scratch_shapes=scratch,
        compiler_params=pltpu.CompilerParams(dimension_semantics=sem, vmem_limit_bytes=VMEM_LIMIT),
        name="ffn_prompt" if carry else "ffn_sample",
    )(*args)


def kernel(x_prompt, x_sample, c_prompt, c_sample, cache_k, cache_v, page_table, state_pool, state_conv,
           w_ada, b_ada, g_pre_mix, g_post_mix, g_pre_ffn, g_post_ffn, w_in, w_out, w_pool, pool_scale,
           lam_q1, lam_k1, lam_q2, lam_k2, g_head, rel_bias, w_up, conv_w, conv_b, w_down):
    nbp, seq, _ = x_prompt.shape
    nbs, dec_seq, _ = x_sample.shape
    past_len = page_table.shape[1] * cache_k.shape[2]
    l = 0

    lam, bias_p, bias_s, b31s = _setup(rel_bias, lam_q1[l:l + 1], lam_k1[l:l + 1],
                                             lam_q2[l:l + 1], lam_k2[l:l + 1])
    mod = _modulation(jnp.concatenate([c_prompt, c_sample], axis=0), w_ada[l], b_ada[l])
    mod = mod.reshape(nbp + nbs, N_MOD, D_MODEL)
    mod_p, mod_s = mod[:nbp], mod[nbp:]

    w_in_b = w_in[l].astype(BF16)
    w_kt_b = w_in[l][:, POOL_WIDTH + QK_WIDTH:POOL_WIDTH + 2 * QK_WIDTH].T.astype(BF16)
    w_pool_b = w_pool[l].astype(BF16)
    w_out_b = w_out[l].astype(BF16)
    w_up_b = w_up[l].astype(BF16)
    w_down_b = w_down[l].astype(BF16)

    pool_prefix = jnp.pad(state_pool[l], ((0, 0), (POOL_HALO - POOL_STATE, 0), (0, 0)))
    conv_prefix = jnp.pad(state_conv[l], ((0, 0), (CONV_HALO - (CONV_WIDTH - 1), 0), (0, 0)))

    outs = []
    for (x, md, pprefix, cprefix, pos0) in ((x_prompt, mod_p, None, None, 0),
                                            (x_sample, mod_s, pool_prefix, conv_prefix, past_len)):
        nb, t = x.shape[0], x.shape[1]
        res = _inproj(x, md, g_pre_mix[l], w_in_b, w_kt_b, w_pool_b, pool_scale[l], pprefix, pos0=pos0)
        if pprefix is None:
            q, kt, v, kb, vb, yp, pstate = res
            attn = _attn_prompt(q, kb, vb, bias_p, lam, g_head[l])
            k = jnp.transpose(kt.reshape(nb, N_HEADS, 2, HEAD_DIM, t), (0, 4, 1, 2, 3))
        else:
            q, k, v, yp, pstate = res
            attn = _attn_sample(page_table, q, k, v, cache_k[l], cache_v[l], bias_s, b31s, lam, g_head[l])
        y, cstate = _ffn(x, yp, attn, md, g_post_mix[l], g_pre_ffn[l], g_post_ffn[l],
                         w_out_b, w_up_b, conv_w[l], conv_b[l], w_down_b, cprefix)
        outs.append((y,
                     k.reshape(1, nb, t, N_HEADS, 2, HEAD_DIM),
                     v.reshape(1, nb, t, N_HEADS, V_HEAD_DIM),
                     pstate[None, :, POOL_HALO - POOL_STATE:, :],
                     cstate[None, :, CONV_HALO - (CONV_WIDTH - 1):, :]))
    (yp_, kp, vp, pp, cp), (ys_, ks, vs, ps, cs) = outs
    return (yp_, ys_, kp, vp, pp, cp, ks, vs, ps, cs)
```

```python
import functools
import math

import jax
import jax.numpy as jnp
from jax import lax
from jax.experimental import pallas as pl
from jax.experimental.pallas import tpu as pltpu

F32 = jnp.float32
BF16 = jnp.bfloat16

D_MODEL = 1024
POOL_WIDTH = 512
POOL_WINDOWS = (2, 4, 8, 16)
POOL_GROUP_DIM = 128
POOL_STATE = 15
POOL_HALO = 16
N_HEADS = 4
HEAD_DIM = 64
V_HEAD_DIM = 128
QK_WIDTH = 512
ATTN_WIDTH = 512
PROJ_WIDTH = 2048
D_FF = 2816
CONV_WIDTH = 3
CONV_HALO = 8
N_BUCKETS = 32
MAX_EXACT = N_BUCKETS // 2
MAX_DISTANCE = 128
N_MOD = 6
EPS = 1e-6
LAM_INIT = 0.8 - 0.6 * math.exp(-0.3 * 0)
NEG_BIG = -1e30
LOG2E = math.log2(math.e)
SUM_ROWS = 16

LANES = 128
MXU_TILE = 256
VMEM_LIMIT = 56 * 1024 * 1024

ATTN_TQ = 256
ATTN_TK = 256
ATTN_LOOKAHEAD = 3
ROW_TILE = 256
FF_CHUNK = 256


def _const_spec(shape):
    nd = len(shape)
    return pl.BlockSpec(shape, lambda *_: (0,) * nd, pipeline_mode=pl.Buffered(1))


def _rms(x, g):
    return x * lax.rsqrt(jnp.mean(x * x, axis=-1, keepdims=True) + EPS) * g


def _bucket(dist):
    d = jnp.maximum(dist, 1).astype(F32)
    large = MAX_EXACT + (jnp.log(d / MAX_EXACT) / math.log(MAX_DISTANCE / MAX_EXACT)
                         * (N_BUCKETS - MAX_EXACT)).astype(jnp.int32)
    large = jnp.minimum(large, N_BUCKETS - 1)
    return jnp.where(dist < MAX_EXACT, dist, large)


def _setup_kernel(rb_ref, lq1_ref, lk1_ref, lq2_ref, lk2_ref,
                  lam_ref, bp_ref, bs_ref, b31s_ref):
    lam = (jnp.exp(jnp.sum(lq1_ref[...] * lk1_ref[...], axis=-1, keepdims=True))
           - jnp.exp(jnp.sum(lq2_ref[...] * lk2_ref[...], axis=-1, keepdims=True)) + LAM_INIT)
    lam_ref[...] = jnp.broadcast_to(lam, lam_ref.shape)

    def lookup(bucket, h):
        out = jnp.full(bucket.shape, rb_ref[N_BUCKETS - 1, h], F32)
        for b in range(N_BUCKETS - 1):
            out = jnp.where(bucket == b, rb_ref[b, h], out)
        return out

    kk = lax.broadcasted_iota(jnp.int32, (ATTN_TK, ATTN_TQ), 0)
    r = lax.broadcasted_iota(jnp.int32, (ATTN_TK, ATTN_TQ), 1)
    for typ in range(2):
        dist = typ * ATTN_TK + r - kk
        bucket = _bucket(jnp.maximum(dist, 0))
        for h in range(N_HEADS):
            shifted = (lookup(bucket, h) - rb_ref[N_BUCKETS - 1, h]) * LOG2E
            bp_ref[h, typ] = jnp.where(dist >= 0, shifted, NEG_BIG)

    rows = 2 * 8
    t = lax.broadcasted_iota(jnp.int32, (rows, 2 * LANES), 0) % 8
    col = lax.broadcasted_iota(jnp.int32, (rows, 2 * LANES), 1)
    dist = jnp.where(col < LANES, LANES + t - col, t - (col - LANES))
    bucket = _bucket(jnp.maximum(dist, 0))
    for h in range(N_HEADS):
        bs_ref[h * rows:(h + 1) * rows, :] = jnp.where(dist >= 0, lookup(bucket, h), NEG_BIG)
        b31s_ref[h * rows:(h + 1) * rows, :] = jnp.full((rows, LANES), rb_ref[N_BUCKETS - 1, h], F32)


def _setup(rel_bias, lq1, lk1, lq2, lk2):
    vspec = pl.BlockSpec(memory_space=pltpu.VMEM)
    return pl.pallas_call(
        _setup_kernel,
        out_shape=(jax.ShapeDtypeStruct((8, LANES), F32),
                   jax.ShapeDtypeStruct((N_HEADS, 2, ATTN_TK, ATTN_TQ), F32),
                   jax.ShapeDtypeStruct((N_HEADS * 16, 2 * LANES), F32),
                   jax.ShapeDtypeStruct((N_HEADS * 16, LANES), F32)),
        in_specs=[pl.BlockSpec(memory_space=pltpu.SMEM), vspec, vspec, vspec, vspec],
        out_specs=(vspec, vspec, vspec, vspec),
        name="setup",
    )(rel_bias, lq1, lk1, lq2, lk2)


def _mod_kernel(c_ref, w_ref, b_ref, o_ref):
    c = c_ref[...]
    s = (c * jax.nn.sigmoid(c)).astype(BF16)
    o_ref[...] = jnp.dot(s, w_ref[...].astype(BF16), preferred_element_type=F32) + b_ref[...]


def _modulation(c_all, w_ada, b_ada):
    n = c_all.shape[0]
    tn = 1024
    return pl.pallas_call(
        _mod_kernel,
        out_shape=jax.ShapeDtypeStruct((n, N_MOD * D_MODEL), F32),
        grid=(N_MOD * D_MODEL // tn,),
        in_specs=[pl.BlockSpec((n, D_MODEL), lambda j: (0, 0)),
                  pl.BlockSpec((D_MODEL, tn), lambda j: (0, j)),
                  pl.BlockSpec((1, tn), lambda j: (0, j))],
        out_specs=pl.BlockSpec((n, tn), lambda j: (0, j)),
        compiler_params=pltpu.CompilerParams(dimension_semantics=("arbitrary",),
                                             vmem_limit_bytes=VMEM_LIMIT),
        name="modulation",
    )(c_all, w_ada, b_ada.reshape(1, -1))


def _inproj_kernel(*refs, carry, pos0, bt, tt):
    if carry:
        (x_ref, mod_ref, g_ref, w_in_ref, w_pool_ref, ps_ref, wkt_ref,
         q_ref, k_ref, v_ref, kb_ref, vb_ref, yp_ref, st_ref, ext_ref, carry_ref) = refs
    else:
        (x_ref, mod_ref, g_ref, w_in_ref, w_pool_ref, ps_ref, prefix_ref,
         q_ref, k_ref, v_ref, yp_ref, st_ref, ext_ref) = refs
    m = bt * tt
    x = x_ref[...]
    h = _rms(x, g_ref[...]) * (1.0 + mod_ref[:, 1:2, :]) + mod_ref[:, 0:1, :]
    h2 = h.reshape(m, D_MODEL).astype(BF16)

    def proj(off, width):
        return jnp.dot(h2, w_in_ref[:, off:off + width], preferred_element_type=F32)

    zq = proj(POOL_WIDTH, QK_WIDTH)
    q_scale = HEAD_DIM ** -0.5 * (LOG2E if carry else 1.0)
    q_ref[...] = (zq * q_scale).astype(BF16).reshape(bt, tt, QK_WIDTH)
    zv = proj(POOL_WIDTH + 2 * QK_WIDTH, ATTN_WIDTH)
    v_ref[...] = zv.reshape(bt, tt, ATTN_WIDTH)
    if carry:
        k_ref[0] = lax.dot_general(wkt_ref[...], h2, (((1,), (1,)), ((), ())), preferred_element_type=F32)
        kb_ref[...] = proj(POOL_WIDTH + QK_WIDTH, QK_WIDTH).astype(BF16).reshape(bt, tt, QK_WIDTH)
        vb_ref[...] = zv.astype(BF16).reshape(bt, tt, ATTN_WIDTH)
    else:
        k_ref[...] = proj(POOL_WIDTH + QK_WIDTH, QK_WIDTH).reshape(bt, tt, QK_WIDTH)

    u3 = proj(0, POOL_WIDTH).reshape(bt, tt, POOL_WIDTH)
    if carry:
        @pl.when(pl.program_id(1) == 0)
        def _():
            carry_ref[...] = jnp.zeros_like(carry_ref)
        ext_ref[:, 0:POOL_HALO, :] = carry_ref[...]
    else:
        ext_ref[:, 0:POOL_HALO, :] = prefix_ref[...]
    ext_ref[:, POOL_HALO:, :] = u3
    if carry:
        carry_ref[...] = u3[:, tt - POOL_HALO:, :]
        t_base = pl.program_id(1) * tt
    else:
        t_base = 0
    st_ref[...] = ext_ref[:, tt:tt + POOL_HALO, :]

    pos = pos0 + t_base + lax.broadcasted_iota(jnp.int32, (bt, tt, POOL_GROUP_DIM), 1)
    for g, w in enumerate(POOL_WINDOWS):
        sl = slice(g * POOL_GROUP_DIM, (g + 1) * POOL_GROUP_DIM)
        tok = ext_ref[:, POOL_HALO:, sl]
        acc = tok
        for j in range(1, w):
            acc = acc + ext_ref[:, POOL_HALO - j:POOL_HALO - j + tt, sl]
        cnt = jnp.minimum(pos + 1, w).astype(F32)
        d = (acc / cnt - tok).reshape(m, POOL_GROUP_DIM).astype(BF16)
        y = jnp.dot(d, w_pool_ref[g], preferred_element_type=F32) * ps_ref[:, sl]
        yp_ref[:, :, sl] = y.astype(BF16).reshape(bt, tt, POOL_GROUP_DIM)


def _inproj(x, mod, g_pre, w_in, w_kt, w_pool, pool_scale, prefix, *, pos0):
    nb, t, _ = x.shape
    carry = prefix is None
    if carry:
        bt, tt = 1, ROW_TILE
        grid = (nb, t // tt)
        tile = lambda w: pl.BlockSpec((bt, tt, w), lambda b, i: (b, i, 0))
        per_b = lambda r, w: pl.BlockSpec((bt, r, w), lambda b, i: (b, 0, 0))
        sem = ("arbitrary", "arbitrary")
    else:
        bt, tt = ROW_TILE // t, t
        grid = (nb // bt,)
        tile = lambda w: pl.BlockSpec((bt, tt, w), lambda i: (i, 0, 0))
        per_b = lambda r, w: pl.BlockSpec((bt, r, w), lambda i: (i, 0, 0))
        sem = ("arbitrary",)
    in_specs = [tile(D_MODEL), per_b(N_MOD, D_MODEL), _const_spec((1, 1, D_MODEL)),
                _const_spec((D_MODEL, PROJ_WIDTH)),
                _const_spec((len(POOL_WINDOWS), POOL_GROUP_DIM, POOL_GROUP_DIM)),
                _const_spec((1, POOL_WIDTH))]
    args = [x, mod, g_pre.reshape(1, 1, D_MODEL), w_in, w_pool, pool_scale.reshape(1, POOL_WIDTH)]
    scratch = [pltpu.VMEM((bt, POOL_HALO + tt, POOL_WIDTH), F32)]
    st_shape = jax.ShapeDtypeStruct((nb, POOL_HALO, POOL_WIDTH), F32)
    act = lambda w, dt: jax.ShapeDtypeStruct((nb, t, w), dt)
    if carry:
        scratch.append(pltpu.VMEM((1, POOL_HALO, POOL_WIDTH), F32))
        in_specs.append(_const_spec((QK_WIDTH, D_MODEL)))
        args.append(w_kt)
        out_shape = (act(QK_WIDTH, BF16),
                     jax.ShapeDtypeStruct((nb, QK_WIDTH, t), F32),
                     act(ATTN_WIDTH, F32), act(QK_WIDTH, BF16),
                     act(ATTN_WIDTH, BF16), act(POOL_WIDTH, BF16), st_shape)
        out_specs = (tile(QK_WIDTH),
                     pl.BlockSpec((1, QK_WIDTH, tt), lambda b, i: (b, 0, i)),
                     tile(ATTN_WIDTH), tile(QK_WIDTH),
                     tile(ATTN_WIDTH), tile(POOL_WIDTH), per_b(POOL_HALO, POOL_WIDTH))
    else:
        in_specs.append(per_b(POOL_HALO, POOL_WIDTH))
        args.append(prefix)
        out_shape = (act(QK_WIDTH, BF16), act(QK_WIDTH, F32), act(ATTN_WIDTH, F32),
                     act(POOL_WIDTH, BF16), st_shape)
        out_specs = (tile(QK_WIDTH), tile(QK_WIDTH), tile(ATTN_WIDTH), tile(POOL_WIDTH),
                     per_b(POOL_HALO, POOL_WIDTH))
    return pl.pallas_call(
        functools.partial(_inproj_kernel, carry=carry, pos0=pos0, bt=bt, tt=tt),
        out_shape=out_shape, grid=grid, in_specs=in_specs, out_specs=out_specs,
        scratch_shapes=scratch,
        compiler_params=pltpu.CompilerParams(dimension_semantics=sem, vmem_limit_bytes=VMEM_LIMIT),
        name="inproj_prompt" if carry else "inproj_sample",
    )(*args)


def _attn_prompt_kernel(q_ref, kb_ref, vb_ref, bp_ref, lam_ref, gh_ref, o_ref,
                        w_ref, m_ref, acc_ref):
    tq, tk = ATTN_TQ, ATTN_TK
    nt = q_ref.shape[1] // tq
    zero = jnp.zeros((HEAD_DIM, tq), BF16)
    for i in range(nt):
        qt = q_ref[0, i * tq:(i + 1) * tq, :].astype(F32).T.astype(BF16)
        w_ref[i] = jnp.concatenate([jnp.concatenate([qt[:HEAD_DIM], zero], axis=1),
                                    jnp.concatenate([zero, qt[HEAD_DIM:]], axis=1)], axis=0)
    lam = lam_ref[0:1, 0:1]

    def scores(i, j):
        return jnp.dot(kb_ref[0, j * tk:(j + 1) * tk, :], w_ref[i], preferred_element_type=F32)

    units = [(i, j) for j in range(nt) for i in range(j, nt)]
    pending = [scores(*u) for u in units[:ATTN_LOOKAHEAD]]
    vt = None
    for n, (i, j) in enumerate(units):
        st = pending.pop(0)
        if n + ATTN_LOOKAHEAD < len(units):
            pending.append(scores(*units[n + ATTN_LOOKAHEAD]))
        if i == j:
            vt = jnp.concatenate([vb_ref[0, j * tk:(j + 1) * tk, :].astype(F32).T,
                                  jnp.ones((SUM_ROWS, tk), F32)], axis=0).astype(BF16)
        if i - j < 2:
            b = bp_ref[0, i - j]
            st = st + jnp.concatenate([b, b], axis=1)
        mx = jnp.max(st, axis=0, keepdims=True)
        if j == 0:
            p = jnp.exp2(st - mx)
            acc_ref[i] = jnp.dot(vt, p.astype(BF16), preferred_element_type=F32)
            m_ref[i] = mx
        else:
            m_old = m_ref[i]
            m_new = jnp.maximum(m_old, mx)
            alpha = jnp.exp2(m_old - m_new)
            p = jnp.exp2(st - m_new)
            acc_ref[i] = alpha * acc_ref[i] + jnp.dot(vt, p.astype(BF16), preferred_element_type=F32)
            m_ref[i] = m_new
        if i == j:
            acc = acc_ref[j]
            on = acc[:V_HEAD_DIM] / acc[V_HEAD_DIM:V_HEAD_DIM + 1]
            ot = on[:, :tq] - lam * on[:, tq:]
            ot = ot * lax.rsqrt(jnp.mean(ot * ot, axis=0, keepdims=True) + EPS)
            o_ref[0, j * tq:(j + 1) * tq, :] = (ot.T * gh_ref[...] * (1.0 - LAM_INIT)).astype(BF16)


def _attn_prompt(q, kb, vb, bias_p, lam, g_head):
    nb, t, _ = q.shape
    tq = ATTN_TQ
    nt = t // tq
    hw = V_HEAD_DIM
    head = pl.BlockSpec((1, t, hw), lambda b, h: (b, 0, h))
    return pl.pallas_call(
        _attn_prompt_kernel,
        out_shape=jax.ShapeDtypeStruct((nb, t, ATTN_WIDTH), BF16),
        grid=(nb, N_HEADS),
        in_specs=[head, head, head,
                  pl.BlockSpec((1, 2, ATTN_TK, ATTN_TQ), lambda b, h: (h, 0, 0, 0)),
                  pl.BlockSpec((8, LANES), lambda b, h: (0, 0)),
                  pl.BlockSpec((1, hw), lambda b, h: (0, 0))],
        out_specs=head,
        scratch_shapes=[pltpu.VMEM((nt, 2 * HEAD_DIM, 2 * tq), BF16),
                        pltpu.VMEM((nt, 1, 2 * tq), F32),
                        pltpu.VMEM((nt, hw + SUM_ROWS, 2 * tq), F32)],
        compiler_params=pltpu.CompilerParams(
            dimension_semantics=("arbitrary", "arbitrary"), vmem_limit_bytes=VMEM_LIMIT),
        name="attn_prompt",
    )(q, kb, vb, bias_p, lam, g_head.reshape(1, hw))


def _attn_sample_kernel(pt_ref, q_ref, kn_ref, vn_ref, bs_ref, b31s_ref, lam_ref, gh_ref, *rest,
                        n_pages, page):
    k_pages = rest[:n_pages]
    v_pages = rest[n_pages:2 * n_pages]
    o_ref, kbf_ref, vbf_ref = rest[2 * n_pages:]
    past = n_pages * page
    tnew = q_ref.shape[1]
    n_maps = 2 * N_HEADS

    q = q_ref[0].astype(F32)
    qt = jnp.concatenate([q] * n_maps, axis=0)
    rg = lax.broadcasted_iota(jnp.int32, qt.shape, 0) // tnew
    cg = lax.broadcasted_iota(jnp.int32, qt.shape, 1) // HEAD_DIM
    qbd = jnp.where(rg == cg, qt, 0.0).astype(BF16)

    for j in range(n_pages):
        kbf_ref[:, j * page:(j + 1) * page] = k_pages[j][0].astype(BF16)
        for h in range(N_HEADS):
            vbf_ref[j * page:(j + 1) * page, h * V_HEAD_DIM:(h + 1) * V_HEAD_DIM] = (
                v_pages[j][0, pl.ds(h, page, stride=N_HEADS), :].astype(BF16))
    pad = jnp.zeros((LANES - tnew, QK_WIDTH), F32)
    k_new = jnp.concatenate([kn_ref[0], pad], axis=0).astype(BF16)
    vbf_ref[past:, :] = jnp.concatenate([vn_ref[0], pad], axis=0).astype(BF16)

    s_past = jnp.dot(qbd, kbf_ref[...], preferred_element_type=F32)
    s_new = lax.dot_general(qbd, k_new, (((1,), (1,)), ((), ())), preferred_element_type=F32)
    near = past - page
    s = jnp.concatenate([s_past[:, :near] + b31s_ref[:, 0:1],
                         jnp.concatenate([s_past[:, near:], s_new], axis=1) + bs_ref[...]], axis=1)
    mx = jnp.max(s, axis=-1, keepdims=True)
    p = jnp.exp(s - mx)
    pn = p / jnp.sum(p, axis=-1, keepdims=True)
    lam = lam_ref[0:1, 0:1]
    a = jnp.concatenate(
        [pn[2 * h * tnew:(2 * h + 1) * tnew] - lam * pn[(2 * h + 1) * tnew:(2 * h + 2) * tnew]
         for h in range(N_HEADS)], axis=0).astype(BF16)
    r = jnp.dot(a, vbf_ref[...], preferred_element_type=F32)
    outs = []
    for h in range(N_HEADS):
        o = r[h * tnew:(h + 1) * tnew, h * V_HEAD_DIM:(h + 1) * V_HEAD_DIM]
        outs.append(_rms(o, gh_ref[...]) * (1.0 - LAM_INIT))
    o_ref[0] = jnp.concatenate(outs, axis=1).astype(BF16)


def _attn_sample(page_table, q, k_new, v_new, cache_k, cache_v, bias_s, b31s, lam, g_head):
    nb, tnew, _ = q.shape
    n_pages = page_table.shape[1]
    n_phys, page = cache_k.shape[0], cache_k.shape[1]
    ck = jnp.transpose(cache_k, (0, 2, 3, 4, 1)).reshape(n_phys, QK_WIDTH, page)
    cv = cache_v.reshape(n_phys, page * N_HEADS, V_HEAD_DIM)
    tile = lambda w: pl.BlockSpec((1, tnew, w), lambda b, pt: (b, 0, 0))
    const = lambda shape: pl.BlockSpec(shape, lambda b, pt: (0,) * len(shape))

    def page_spec(j, shape):
        return pl.BlockSpec((1,) + shape, lambda b, pt: (pt[b, j], 0, 0))

    in_specs = ([tile(QK_WIDTH), tile(QK_WIDTH), tile(ATTN_WIDTH),
                 const(bias_s.shape), const(b31s.shape), const(lam.shape), const((1, V_HEAD_DIM))]
                + [page_spec(j, (QK_WIDTH, page)) for j in range(n_pages)]
                + [page_spec(j, (page * N_HEADS, V_HEAD_DIM)) for j in range(n_pages)])
    total = n_pages * page + LANES
    return pl.pallas_call(
        functools.partial(_attn_sample_kernel, n_pages=n_pages, page=page),
        out_shape=jax.ShapeDtypeStruct((nb, tnew, ATTN_WIDTH), BF16),
        grid_spec=pltpu.PrefetchScalarGridSpec(
            num_scalar_prefetch=1, grid=(nb,), in_specs=in_specs,
            out_specs=pl.BlockSpec((1, tnew, ATTN_WIDTH), lambda b, pt: (b, 0, 0)),
            scratch_shapes=[pltpu.VMEM((QK_WIDTH, n_pages * page), BF16),
                            pltpu.VMEM((total, ATTN_WIDTH), BF16)]),
        compiler_params=pltpu.CompilerParams(dimension_semantics=("arbitrary",),
                                             vmem_limit_bytes=VMEM_LIMIT),
        name="attn_sample",
    )(page_table, q, k_new, v_new, bias_s, b31s, lam, g_head.reshape(1, V_HEAD_DIM),
      *([ck] * n_pages), *([cv] * n_pages))


def _ffn_kernel(*refs, carry, bt, tt):
    if carry:
        (x_ref, yp_ref, at_ref, mod_ref, gpm_ref, gpf_ref, gpo_ref, w_out_ref, w_up_ref, cw_ref, cb_ref,
         w_down_ref, y_ref, cs_ref, *buf_refs, carry_ref) = refs
    else:
        (x_ref, yp_ref, at_ref, mod_ref, gpm_ref, gpf_ref, gpo_ref, w_out_ref, w_up_ref, cw_ref, cb_ref,
         w_down_ref, prefix_ref, y_ref, cs_ref, *buf_refs) = refs
    m = bt * tt
    x = x_ref[...]
    mix_in = jnp.concatenate([yp_ref[...], at_ref[...]], axis=-1).reshape(m, D_MODEL)
    mix = jnp.dot(mix_in, w_out_ref[...], preferred_element_type=F32).reshape(bt, tt, D_MODEL)
    x1 = x + mod_ref[:, 2:3, :] * _rms(mix, gpm_ref[...])
    h = _rms(x1, gpf_ref[...]) * (1.0 + mod_ref[:, 4:5, :]) + mod_ref[:, 3:4, :]
    h2 = h.reshape(m, D_MODEL).astype(BF16)

    if carry:
        @pl.when(pl.program_id(1) == 0)
        def _():
            carry_ref[...] = jnp.zeros_like(carry_ref)
        halo_ref = carry_ref
    else:
        halo_ref = prefix_ref

    def up_part(slot, off):
        sl = slice(off, off + FF_CHUNK)
        buf_ref = buf_refs[slot]
        u3 = jnp.dot(h2, w_up_ref[:, sl], preferred_element_type=F32).reshape(bt, tt, FF_CHUNK)
        buf_ref[:, 0:CONV_HALO, :] = halo_ref[:, :, sl]
        buf_ref[:, CONV_HALO:, :] = u3
        cs_ref[:, :, sl] = u3[:, tt - CONV_HALO:, :]

    def conv_part(slot, off):
        sl = slice(off, off + FF_CHUNK)
        buf_ref = buf_refs[slot]
        y = (cb_ref[:, :, sl]
             + cw_ref[0:1, :, sl] * buf_ref[:, CONV_HALO - 2:CONV_HALO - 2 + tt, :]
             + cw_ref[1:2, :, sl] * buf_ref[:, CONV_HALO - 1:CONV_HALO - 1 + tt, :]
             + cw_ref[2:3, :, sl] * buf_ref[:, CONV_HALO:, :])
        return y.reshape(m, FF_CHUNK)

    def up_chunk(c):
        up_part(2 * (c % 2), c * FF_CHUNK)
        up_part(2 * (c % 2) + 1, D_FF + c * FF_CHUNK)

    n_chunks = D_FF // FF_CHUNK
    f = jnp.zeros((m, D_MODEL), F32)
    up_chunk(0)
    for c in range(n_chunks):
        if c + 1 < n_chunks:
            up_chunk(c + 1)
        gate = conv_part(2 * (c % 2), c * FF_CHUNK)
        val = conv_part(2 * (c % 2) + 1, D_FF + c * FF_CHUNK)
        act = (jax.nn.gelu(gate, approximate=True) * val).astype(BF16)
        f = f + jnp.dot(act, w_down_ref[c * FF_CHUNK:(c + 1) * FF_CHUNK, :], preferred_element_type=F32)
    if carry:
        carry_ref[...] = cs_ref[...]
    y_ref[...] = x1 + mod_ref[:, 5:6, :] * _rms(f.reshape(bt, tt, D_MODEL), gpo_ref[...])


def _ffn(x, yp, attn, mod, g_post_mix, g_pre_ffn, g_post_ffn, w_out, w_up, conv_w, conv_b, w_down, prefix):
    nb, t, _ = x.shape
    carry = prefix is None
    if carry:
        bt, tt = 1, ROW_TILE
        grid = (nb, t // tt)
        tile = lambda w: pl.BlockSpec((bt, tt, w), lambda b, i: (b, i, 0))
        per_b = lambda r, w: pl.BlockSpec((bt, r, w), lambda b, i: (b, 0, 0))
        sem = ("arbitrary", "arbitrary")
    else:
        bt, tt = ROW_TILE // t, t
        grid = (nb // bt,)
        tile = lambda w: pl.BlockSpec((bt, tt, w), lambda i: (i, 0, 0))
        per_b = lambda r, w: pl.BlockSpec((bt, r, w), lambda i: (i, 0, 0))
        sem = ("arbitrary",)
    gspec = _const_spec((1, 1, D_MODEL))
    in_specs = [tile(D_MODEL), tile(POOL_WIDTH), tile(ATTN_WIDTH), per_b(N_MOD, D_MODEL),
                gspec, gspec, gspec,
                _const_spec((D_MODEL, D_MODEL)), _const_spec((D_MODEL, 2 * D_FF)),
                _const_spec((CONV_WIDTH, 1, 2 * D_FF)), _const_spec((1, 1, 2 * D_FF)),
                _const_spec((D_FF, D_MODEL))]
    args = [x, yp, attn, mod, g_post_mix.reshape(1, 1, -1), g_pre_ffn.reshape(1, 1, -1),
            g_post_ffn.reshape(1, 1, -1), w_out, w_up, conv_w.reshape(CONV_WIDTH, 1, -1),
            conv_b.reshape(1, 1, -1), w_down]
    scratch = [pltpu.VMEM((bt, CONV_HALO + tt, FF_CHUNK), F32) for _ in range(4)]
    if carry:
        scratch.append(pltpu.VMEM((1, CONV_HALO, 2 * D_FF), F32))
    else:
        in_specs.append(per_b(CONV_HALO, 2 * D_FF))
        args.append(prefix)
    return pl.pallas_call(
        functools.partial(_ffn_kernel, carry=carry, bt=bt, tt=tt),
        out_shape=(jax.ShapeDtypeStruct((nb, t, D_MODEL), F32),
                   jax.ShapeDtypeStruct((nb, CONV_HALO, 2 * D_FF), F32)),
        grid=grid, in_specs=in_specs,
        out_specs=(tile(D_MODEL), per_b(CONV_HALO, 2 * D_FF)),
        scratch_shapes=scratch,
        compiler_params=pltpu.CompilerParams(dimension_semantics=sem, vmem_limit_bytes=VMEM_LIMIT),
        name="ffn_prompt" if carry else "ffn_sample",
    )(*args)


def kernel(x_prompt, x_sample, c_prompt, c_sample, cache_k, cache_v, page_table, state_pool, state_conv,
           w_ada, b_ada, g_pre_mix, g_post_mix, g_pre_ffn, g_post_ffn, w_in, w_out, w_pool, pool_scale,
           lam_q1, lam_k1, lam_q2, lam_k2, g_head, rel_bias, w_up, conv_w, conv_b, w_down):
    nbp, seq, _ = x_prompt.shape
    nbs, dec_seq, _ = x_sample.shape
    past_len = page_table.shape[1] * cache_k.shape[2]
    l = 0

    lam, bias_p, bias_s, b31s = _setup(rel_bias, lam_q1[l:l + 1], lam_k1[l:l + 1],
                                             lam_q2[l:l + 1], lam_k2[l:l + 1])
    mod = _modulation(jnp.concatenate([c_prompt, c_sample], axis=0), w_ada[l], b_ada[l])
    mod = mod.reshape(nbp + nbs, N_MOD, D_MODEL)
    mod_p, mod_s = mod[:nbp], mod[nbp:]

    w_in_b = w_in[l].astype(BF16)
    w_kt_b = w_in[l][:, POOL_WIDTH + QK_WIDTH:POOL_WIDTH + 2 * QK_WIDTH].T.astype(BF16)
    w_pool_b = w_pool[l].astype(BF16)
    w_out_b = w_out[l].astype(BF16)
    w_up_b = w_up[l].astype(BF16)
    w_down_b = w_down[l].astype(BF16)

    pool_prefix = jnp.pad(state_pool[l], ((0, 0), (POOL_HALO - POOL_STATE, 0), (0, 0)))
    conv_prefix = jnp.pad(state_conv[l], ((0, 0), (CONV_HALO - (CONV_WIDTH - 1), 0), (0, 0)))

    outs = []
    for (x, md, pprefix, cprefix, pos0) in ((x_prompt, mod_p, None, None, 0),
                                            (x_sample, mod_s, pool_prefix, conv_prefix, past_len)):
        nb, t = x.shape[0], x.shape[1]
        res = _inproj(x, md, g_pre_mix[l], w_in_b, w_kt_b, w_pool_b, pool_scale[l], pprefix, pos0=pos0)
        if pprefix is None:
            q, kt, v, kb, vb, yp, pstate = res
            attn = _attn_prompt(q, kb, vb, bias_p, lam, g_head[l])
            k = jnp.transpose(kt.reshape(nb, N_HEADS, 2, HEAD_DIM, t), (0, 4, 1, 2, 3))
        else:
            q, k, v, yp, pstate = res
            attn = _attn_sample(page_table, q, k, v, cache_k[l], cache_v[l], bias_s, b31s, lam, g_head[l])
        y, cstate = _ffn(x, yp, attn, md, g_post_mix[l], g_pre_ffn[l], g_post_ffn[l],
                         w_out_b, w_up_b, conv_w[l], conv_b[l], w_down_b, cprefix)
        outs.append((y,
                     k.reshape(1, nb, t, N_HEADS, 2, HEAD_DIM),
                     v.reshape(1, nb, t, N_HEADS, V_HEAD_DIM),
                     pstate[None, :, POOL_HALO - POOL_STATE:, :],
                     cstate[None, :, CONV_HALO - (CONV_WIDTH - 1):, :]))
    (yp_, kp, vp, pp, cp), (ys_, ks, vs, ps, cs) = outs
    return (yp_, ys_, kp, vp, pp, cp, ks, vs, ps, cs)
```

```python
import functools
import math

import jax
import jax.numpy as jnp
from jax import lax
from jax.experimental import pallas as pl
from jax.experimental.pallas import tpu as pltpu

F32 = jnp.float32
BF16 = jnp.bfloat16

D_MODEL = 1024
POOL_WIDTH = 512
POOL_WINDOWS = (2, 4, 8, 16)
POOL_GROUP_DIM = 128
POOL_STATE = 15
POOL_HALO = 16
N_HEADS = 4
HEAD_DIM = 64
V_HEAD_DIM = 128
QK_WIDTH = 512
ATTN_WIDTH = 512
PROJ_WIDTH = 2048
D_FF = 2816
CONV_WIDTH = 3
CONV_HALO = 8
N_BUCKETS = 32
MAX_EXACT = N_BUCKETS // 2
MAX_DISTANCE = 128
N_MOD = 6
EPS = 1e-6
LAM_INIT = 0.8 - 0.6 * math.exp(-0.3 * 0)
NEG_BIG = -1e30
LOG2E = math.log2(math.e)
SUM_ROWS = 16

LANES = 128
MXU_TILE = 256
VMEM_LIMIT = 56 * 1024 * 1024

ATTN_TQ = 256
ATTN_TK = 256
ATTN_LOOKAHEAD = 3
ROW_TILE = 256
FF_CHUNK = 256
FFN_STREAMS = 2


def _const_spec(shape):
    nd = len(shape)
    return pl.BlockSpec(shape, lambda *_: (0,) * nd, pipeline_mode=pl.Buffered(1))


def _rms(x, g):
    return x * lax.rsqrt(jnp.mean(x * x, axis=-1, keepdims=True) + EPS) * g


def _gelu_tanh(x):
    c0 = -2.0 * math.sqrt(2.0 / math.pi) * LOG2E
    return x / (1.0 + jnp.exp2(x * (c0 + (c0 * 0.044715) * (x * x))))


def _bucket(dist):
    d = jnp.maximum(dist, 1).astype(F32)
    large = MAX_EXACT + jnp.floor(jnp.log(d / MAX_EXACT) / math.log(MAX_DISTANCE / MAX_EXACT)
                                  * (N_BUCKETS - MAX_EXACT)).astype(jnp.int32)
    large = jnp.minimum(large, N_BUCKETS - 1)
    return jnp.where(dist < MAX_EXACT, dist, large)


def _setup_kernel(rb_ref, lq1_ref, lk1_ref, lq2_ref, lk2_ref,
                  lam_ref, bp_ref, bs_ref, b31s_ref):
    lam = (jnp.exp(jnp.sum(lq1_ref[...] * lk1_ref[...], axis=-1, keepdims=True))
           - jnp.exp(jnp.sum(lq2_ref[...] * lk2_ref[...], axis=-1, keepdims=True)) + LAM_INIT)
    lam_ref[...] = jnp.broadcast_to(lam, lam_ref.shape)

    def lookup(bucket, h):
        out = jnp.full(bucket.shape, rb_ref[N_BUCKETS - 1, h], F32)
        for b in range(N_BUCKETS - 1):
            out = jnp.where(bucket == b, rb_ref[b, h], out)
        return out

    kk = lax.broadcasted_iota(jnp.int32, (ATTN_TK, ATTN_TQ), 0)
    r = lax.broadcasted_iota(jnp.int32, (ATTN_TK, ATTN_TQ), 1)
    for typ in range(2):
        dist = typ * ATTN_TK + r - kk
        bucket = _bucket(jnp.maximum(dist, 0))
        for h in range(N_HEADS):
            shifted = (lookup(bucket, h) - rb_ref[N_BUCKETS - 1, h]) * LOG2E
            bp_ref[h, typ] = jnp.where(dist >= 0, shifted, NEG_BIG)

    rows = 2 * 8
    t = lax.broadcasted_iota(jnp.int32, (rows, 2 * LANES), 0) % 8
    col = lax.broadcasted_iota(jnp.int32, (rows, 2 * LANES), 1)
    dist = jnp.where(col < LANES, LANES + t - col, t - (col - LANES))
    bucket = _bucket(jnp.maximum(dist, 0))
    for h in range(N_HEADS):
        bs_ref[h * rows:(h + 1) * rows, :] = jnp.where(dist >= 0, lookup(bucket, h), NEG_BIG)
        b31s_ref[h * rows:(h + 1) * rows, :] = jnp.full((rows, LANES), rb_ref[N_BUCKETS - 1, h], F32)


def _setup(rel_bias, lq1, lk1, lq2, lk2):
    vspec = pl.BlockSpec(memory_space=pltpu.VMEM)
    return pl.pallas_call(
        _setup_kernel,
        out_shape=(jax.ShapeDtypeStruct((8, LANES), F32),
                   jax.ShapeDtypeStruct((N_HEADS, 2, ATTN_TK, ATTN_TQ), F32),
                   jax.ShapeDtypeStruct((N_HEADS * 16, 2 * LANES), F32),
                   jax.ShapeDtypeStruct((N_HEADS * 16, LANES), F32)),
        in_specs=[pl.BlockSpec(memory_space=pltpu.SMEM), vspec, vspec, vspec, vspec],
        out_specs=(vspec, vspec, vspec, vspec),
        name="setup",
    )(rel_bias, lq1, lk1, lq2, lk2)


def _mod_kernel(c_ref, w_ref, b_ref, o_ref):
    c = c_ref[...]
    s = (c * jax.nn.sigmoid(c)).astype(BF16)
    o_ref[...] = jnp.dot(s, w_ref[...].astype(BF16), preferred_element_type=F32) + b_ref[...]


def _modulation(c_all, w_ada, b_ada):
    n = c_all.shape[0]
    tn = 1024
    return pl.pallas_call(
        _mod_kernel,
        out_shape=jax.ShapeDtypeStruct((n, N_MOD * D_MODEL), F32),
        grid=(N_MOD * D_MODEL // tn,),
        in_specs=[pl.BlockSpec((n, D_MODEL), lambda j: (0, 0)),
                  pl.BlockSpec((D_MODEL, tn), lambda j: (0, j)),
                  pl.BlockSpec((1, tn), lambda j: (0, j))],
        out_specs=pl.BlockSpec((n, tn), lambda j: (0, j)),
        compiler_params=pltpu.CompilerParams(dimension_semantics=("arbitrary",),
                                             vmem_limit_bytes=VMEM_LIMIT),
        name="modulation",
    )(c_all, w_ada, b_ada.reshape(1, -1))


def _inproj_kernel(*refs, carry, pos0, bt, tt):
    if carry:
        (x_ref, mod_ref, g_ref, w_in_ref, w_pool_ref, ps_ref, wkt_ref,
         q_ref, k_ref, v_ref, kb_ref, vb_ref, yp_ref, st_ref, ext_ref, carry_ref) = refs
    else:
        (x_ref, mod_ref, g_ref, w_in_ref, w_pool_ref, ps_ref, prefix_ref,
         q_ref, k_ref, v_ref, yp_ref, st_ref, ext_ref) = refs
    m = bt * tt
    x = x_ref[...]
    h = _rms(x, g_ref[...]) * (1.0 + mod_ref[:, 1:2, :]) + mod_ref[:, 0:1, :]
    h2 = h.reshape(m, D_MODEL).astype(BF16)

    def proj(off, width):
        return jnp.dot(h2, w_in_ref[:, off:off + width], preferred_element_type=F32)

    u3 = proj(0, POOL_WIDTH).reshape(bt, tt, POOL_WIDTH)
    if carry:
        @pl.when(pl.program_id(1) == 0)
        def _():
            carry_ref[...] = jnp.zeros_like(carry_ref)
        ext_ref[:, 0:POOL_HALO, :] = carry_ref[...]
    else:
        ext_ref[:, 0:POOL_HALO, :] = prefix_ref[...]
    ext_ref[:, POOL_HALO:, :] = u3
    if carry:
        carry_ref[...] = u3[:, tt - POOL_HALO:, :]
        t_base = pl.program_id(1) * tt
    else:
        t_base = 0
    st_ref[...] = ext_ref[:, tt:tt + POOL_HALO, :]

    zq = proj(POOL_WIDTH, QK_WIDTH)
    q_scale = HEAD_DIM ** -0.5 * (LOG2E if carry else 1.0)
    q_ref[...] = (zq * q_scale).astype(BF16).reshape(bt, tt, QK_WIDTH)
    zv = proj(POOL_WIDTH + 2 * QK_WIDTH, ATTN_WIDTH)
    if carry:
        vb_ref[...] = zv.astype(BF16).reshape(bt, tt, ATTN_WIDTH)
        for hd in range(N_HEADS):
            v_ref[0, pl.ds(hd, tt, stride=N_HEADS), :] = zv[:, hd * V_HEAD_DIM:(hd + 1) * V_HEAD_DIM]
    else:
        v_ref[...] = zv.reshape(bt, tt, ATTN_WIDTH)

    pos = pos0 + t_base + lax.broadcasted_iota(jnp.int32, (bt, tt, POOL_GROUP_DIM), 1)
    for g, w in enumerate(POOL_WINDOWS):
        sl = slice(g * POOL_GROUP_DIM, (g + 1) * POOL_GROUP_DIM)
        tok = ext_ref[:, POOL_HALO:, sl]
        acc = tok
        for j in range(1, w):
            acc = acc + ext_ref[:, POOL_HALO - j:POOL_HALO - j + tt, sl]
        cnt = jnp.minimum(pos + 1, w).astype(F32)
        d = (acc / cnt - tok).reshape(m, POOL_GROUP_DIM).astype(BF16)
        y = jnp.dot(d, w_pool_ref[g], preferred_element_type=F32) * ps_ref[:, sl]
        yp_ref[:, :, sl] = y.astype(BF16).reshape(bt, tt, POOL_GROUP_DIM)

    if carry:
        zkt = lax.dot_general(wkt_ref[...], h2, (((1,), (1,)), ((), ())), preferred_element_type=F32)
        k_ref[0] = zkt
        kb_ref[...] = zkt.T.astype(BF16).reshape(bt, tt, QK_WIDTH)
    else:
        k_ref[...] = proj(POOL_WIDTH + QK_WIDTH, QK_WIDTH).reshape(bt, tt, QK_WIDTH)


def _inproj(x, mod, g_pre, w_in, w_kt, w_pool, pool_scale, prefix, *, pos0):
    nb, t, _ = x.shape
    carry = prefix is None
    if carry:
        bt, tt = 1, ROW_TILE
        grid = (nb, t // tt)
        tile = lambda w: pl.BlockSpec((bt, tt, w), lambda b, i: (b, i, 0))
        per_b = lambda r, w: pl.BlockSpec((bt, r, w), lambda b, i: (b, 0, 0))
        sem = ("arbitrary", "arbitrary")
    else:
        bt, tt = ROW_TILE // t, t
        grid = (nb // bt,)
        tile = lambda w: pl.BlockSpec((bt, tt, w), lambda i: (i, 0, 0))
        per_b = lambda r, w: pl.BlockSpec((bt, r, w), lambda i: (i, 0, 0))
        sem = ("arbitrary",)
    in_specs = [tile(D_MODEL), per_b(N_MOD, D_MODEL), _const_spec((1, 1, D_MODEL)),
                _const_spec((D_MODEL, PROJ_WIDTH)),
                _const_spec((len(POOL_WINDOWS), POOL_GROUP_DIM, POOL_GROUP_DIM)),
                _const_spec((1, POOL_WIDTH))]
    args = [x, mod, g_pre.reshape(1, 1, D_MODEL), w_in, w_pool, pool_scale.reshape(1, POOL_WIDTH)]
    scratch = [pltpu.VMEM((bt, POOL_HALO + tt, POOL_WIDTH), F32)]
    st_shape = jax.ShapeDtypeStruct((nb, POOL_HALO, POOL_WIDTH), F32)
    act = lambda w, dt: jax.ShapeDtypeStruct((nb, t, w), dt)
    if carry:
        scratch.append(pltpu.VMEM((1, POOL_HALO, POOL_WIDTH), F32))
        in_specs.append(_const_spec((QK_WIDTH, D_MODEL)))
        args.append(w_kt)
        out_shape = (act(QK_WIDTH, BF16),
                     jax.ShapeDtypeStruct((nb, QK_WIDTH, t), F32),
                     jax.ShapeDtypeStruct((nb, t * N_HEADS, V_HEAD_DIM), F32), act(QK_WIDTH, BF16),
                     act(ATTN_WIDTH, BF16), act(POOL_WIDTH, BF16), st_shape)
        out_specs = (tile(QK_WIDTH),
                     pl.BlockSpec((1, QK_WIDTH, tt), lambda b, i: (b, 0, i)),
                     pl.BlockSpec((1, tt * N_HEADS, V_HEAD_DIM), lambda b, i: (b, i, 0)), tile(QK_WIDTH),
                     tile(ATTN_WIDTH), tile(POOL_WIDTH), per_b(POOL_HALO, POOL_WIDTH))
    else:
        in_specs.append(per_b(POOL_HALO, POOL_WIDTH))
        args.append(prefix)
        out_shape = (act(QK_WIDTH, BF16), act(QK_WIDTH, F32), act(ATTN_WIDTH, F32),
                     act(POOL_WIDTH, BF16), st_shape)
        out_specs = (tile(QK_WIDTH), tile(QK_WIDTH), tile(ATTN_WIDTH), tile(POOL_WIDTH),
                     per_b(POOL_HALO, POOL_WIDTH))
    return pl.pallas_call(
        functools.partial(_inproj_kernel, carry=carry, pos0=pos0, bt=bt, tt=tt),
        out_shape=out_shape, grid=grid, in_specs=in_specs, out_specs=out_specs,
        scratch_shapes=scratch,
        compiler_params=pltpu.CompilerParams(dimension_semantics=sem, vmem_limit_bytes=VMEM_LIMIT),
        name="inproj_prompt" if carry else "inproj_sample",
    )(*args)


def _attn_prompt_kernel(q_ref, kb_ref, vb_ref, bp_ref, lam_ref, gh_ref, o_ref,
                        w_ref, m_ref, acc_ref):
    tq, tk = ATTN_TQ, ATTN_TK
    nt = q_ref.shape[1] // tq
    zero = jnp.zeros((HEAD_DIM, tq), BF16)
    for i in range(nt):
        qt = q_ref[0, i * tq:(i + 1) * tq, :].astype(F32).T.astype(BF16)
        w_ref[i] = jnp.concatenate([jnp.concatenate([qt[:HEAD_DIM], zero], axis=1),
                                    jnp.concatenate([zero, qt[HEAD_DIM:]], axis=1)], axis=0)
    lam = lam_ref[0:1, 0:1]

    def scores(i, j):
        return jnp.dot(kb_ref[0, j * tk:(j + 1) * tk, :], w_ref[i], preferred_element_type=F32)

    units = [(i, j) for j in range(nt) for i in range(j, nt)]
    pending = [scores(*u) for u in units[:ATTN_LOOKAHEAD]]
    vt = None
    for n, (i, j) in enumerate(units):
        st = pending.pop(0)
        if n + ATTN_LOOKAHEAD < len(units):
            pending.append(scores(*units[n + ATTN_LOOKAHEAD]))
        if i == j:
            vt = jnp.concatenate([vb_ref[0, j * tk:(j + 1) * tk, :].astype(F32).T,
                                  jnp.ones((SUM_ROWS, tk), F32)], axis=0).astype(BF16)
        if i - j < 2:
            b = bp_ref[0, i - j]
            st = st + jnp.concatenate([b, b], axis=1)
        mx = jnp.max(st, axis=0, keepdims=True)
        if j == 0:
            p = jnp.exp2(st - mx)
            acc_ref[i] = jnp.dot(vt, p.astype(BF16), preferred_element_type=F32)
            m_ref[i] = mx
        else:
            m_old = m_ref[i]
            m_new = jnp.maximum(m_old, mx)
            alpha = jnp.exp2(m_old - m_new)
            p = jnp.exp2(st - m_new)
            acc_ref[i] = alpha * acc_ref[i] + jnp.dot(vt, p.astype(BF16), preferred_element_type=F32)
            m_ref[i] = m_new
        if i == j:
            acc = acc_ref[j]
            on = acc[:V_HEAD_DIM] / acc[V_HEAD_DIM:V_HEAD_DIM + 1]
            ot = on[:, :tq] - lam * on[:, tq:]
            ot = ot * lax.rsqrt(jnp.mean(ot * ot, axis=0, keepdims=True) + EPS)
            o_ref[0, j * tq:(j + 1) * tq, :] = (ot.T * gh_ref[...] * (1.0 - LAM_INIT)).astype(BF16)


def _attn_prompt(q, kb, vb, bias_p, lam, g_head):
    nb, t, _ = q.shape
    tq = ATTN_TQ
    nt = t // tq
    hw = V_HEAD_DIM
    head = pl.BlockSpec((1, t, hw), lambda b, h: (b, 0, h))
    return pl.pallas_call(
        _attn_prompt_kernel,
        out_shape=jax.ShapeDtypeStruct((nb, t, ATTN_WIDTH), BF16),
        grid=(nb, N_HEADS),
        in_specs=[head, head, head,
                  pl.BlockSpec((1, 2, ATTN_TK, ATTN_TQ), lambda b, h: (h, 0, 0, 0)),
                  pl.BlockSpec((8, LANES), lambda b, h: (0, 0)),
                  pl.BlockSpec((1, hw), lambda b, h: (0, 0))],
        out_specs=head,
        scratch_shapes=[pltpu.VMEM((nt, 2 * HEAD_DIM, 2 * tq), BF16),
                        pltpu.VMEM((nt, 1, 2 * tq), F32),
                        pltpu.VMEM((nt, hw + SUM_ROWS, 2 * tq), F32)],
        compiler_params=pltpu.CompilerParams(
            dimension_semantics=("arbitrary", "arbitrary"), vmem_limit_bytes=VMEM_LIMIT),
        name="attn_prompt",
    )(q, kb, vb, bias_p, lam, g_head.reshape(1, hw))


def _attn_sample_kernel(pt_ref, q_ref, kn_ref, vn_ref, bs_ref, b31s_ref, lam_ref, gh_ref, *rest,
                        n_pages, page):
    k_pages = rest[:n_pages]
    v_pages = rest[n_pages:2 * n_pages]
    o_ref, kbf_ref, vbf_ref = rest[2 * n_pages:]
    past = n_pages * page
    tnew = q_ref.shape[1]
    n_maps = 2 * N_HEADS

    q = q_ref[0].astype(F32)
    qt = jnp.concatenate([q] * n_maps, axis=0)
    rg = lax.broadcasted_iota(jnp.int32, qt.shape, 0) // tnew
    cg = lax.broadcasted_iota(jnp.int32, qt.shape, 1) // HEAD_DIM
    qbd = jnp.where(rg == cg, qt, 0.0).astype(BF16)

    for j in range(n_pages):
        kbf_ref[:, j * page:(j + 1) * page] = k_pages[j][0].astype(BF16)
        for h in range(N_HEADS):
            vbf_ref[j * page:(j + 1) * page, h * V_HEAD_DIM:(h + 1) * V_HEAD_DIM] = (
                v_pages[j][0, pl.ds(h, page, stride=N_HEADS), :].astype(BF16))
    pad = jnp.zeros((LANES - tnew, QK_WIDTH), F32)
    k_new = jnp.concatenate([kn_ref[0], pad], axis=0).astype(BF16)
    vbf_ref[past:, :] = jnp.concatenate([vn_ref[0], pad], axis=0).astype(BF16)

    s_past = jnp.dot(qbd, kbf_ref[...], preferred_element_type=F32)
    s_new = lax.dot_general(qbd, k_new, (((1,), (1,)), ((), ())), preferred_element_type=F32)
    near = past - page
    s = jnp.concatenate([s_past[:, :near] + b31s_ref[:, 0:1],
                         jnp.concatenate([s_past[:, near:], s_new], axis=1) + bs_ref[...]], axis=1)
    mx = jnp.max(s, axis=-1, keepdims=True)
    p = jnp.exp(s - mx)
    pn = p / jnp.sum(p, axis=-1, keepdims=True)
    lam = lam_ref[0:1, 0:1]
    a = jnp.concatenate(
        [pn[2 * h * tnew:(2 * h + 1) * tnew] - lam * pn[(2 * h + 1) * tnew:(2 * h + 2) * tnew]
         for h in range(N_HEADS)], axis=0).astype(BF16)
    r = jnp.dot(a, vbf_ref[...], preferred_element_type=F32)
    outs = []
    for h in range(N_HEADS):
        o = r[h * tnew:(h + 1) * tnew, h * V_HEAD_DIM:(h + 1) * V_HEAD_DIM]
        outs.append(_rms(o, gh_ref[...]) * (1.0 - LAM_INIT))
    o_ref[0] = jnp.concatenate(outs, axis=1).astype(BF16)


def _attn_sample(page_table, q, k_new, v_new, cache_k, cache_v, bias_s, b31s, lam, g_head):
    nb, tnew, _ = q.shape
    n_pages = page_table.shape[1]
    n_phys, page = cache_k.shape[0], cache_k.shape[1]
    ck = jnp.transpose(cache_k, (0, 2, 3, 4, 1)).reshape(n_phys, QK_WIDTH, page)
    cv = cache_v.reshape(n_phys, page * N_HEADS, V_HEAD_DIM)
    tile = lambda w: pl.BlockSpec((1, tnew, w), lambda b, pt: (b, 0, 0))
    const = lambda shape: pl.BlockSpec(shape, lambda b, pt: (0,) * len(shape))

    def page_spec(j, shape):
        return pl.BlockSpec((1,) + shape, lambda b, pt: (pt[b, j], 0, 0))

    in_specs = ([tile(QK_WIDTH), tile(QK_WIDTH), tile(ATTN_WIDTH),
                 const(bias_s.shape), const(b31s.shape), const(lam.shape), const((1, V_HEAD_DIM))]
                + [page_spec(j, (QK_WIDTH, page)) for j in range(n_pages)]
                + [page_spec(j, (page * N_HEADS, V_HEAD_DIM)) for j in range(n_pages)])
    total = n_pages * page + LANES
    return pl.pallas_call(
        functools.partial(_attn_sample_kernel, n_pages=n_pages, page=page),
        out_shape=jax.ShapeDtypeStruct((nb, tnew, ATTN_WIDTH), BF16),
        grid_spec=pltpu.PrefetchScalarGridSpec(
            num_scalar_prefetch=1, grid=(nb,), in_specs=in_specs,
            out_specs=pl.BlockSpec((1, tnew, ATTN_WIDTH), lambda b, pt: (b, 0, 0)),
            scratch_shapes=[pltpu.VMEM((QK_WIDTH, n_pages * page), BF16),
                            pltpu.VMEM((total, ATTN_WIDTH), BF16)]),
        compiler_params=pltpu.CompilerParams(dimension_semantics=("arbitrary",),
                                             vmem_limit_bytes=VMEM_LIMIT),
        name="attn_sample",
    )(page_table, q, k_new, v_new, bias_s, b31s, lam, g_head.reshape(1, V_HEAD_DIM),
      *([ck] * n_pages), *([cv] * n_pages))


def _ffn_kernel(*refs, carry, ns, bt, tt):
    if carry:
        (x_ref, yp_ref, at_ref, mod_ref, gpm_ref, gpf_ref, gpo_ref, w_out_ref, w_up_ref, cw_ref, cb_ref,
         w_down_ref, y_ref, cs_ref, *buf_refs, carry_ref) = refs
    else:
        (x_ref, yp_ref, at_ref, mod_ref, gpm_ref, gpf_ref, gpo_ref, w_out_ref, w_up_ref, cw_ref, cb_ref,
         w_down_ref, prefix_ref, y_ref, cs_ref, *buf_refs) = refs
    m = bt * tt
    n_chunks = D_FF // FF_CHUNK
    streams = range(ns)

    def rows(s):
        return (slice(None), slice(s * tt, (s + 1) * tt)) if carry else (slice(s * bt, (s + 1) * bt),)

    def mod_row(s, i):
        return mod_ref[:, i:i + 1, :] if carry else mod_ref[s * bt:(s + 1) * bt, i:i + 1, :]

    def buf(s, slot):
        return buf_refs[4 * s + slot]

    if carry:
        @pl.when(pl.program_id(1) == 0)
        def _():
            carry_ref[...] = jnp.zeros_like(carry_ref)
    halo_rows = CONV_HALO if carry else prefix_ref.shape[1]

    x1, h2 = {}, {}
    for s in streams:
        mix_in = jnp.concatenate([yp_ref[rows(s)], at_ref[rows(s)]], axis=-1).reshape(m, D_MODEL)
        mix = jnp.dot(mix_in, w_out_ref[...], preferred_element_type=F32).reshape(bt, tt, D_MODEL)
        x1[s] = x_ref[rows(s)] + mod_row(s, 2) * _rms(mix, gpm_ref[...])
        h = _rms(x1[s], gpf_ref[...]) * (1.0 + mod_row(s, 4)) + mod_row(s, 3)
        h2[s] = h.reshape(m, D_MODEL).astype(BF16)

    def up_part(s, slot, off):
        sl = slice(off, off + FF_CHUNK)
        u3 = jnp.dot(h2[s], w_up_ref[:, sl], preferred_element_type=F32).reshape(bt, tt, FF_CHUNK)
        if not carry:
            halo = prefix_ref[s * bt:(s + 1) * bt, :, sl]
        elif s == 0:
            halo = carry_ref[:, :, sl]
        else:
            halo = buf(s - 1, slot)[:, tt:tt + CONV_HALO, :]
        buf(s, slot)[:, CONV_HALO - halo_rows:CONV_HALO, :] = halo
        buf(s, slot)[:, CONV_HALO:, :] = u3
        if not carry:
            cs_ref[s * bt:(s + 1) * bt, :, sl] = u3[:, tt - halo_rows:, :]
        elif s == ns - 1:
            cs_ref[:, :, sl] = u3[:, tt - halo_rows:, :]

    def conv_part(s, slot, off):
        sl = slice(off, off + FF_CHUNK)
        b = buf(s, slot)
        y = (cb_ref[:, :, sl]
             + cw_ref[0:1, :, sl] * b[:, CONV_HALO - 2:CONV_HALO - 2 + tt, :]
             + cw_ref[1:2, :, sl] * b[:, CONV_HALO - 1:CONV_HALO - 1 + tt, :]
             + cw_ref[2:3, :, sl] * b[:, CONV_HALO:, :])
        return y.reshape(m, FF_CHUNK)

    def up_chunk(s, c):
        up_part(s, 2 * (c % 2), c * FF_CHUNK)
        up_part(s, 2 * (c % 2) + 1, D_FF + c * FF_CHUNK)

    f = {s: jnp.zeros((m, D_MODEL), F32) for s in streams}
    for s in streams:
        up_chunk(s, 0)
    for c in range(n_chunks):
        if c + 1 < n_chunks:
            for s in streams:
                up_chunk(s, c + 1)
        for s in streams:
            gate = conv_part(s, 2 * (c % 2), c * FF_CHUNK)
            val = conv_part(s, 2 * (c % 2) + 1, D_FF + c * FF_CHUNK)
            act = (_gelu_tanh(gate) * val).astype(BF16)
            f[s] = f[s] + jnp.dot(act, w_down_ref[c * FF_CHUNK:(c + 1) * FF_CHUNK, :],
                                  preferred_element_type=F32)
    if carry:
        carry_ref[...] = cs_ref[...]
    for s in streams:
        y_ref[rows(s)] = x1[s] + mod_row(s, 5) * _rms(f[s].reshape(bt, tt, D_MODEL), gpo_ref[...])


def _ffn(x, yp, attn, mod, g_post_mix, g_pre_ffn, g_post_ffn, w_out, w_up, conv_w, conv_b, w_down, prefix):
    nb, t, _ = x.shape
    carry = prefix is None
    ns = FFN_STREAMS
    if carry:
        bt, tt = 1, ROW_TILE
        grid = (nb, t // (ns * tt))
        tile = lambda w: pl.BlockSpec((1, ns * tt, w), lambda b, i: (b, i, 0))
        per_b = lambda r, w: pl.BlockSpec((1, r, w), lambda b, i: (b, 0, 0))
        sem = ("arbitrary", "arbitrary")
    else:
        bt, tt = ROW_TILE // t, t
        grid = (nb // (ns * bt),)
        tile = lambda w: pl.BlockSpec((ns * bt, tt, w), lambda i: (i, 0, 0))
        per_b = lambda r, w: pl.BlockSpec((ns * bt, r, w), lambda i: (i, 0, 0))
        sem = ("arbitrary",)
    gspec = _const_spec((1, 1, D_MODEL))
    in_specs = [tile(D_MODEL), tile(POOL_WIDTH), tile(ATTN_WIDTH), per_b(N_MOD, D_MODEL),
                gspec, gspec, gspec,
                _const_spec((D_MODEL, D_MODEL)), _const_spec((D_MODEL, 2 * D_FF)),
                _const_spec((CONV_WIDTH, 1, 2 * D_FF)), _const_spec((1, 1, 2 * D_FF)),
                _const_spec((D_FF, D_MODEL))]
    args = [x, yp, attn, mod, g_post_mix.reshape(1, 1, -1), g_pre_ffn.reshape(1, 1, -1),
            g_post_ffn.reshape(1, 1, -1), w_out, w_up, conv_w.reshape(CONV_WIDTH, 1, -1),
            conv_b.reshape(1, 1, -1), w_down]
    scratch = [pltpu.VMEM((bt, CONV_HALO + tt, FF_CHUNK), F32) for _ in range(4 * ns)]
    if carry:
        halo_rows = CONV_HALO
        scratch.append(pltpu.VMEM((1, halo_rows, 2 * D_FF), F32))
    else:
        halo_rows = prefix.shape[1]
        in_specs.append(per_b(halo_rows, 2 * D_FF))
        args.append(prefix)
    return pl.pallas_call(
        functools.partial(_ffn_kernel, carry=carry, ns=ns, bt=bt, tt=tt),
        out_shape=(jax.ShapeDtypeStruct((nb, t, D_MODEL), F32),
                   jax.ShapeDtypeStruct((nb, halo_rows, 2 * D_FF), F32)),
        grid=grid, in_specs=in_specs,
        out_specs=(tile(D_MODEL), per_b(halo_rows, 2 * D_FF)),
        scratch_shapes=scratch,
        compiler_params=pltpu.CompilerParams(dimension_semantics=sem, vmem_limit_bytes=VMEM_LIMIT),
        name="ffn_prompt" if carry else "ffn_sample",
    )(*args)


def kernel(x_prompt, x_sample, c_prompt, c_sample, cache_k, cache_v, page_table, state_pool, state_conv,
           w_ada, b_ada, g_pre_mix, g_post_mix, g_pre_ffn, g_post_ffn, w_in, w_out, w_pool, pool_scale,
           lam_q1, lam_k1, lam_q2, lam_k2, g_head, rel_bias, w_up, conv_w, conv_b, w_down):
    nbp, seq, _ = x_prompt.shape
    nbs, dec_seq, _ = x_sample.shape
    past_len = page_table.shape[1] * cache_k.shape[2]
    l = 0

    lam, bias_p, bias_s, b31s = _setup(rel_bias, lam_q1[l:l + 1], lam_k1[l:l + 1],
                                             lam_q2[l:l + 1], lam_k2[l:l + 1])
    mod = _modulation(jnp.concatenate([c_prompt, c_sample], axis=0), w_ada[l], b_ada[l])
    mod = mod.reshape(nbp + nbs, N_MOD, D_MODEL)
    mod_p, mod_s = mod[:nbp], mod[nbp:]

    w_in_b = w_in[l].astype(BF16)
    w_kt_b = w_in[l][:, POOL_WIDTH + QK_WIDTH:POOL_WIDTH + 2 * QK_WIDTH].T.astype(BF16)
    w_pool_b = w_pool[l].astype(BF16)
    w_out_b = w_out[l].astype(BF16)
    w_up_b = w_up[l].astype(BF16)
    w_down_b = w_down[l].astype(BF16)

    pool_prefix = jnp.pad(state_pool[l], ((0, 0), (POOL_HALO - POOL_STATE, 0), (0, 0)))

    outs = []
    for (x, md, pprefix, cprefix, pos0) in ((x_prompt, mod_p, None, None, 0),
                                            (x_sample, mod_s, pool_prefix, state_conv[l], past_len)):
        nb, t = x.shape[0], x.shape[1]
        res = _inproj(x, md, g_pre_mix[l], w_in_b, w_kt_b, w_pool_b, pool_scale[l], pprefix, pos0=pos0)
        if pprefix is None:
            q, kt, v, kb, vb, yp, pstate = res
            attn = _attn_prompt(q, kb, vb, bias_p, lam, g_head[l])
            k = jnp.transpose(kt.reshape(nb, N_HEADS, 2, HEAD_DIM, t), (0, 4, 1, 2, 3))
        else:
            q, k, v, yp, pstate = res
            attn = _attn_sample(page_table, q, k, v, cache_k[l], cache_v[l], bias_s, b31s, lam, g_head[l])
        y, cstate = _ffn(x, yp, attn, md, g_post_mix[l], g_pre_ffn[l], g_post_ffn[l],
                         w_out_b, w_up_b, conv_w[l], conv_b[l], w_down_b, cprefix)
        outs.append((y,
                     k.reshape(1, nb, t, N_HEADS, 2, HEAD_DIM),
                     v.reshape(1, nb, t, N_HEADS, V_HEAD_DIM),
                     pstate[None, :, POOL_HALO - POOL_STATE:, :],
                     cstate[None, :, cstate.shape[1] - (CONV_WIDTH - 1):, :]))
    (yp_, kp, vp, pp, cp), (ys_, ks, vs, ps, cs) = outs
    return (yp_, ys_, kp, vp, pp, cp, ks, vs, ps, cs)
```

```python
import functools
import math

import jax
import jax.numpy as jnp
from jax import lax
from jax.experimental import pallas as pl
from jax.experimental.pallas import tpu as pltpu

F32 = jnp.float32
BF16 = jnp.bfloat16

D_MODEL = 1024
POOL_WIDTH = 512
POOL_WINDOWS = (2, 4, 8, 16)
POOL_GROUP_DIM = 128
POOL_STATE = 15
POOL_HALO = 16
N_HEADS = 4
HEAD_DIM = 64
V_HEAD_DIM = 128
QK_WIDTH = 512
ATTN_WIDTH = 512
PROJ_WIDTH = 2048
D_FF = 2816
CONV_WIDTH = 3
CONV_HALO = 8
N_BUCKETS = 32
MAX_EXACT = N_BUCKETS // 2
MAX_DISTANCE = 128
N_MOD = 6
EPS = 1e-6
LAM_INIT = 0.8 - 0.6 * math.exp(-0.3 * 0)
NEG_BIG = -1e30
LOG2E = math.log2(math.e)
SUM_ROWS = 16

LANES = 128
MXU_TILE = 256
VMEM_LIMIT = 56 * 1024 * 1024

ATTN_TQ = 256
ATTN_TK = 256
ATTN_LOOKAHEAD = 3
ROW_TILE = 256
FF_CHUNK = 256
SAMPLE_SLOTS = 2
FFN_STREAMS = 2


def _const_spec(shape):
    nd = len(shape)
    return pl.BlockSpec(shape, lambda *_: (0,) * nd, pipeline_mode=pl.Buffered(1))


def _rms(x, g):
    return x * lax.rsqrt(jnp.mean(x * x, axis=-1, keepdims=True) + EPS) * g


def _gelu_tanh(x):
    c0 = -2.0 * math.sqrt(2.0 / math.pi) * LOG2E
    return x / (1.0 + jnp.exp2(x * (c0 + (c0 * 0.044715) * (x * x))))


def _bucket(dist):
    d = jnp.maximum(dist, 1).astype(F32)
    large = MAX_EXACT + jnp.floor(jnp.log(d / MAX_EXACT) / math.log(MAX_DISTANCE / MAX_EXACT)
                                  * (N_BUCKETS - MAX_EXACT)).astype(jnp.int32)
    large = jnp.minimum(large, N_BUCKETS - 1)
    return jnp.where(dist < MAX_EXACT, dist, large)


def _setup_kernel(rb_ref, lq1_ref, lk1_ref, lq2_ref, lk2_ref,
                  lam_ref, bp_ref, bs_ref, b31s_ref):
    lam = (jnp.exp(jnp.sum(lq1_ref[...] * lk1_ref[...], axis=-1, keepdims=True))
           - jnp.exp(jnp.sum(lq2_ref[...] * lk2_ref[...], axis=-1, keepdims=True)) + LAM_INIT)
    lam_ref[...] = jnp.broadcast_to(lam, lam_ref.shape)

    def lookup(bucket, h):
        out = jnp.full(bucket.shape, rb_ref[N_BUCKETS - 1, h], F32)
        for b in range(N_BUCKETS - 1):
            out = jnp.where(bucket == b, rb_ref[b, h], out)
        return out

    kk = lax.broadcasted_iota(jnp.int32, (ATTN_TK, ATTN_TQ), 0)
    r = lax.broadcasted_iota(jnp.int32, (ATTN_TK, ATTN_TQ), 1)
    for typ in range(2):
        dist = typ * ATTN_TK + r - kk
        bucket = _bucket(jnp.maximum(dist, 0))
        for h in range(N_HEADS):
            shifted = (lookup(bucket, h) - rb_ref[N_BUCKETS - 1, h]) * LOG2E
            bp_ref[h, typ] = jnp.where(dist >= 0, shifted, NEG_BIG)

    rows = 2 * 8
    t = lax.broadcasted_iota(jnp.int32, (rows, 2 * LANES), 0) % 8
    col = lax.broadcasted_iota(jnp.int32, (rows, 2 * LANES), 1)
    dist = jnp.where(col < LANES, LANES + t - col, t - (col - LANES))
    bucket = _bucket(jnp.maximum(dist, 0))
    for h in range(N_HEADS):
        bs_ref[h * rows:(h + 1) * rows, :] = jnp.where(dist >= 0, lookup(bucket, h), NEG_BIG)
        b31s_ref[h * rows:(h + 1) * rows, :] = jnp.full((rows, LANES), rb_ref[N_BUCKETS - 1, h], F32)


def _setup(rel_bias, lq1, lk1, lq2, lk2):
    vspec = pl.BlockSpec(memory_space=pltpu.VMEM)
    return pl.pallas_call(
        _setup_kernel,
        out_shape=(jax.ShapeDtypeStruct((8, LANES), F32),
                   jax.ShapeDtypeStruct((N_HEADS, 2, ATTN_TK, ATTN_TQ), F32),
                   jax.ShapeDtypeStruct((N_HEADS * 16, 2 * LANES), F32),
                   jax.ShapeDtypeStruct((N_HEADS * 16, LANES), F32)),
        in_specs=[pl.BlockSpec(memory_space=pltpu.SMEM), vspec, vspec, vspec, vspec],
        out_specs=(vspec, vspec, vspec, vspec),
        name="setup",
    )(rel_bias, lq1, lk1, lq2, lk2)


def _mod_kernel(c_ref, w_ref, b_ref, o_ref):
    c = c_ref[...]
    s = (c * jax.nn.sigmoid(c)).astype(BF16)
    o_ref[...] = jnp.dot(s, w_ref[...].astype(BF16), preferred_element_type=F32) + b_ref[...]


def _modulation(c_all, w_ada, b_ada):
    n = c_all.shape[0]
    tn = 1024
    return pl.pallas_call(
        _mod_kernel,
        out_shape=jax.ShapeDtypeStruct((n, N_MOD * D_MODEL), F32),
        grid=(N_MOD * D_MODEL // tn,),
        in_specs=[pl.BlockSpec((n, D_MODEL), lambda j: (0, 0)),
                  pl.BlockSpec((D_MODEL, tn), lambda j: (0, j)),
                  pl.BlockSpec((1, tn), lambda j: (0, j))],
        out_specs=pl.BlockSpec((n, tn), lambda j: (0, j)),
        compiler_params=pltpu.CompilerParams(dimension_semantics=("arbitrary",),
                                             vmem_limit_bytes=VMEM_LIMIT),
        name="modulation",
    )(c_all, w_ada, b_ada.reshape(1, -1))


def _inproj_kernel(*refs, carry, pos0, bt, tt):
    if carry:
        (x_ref, mod_ref, g_ref, w_in_ref, w_pool_ref, ps_ref, wkt_ref,
         q_ref, k_ref, v_ref, kb_ref, vb_ref, yp_ref, st_ref, ext_ref, carry_ref) = refs
    else:
        (x_ref, mod_ref, g_ref, w_in_ref, w_pool_ref, ps_ref, prefix_ref,
         q_ref, k_ref, v_ref, yp_ref, st_ref, ext_ref) = refs
    m = bt * tt
    x = x_ref[...]
    h = _rms(x, g_ref[...]) * (1.0 + mod_ref[:, 1:2, :]) + mod_ref[:, 0:1, :]
    h2 = h.reshape(m, D_MODEL).astype(BF16)

    def proj(off, width):
        return jnp.dot(h2, w_in_ref[:, off:off + width], preferred_element_type=F32)

    u3 = proj(0, POOL_WIDTH).reshape(bt, tt, POOL_WIDTH)
    if carry:
        @pl.when(pl.program_id(1) == 0)
        def _():
            carry_ref[...] = jnp.zeros_like(carry_ref)
        ext_ref[:, 0:POOL_HALO, :] = carry_ref[...]
    else:
        ext_ref[:, 0:POOL_HALO, :] = prefix_ref[...]
    ext_ref[:, POOL_HALO:, :] = u3
    if carry:
        carry_ref[...] = u3[:, tt - POOL_HALO:, :]
        t_base = pl.program_id(1) * tt
    else:
        t_base = 0
    st_ref[...] = ext_ref[:, tt:tt + POOL_HALO, :]

    zq = proj(POOL_WIDTH, QK_WIDTH)
    q_scale = HEAD_DIM ** -0.5 * (LOG2E if carry else 1.0)
    q_ref[...] = (zq * q_scale).astype(BF16).reshape(bt, tt, QK_WIDTH)
    zv = proj(POOL_WIDTH + 2 * QK_WIDTH, ATTN_WIDTH)
    if carry:
        vb_ref[...] = zv.astype(BF16).reshape(bt, tt, ATTN_WIDTH)
        for hd in range(N_HEADS):
            v_ref[0, pl.ds(hd, tt, stride=N_HEADS), :] = zv[:, hd * V_HEAD_DIM:(hd + 1) * V_HEAD_DIM]
    else:
        v_ref[...] = zv.reshape(bt, tt, ATTN_WIDTH)

    pos = pos0 + t_base + lax.broadcasted_iota(jnp.int32, (bt, tt, POOL_GROUP_DIM), 1)
    for g, w in enumerate(POOL_WINDOWS):
        sl = slice(g * POOL_GROUP_DIM, (g + 1) * POOL_GROUP_DIM)
        tok = ext_ref[:, POOL_HALO:, sl]
        acc = tok
        for j in range(1, w):
            acc = acc + ext_ref[:, POOL_HALO - j:POOL_HALO - j + tt, sl]
        cnt = jnp.minimum(pos + 1, w).astype(F32)
        d = (acc / cnt - tok).reshape(m, POOL_GROUP_DIM).astype(BF16)
        y = jnp.dot(d, w_pool_ref[g], preferred_element_type=F32) * ps_ref[:, sl]
        yp_ref[:, :, sl] = y.astype(BF16).reshape(bt, tt, POOL_GROUP_DIM)

    if carry:
        zkt = lax.dot_general(wkt_ref[...], h2, (((1,), (1,)), ((), ())), preferred_element_type=F32)
        k_ref[0] = zkt
        kb_ref[...] = zkt.T.astype(BF16).reshape(bt, tt, QK_WIDTH)
    else:
        k_ref[...] = proj(POOL_WIDTH + QK_WIDTH, QK_WIDTH).reshape(bt, tt, QK_WIDTH)


def _inproj(x, mod, g_pre, w_in, w_kt, w_pool, pool_scale, prefix, *, pos0):
    nb, t, _ = x.shape
    carry = prefix is None
    if carry:
        bt, tt = 1, ROW_TILE
        grid = (nb, t // tt)
        tile = lambda w: pl.BlockSpec((bt, tt, w), lambda b, i: (b, i, 0))
        per_b = lambda r, w: pl.BlockSpec((bt, r, w), lambda b, i: (b, 0, 0))
        sem = ("arbitrary", "arbitrary")
    else:
        bt, tt = ROW_TILE // t, t
        grid = (nb // bt,)
        tile = lambda w: pl.BlockSpec((bt, tt, w), lambda i: (i, 0, 0))
        per_b = lambda r, w: pl.BlockSpec((bt, r, w), lambda i: (i, 0, 0))
        sem = ("arbitrary",)
    in_specs = [tile(D_MODEL), per_b(N_MOD, D_MODEL), _const_spec((1, 1, D_MODEL)),
                _const_spec((D_MODEL, PROJ_WIDTH)),
                _const_spec((len(POOL_WINDOWS), POOL_GROUP_DIM, POOL_GROUP_DIM)),
                _const_spec((1, POOL_WIDTH))]
    args = [x, mod, g_pre.reshape(1, 1, D_MODEL), w_in, w_pool, pool_scale.reshape(1, POOL_WIDTH)]
    scratch = [pltpu.VMEM((bt, POOL_HALO + tt, POOL_WIDTH), F32)]
    st_shape = jax.ShapeDtypeStruct((nb, POOL_HALO, POOL_WIDTH), F32)
    act = lambda w, dt: jax.ShapeDtypeStruct((nb, t, w), dt)
    if carry:
        scratch.append(pltpu.VMEM((1, POOL_HALO, POOL_WIDTH), F32))
        in_specs.append(_const_spec((QK_WIDTH, D_MODEL)))
        args.append(w_kt)
        out_shape = (act(QK_WIDTH, BF16),
                     jax.ShapeDtypeStruct((nb, QK_WIDTH, t), F32),
                     jax.ShapeDtypeStruct((nb, t * N_HEADS, V_HEAD_DIM), F32), act(QK_WIDTH, BF16),
                     act(ATTN_WIDTH, BF16), act(POOL_WIDTH, BF16), st_shape)
        out_specs = (tile(QK_WIDTH),
                     pl.BlockSpec((1, QK_WIDTH, tt), lambda b, i: (b, 0, i)),
                     pl.BlockSpec((1, tt * N_HEADS, V_HEAD_DIM), lambda b, i: (b, i, 0)), tile(QK_WIDTH),
                     tile(ATTN_WIDTH), tile(POOL_WIDTH), per_b(POOL_HALO, POOL_WIDTH))
    else:
        in_specs.append(per_b(POOL_HALO, POOL_WIDTH))
        args.append(prefix)
        out_shape = (act(QK_WIDTH, BF16), act(QK_WIDTH, F32), act(ATTN_WIDTH, F32),
                     act(POOL_WIDTH, BF16), st_shape)
        out_specs = (tile(QK_WIDTH), tile(QK_WIDTH), tile(ATTN_WIDTH), tile(POOL_WIDTH),
                     per_b(POOL_HALO, POOL_WIDTH))
    return pl.pallas_call(
        functools.partial(_inproj_kernel, carry=carry, pos0=pos0, bt=bt, tt=tt),
        out_shape=out_shape, grid=grid, in_specs=in_specs, out_specs=out_specs,
        scratch_shapes=scratch,
        compiler_params=pltpu.CompilerParams(dimension_semantics=sem, vmem_limit_bytes=VMEM_LIMIT),
        name="inproj_prompt" if carry else "inproj_sample",
    )(*args)


def _attn_prompt_kernel(q_ref, kb_ref, vb_ref, bp_ref, lam_ref, gh_ref, o_ref,
                        w_ref, m_ref, acc_ref):
    tq, tk = ATTN_TQ, ATTN_TK
    nt = q_ref.shape[1] // tq
    zero = jnp.zeros((HEAD_DIM, tq), BF16)
    for i in range(nt):
        qt = q_ref[0, i * tq:(i + 1) * tq, :].astype(F32).T.astype(BF16)
        w_ref[i] = jnp.concatenate([jnp.concatenate([qt[:HEAD_DIM], zero], axis=1),
                                    jnp.concatenate([zero, qt[HEAD_DIM:]], axis=1)], axis=0)
    lam = lam_ref[0:1, 0:1]

    def scores(i, j):
        return jnp.dot(kb_ref[0, j * tk:(j + 1) * tk, :], w_ref[i], preferred_element_type=F32)

    units = [(i, j) for j in range(nt) for i in range(j, nt)]
    pending = [scores(*u) for u in units[:ATTN_LOOKAHEAD]]
    vt = None
    for n, (i, j) in enumerate(units):
        st = pending.pop(0)
        if n + ATTN_LOOKAHEAD < len(units):
            pending.append(scores(*units[n + ATTN_LOOKAHEAD]))
        if i == j:
            vt = jnp.concatenate([vb_ref[0, j * tk:(j + 1) * tk, :].astype(F32).T,
                                  jnp.ones((SUM_ROWS, tk), F32)], axis=0).astype(BF16)
        if i - j < 2:
            b = bp_ref[0, i - j]
            st = st + jnp.concatenate([b, b], axis=1)
        mx = jnp.max(st, axis=0, keepdims=True)
        if j == 0:
            p = jnp.exp2(st - mx)
            acc_ref[i] = jnp.dot(vt, p.astype(BF16), preferred_element_type=F32)
            m_ref[i] = mx
        else:
            m_old = m_ref[i]
            m_new = jnp.maximum(m_old, mx)
            alpha = jnp.exp2(m_old - m_new)
            p = jnp.exp2(st - m_new)
            acc_ref[i] = alpha * acc_ref[i] + jnp.dot(vt, p.astype(BF16), preferred_element_type=F32)
            m_ref[i] = m_new
        if i == j:
            acc = acc_ref[j]
            on = acc[:V_HEAD_DIM] / acc[V_HEAD_DIM:V_HEAD_DIM + 1]
            ot = on[:, :tq] - lam * on[:, tq:]
            ot = ot * lax.rsqrt(jnp.mean(ot * ot, axis=0, keepdims=True) + EPS)
            o_ref[0, j * tq:(j + 1) * tq, :] = (ot.T * gh_ref[...] * (1.0 - LAM_INIT)).astype(BF16)


def _attn_prompt(q, kb, vb, bias_p, lam, g_head):
    nb, t, _ = q.shape
    tq = ATTN_TQ
    nt = t // tq
    hw = V_HEAD_DIM
    head = pl.BlockSpec((1, t, hw), lambda b, h: (b, 0, h))
    return pl.pallas_call(
        _attn_prompt_kernel,
        out_shape=jax.ShapeDtypeStruct((nb, t, ATTN_WIDTH), BF16),
        grid=(nb, N_HEADS),
        in_specs=[head, head, head,
                  pl.BlockSpec((1, 2, ATTN_TK, ATTN_TQ), lambda b, h: (h, 0, 0, 0)),
                  pl.BlockSpec((8, LANES), lambda b, h: (0, 0)),
                  pl.BlockSpec((1, hw), lambda b, h: (0, 0))],
        out_specs=head,
        scratch_shapes=[pltpu.VMEM((nt, 2 * HEAD_DIM, 2 * tq), BF16),
                        pltpu.VMEM((nt, 1, 2 * tq), F32),
                        pltpu.VMEM((nt, hw + SUM_ROWS, 2 * tq), F32)],
        compiler_params=pltpu.CompilerParams(
            dimension_semantics=("arbitrary", "arbitrary"), vmem_limit_bytes=VMEM_LIMIT),
        name="attn_prompt",
    )(q, kb, vb, bias_p, lam, g_head.reshape(1, hw))


def _attn_sample_kernel(pt_ref, q_ref, kn_ref, vn_ref, bs_ref, b31s_ref, lam_ref, gh_ref, ck_hbm, cv_hbm,
                        o_ref, kraw_ref, vraw_ref, kbf_ref, vbf_ref, sem_ref, *, n_pages, page):
    g = pl.program_id(0)
    n_slots = q_ref.shape[0]

    def page_copies(seq, slot):
        cps = []
        for j in range(n_pages):
            pg = pt_ref[seq, j]
            cps.append(pltpu.make_async_copy(ck_hbm.at[pg], kraw_ref.at[slot, j], sem_ref.at[slot]))
            cps.append(pltpu.make_async_copy(cv_hbm.at[pg], vraw_ref.at[slot, j], sem_ref.at[slot]))
        return cps

    def start(seq, slot):
        for cp in page_copies(seq, slot):
            cp.start()

    def wait(seq, slot):
        for cp in page_copies(seq, slot):
            cp.wait()

    first = g * n_slots

    @pl.when(g == 0)
    def _():
        start(first, 0)

    for slot in range(n_slots):
        if slot + 1 < n_slots:
            start(first + slot + 1, slot + 1)
        else:
            @pl.when(g + 1 < pl.num_programs(0))
            def _():
                start(first + n_slots, 0)
        wait(first + slot, slot)
        _attn_sample_one(slot, q_ref, kn_ref, vn_ref, bs_ref, b31s_ref, lam_ref, gh_ref, o_ref,
                         kraw_ref, vraw_ref, kbf_ref, vbf_ref, n_pages=n_pages, page=page)


def _attn_sample_one(slot, q_ref, kn_ref, vn_ref, bs_ref, b31s_ref, lam_ref, gh_ref, o_ref,
                     kraw_ref, vraw_ref, kbf_ref, vbf_ref, *, n_pages, page):
    past = n_pages * page
    tnew = q_ref.shape[1]
    n_maps = 2 * N_HEADS

    q = q_ref[slot].astype(F32)
    qt = jnp.concatenate([q] * n_maps, axis=0)
    rg = lax.broadcasted_iota(jnp.int32, qt.shape, 0) // tnew
    cg = lax.broadcasted_iota(jnp.int32, qt.shape, 1) // HEAD_DIM
    qbd = jnp.where(rg == cg, qt, 0.0).astype(BF16)

    for j in range(n_pages):
        kbf_ref[:, j * page:(j + 1) * page] = kraw_ref[slot, j].astype(BF16)
        for h in range(N_HEADS):
            vbf_ref[j * page:(j + 1) * page, h * V_HEAD_DIM:(h + 1) * V_HEAD_DIM] = (
                vraw_ref[slot, j, pl.ds(h, page, stride=N_HEADS), :].astype(BF16))
    pad = jnp.zeros((LANES - tnew, QK_WIDTH), F32)
    k_new = jnp.concatenate([kn_ref[slot], pad], axis=0).astype(BF16)
    vbf_ref[past:, :] = jnp.concatenate([vn_ref[slot], pad], axis=0).astype(BF16)

    s_past = jnp.dot(qbd, kbf_ref[...], preferred_element_type=F32)
    s_new = lax.dot_general(qbd, k_new, (((1,), (1,)), ((), ())), preferred_element_type=F32)
    near = past - page
    s = jnp.concatenate([s_past[:, :near] + b31s_ref[:, 0:1],
                         jnp.concatenate([s_past[:, near:], s_new], axis=1) + bs_ref[...]], axis=1)
    mx = jnp.max(s, axis=-1, keepdims=True)
    p = jnp.exp(s - mx)
    pn = p / jnp.sum(p, axis=-1, keepdims=True)
    lam = lam_ref[0:1, 0:1]
    a = jnp.concatenate(
        [pn[2 * h * tnew:(2 * h + 1) * tnew] - lam * pn[(2 * h + 1) * tnew:(2 * h + 2) * tnew]
         for h in range(N_HEADS)], axis=0).astype(BF16)
    r = jnp.dot(a, vbf_ref[...], preferred_element_type=F32)
    outs = []
    for h in range(N_HEADS):
        o = r[h * tnew:(h + 1) * tnew, h * V_HEAD_DIM:(h + 1) * V_HEAD_DIM]
        outs.append(_rms(o, gh_ref[...]) * (1.0 - LAM_INIT))
    o_ref[slot] = jnp.concatenate(outs, axis=1).astype(BF16)


def _attn_sample(page_table, q, k_new, v_new, cache_k, cache_v, bias_s, b31s, lam, g_head):
    nb, tnew, _ = q.shape
    n_pages = page_table.shape[1]
    n_phys, page = cache_k.shape[0], cache_k.shape[1]
    ck = jnp.transpose(cache_k, (0, 2, 3, 4, 1)).reshape(n_phys, QK_WIDTH, page)
    cv = cache_v.reshape(n_phys, page * N_HEADS, V_HEAD_DIM)
    ns = SAMPLE_SLOTS
    tile = lambda w: pl.BlockSpec((ns, tnew, w), lambda b, pt: (b, 0, 0))
    const = lambda shape: pl.BlockSpec(shape, lambda b, pt: (0,) * len(shape))
    hbm = pl.BlockSpec(memory_space=pl.ANY)
    in_specs = [tile(QK_WIDTH), tile(QK_WIDTH), tile(ATTN_WIDTH),
                const(bias_s.shape), const(b31s.shape), const(lam.shape), const((1, V_HEAD_DIM)), hbm, hbm]
    total = n_pages * page + LANES
    return pl.pallas_call(
        functools.partial(_attn_sample_kernel, n_pages=n_pages, page=page),
        out_shape=jax.ShapeDtypeStruct((nb, tnew, ATTN_WIDTH), BF16),
        grid_spec=pltpu.PrefetchScalarGridSpec(
            num_scalar_prefetch=1, grid=(nb // ns,), in_specs=in_specs,
            out_specs=pl.BlockSpec((ns, tnew, ATTN_WIDTH), lambda b, pt: (b, 0, 0)),
            scratch_shapes=[pltpu.VMEM((ns, n_pages, QK_WIDTH, page), F32),
                            pltpu.VMEM((ns, n_pages, page * N_HEADS, V_HEAD_DIM), F32),
                            pltpu.VMEM((QK_WIDTH, n_pages * page), BF16),
                            pltpu.VMEM((total, ATTN_WIDTH), BF16),
                            pltpu.SemaphoreType.DMA((ns,))]),
        compiler_params=pltpu.CompilerParams(dimension_semantics=("arbitrary",),
                                             vmem_limit_bytes=VMEM_LIMIT),
        name="attn_sample",
    )(page_table, q, k_new, v_new, bias_s, b31s, lam, g_head.reshape(1, V_HEAD_DIM), ck, cv)


def _ffn_kernel(*refs, carry, ns, bt, tt):
    if carry:
        (x_ref, yp_ref, at_ref, mod_ref, gpm_ref, gpf_ref, gpo_ref, w_out_ref, w_up_ref, cw_ref, cb_ref,
         w_down_ref, y_ref, cs_ref, *buf_refs, carry_ref) = refs
    else:
        (x_ref, yp_ref, at_ref, mod_ref, gpm_ref, gpf_ref, gpo_ref, w_out_ref, w_up_ref, cw_ref, cb_ref,
         w_down_ref, prefix_ref, y_ref, cs_ref, *buf_refs) = refs
    m = bt * tt
    n_chunks = D_FF // FF_CHUNK
    streams = range(ns)

    def rows(s):
        return (slice(None), slice(s * tt, (s + 1) * tt)) if carry else (slice(s * bt, (s + 1) * bt),)

    def mod_row(s, i):
        return mod_ref[:, i:i + 1, :] if carry else mod_ref[s * bt:(s + 1) * bt, i:i + 1, :]

    def buf(s, slot):
        return buf_refs[4 * s + slot]

    if carry:
        @pl.when(pl.program_id(1) == 0)
        def _():
            carry_ref[...] = jnp.zeros_like(carry_ref)
    halo_rows = CONV_HALO if carry else prefix_ref.shape[1]

    x1, h2 = {}, {}
    for s in streams:
        mix_in = jnp.concatenate([yp_ref[rows(s)], at_ref[rows(s)]], axis=-1).reshape(m, D_MODEL)
        mix = jnp.dot(mix_in, w_out_ref[...], preferred_element_type=F32).reshape(bt, tt, D_MODEL)
        x1[s] = x_ref[rows(s)] + mod_row(s, 2) * _rms(mix, gpm_ref[...])
        h = _rms(x1[s], gpf_ref[...]) * (1.0 + mod_row(s, 4)) + mod_row(s, 3)
        h2[s] = h.reshape(m, D_MODEL).astype(BF16)

    def up_part(s, slot, off):
        sl = slice(off, off + FF_CHUNK)
        u3 = jnp.dot(h2[s], w_up_ref[:, sl], preferred_element_type=F32).reshape(bt, tt, FF_CHUNK)
        if not carry:
            halo = prefix_ref[s * bt:(s + 1) * bt, :, sl]
        elif s == 0:
            halo = carry_ref[:, :, sl]
        else:
            halo = buf(s - 1, slot)[:, tt:tt + CONV_HALO, :]
        buf(s, slot)[:, CONV_HALO - halo_rows:CONV_HALO, :] = halo
        buf(s, slot)[:, CONV_HALO:, :] = u3
        if not carry:
            cs_ref[s * bt:(s + 1) * bt, :, sl] = u3[:, tt - halo_rows:, :]
        elif s == ns - 1:
            cs_ref[:, :, sl] = u3[:, tt - halo_rows:, :]

    def conv_part(s, slot, off):
        sl = slice(off, off + FF_CHUNK)
        b = buf(s, slot)
        y = (cb_ref[:, :, sl]
             + cw_ref[0:1, :, sl] * b[:, CONV_HALO - 2:CONV_HALO - 2 + tt, :]
             + cw_ref[1:2, :, sl] * b[:, CONV_HALO - 1:CONV_HALO - 1 + tt, :]
             + cw_ref[2:3, :, sl] * b[:, CONV_HALO:, :])
        return y.reshape(m, FF_CHUNK)

    def up_chunk(s, c):
        up_part(s, 2 * (c % 2), c * FF_CHUNK)
        up_part(s, 2 * (c % 2) + 1, D_FF + c * FF_CHUNK)

    f = {s: jnp.zeros((m, D_MODEL), F32) for s in streams}
    for s in streams:
        up_chunk(s, 0)
    for c in range(n_chunks):
        if c + 1 < n_chunks:
            for s in streams:
                up_chunk(s, c + 1)
        for s in streams:
            gate = conv_part(s, 2 * (c % 2), c * FF_CHUNK)
            val = conv_part(s, 2 * (c % 2) + 1, D_FF + c * FF_CHUNK)
            act = (_gelu_tanh(gate) * val).astype(BF16)
            f[s] = f[s] + jnp.dot(act, w_down_ref[c * FF_CHUNK:(c + 1) * FF_CHUNK, :],
                                  preferred_element_type=F32)
    if carry:
        carry_ref[...] = cs_ref[...]
    for s in streams:
        y_ref[rows(s)] = x1[s] + mod_row(s, 5) * _rms(f[s].reshape(bt, tt, D_MODEL), gpo_ref[...])


def _ffn(x, yp, attn, mod, g_post_mix, g_pre_ffn, g_post_ffn, w_out, w_up, conv_w, conv_b, w_down, prefix):
    nb, t, _ = x.shape
    carry = prefix is None
    ns = FFN_STREAMS
    if carry:
        bt, tt = 1, ROW_TILE
        grid = (nb, t // (ns * tt))
        tile = lambda w: pl.BlockSpec((1, ns * tt, w), lambda b, i: (b, i, 0))
        per_b = lambda r, w: pl.BlockSpec((1, r, w), lambda b, i: (b, 0, 0))
        sem = ("arbitrary", "arbitrary")
    else:
        bt, tt = ROW_TILE // t, t
        grid = (nb // (ns * bt),)
        tile = lambda w: pl.BlockSpec((ns * bt, tt, w), lambda i: (i, 0, 0))
        per_b = lambda r, w: pl.BlockSpec((ns * bt, r, w), lambda i: (i, 0, 0))
        sem = ("arbitrary",)
    gspec = _const_spec((1, 1, D_MODEL))
    in_specs = [tile(D_MODEL), tile(POOL_WIDTH), tile(ATTN_WIDTH), per_b(N_MOD, D_MODEL),
                gspec, gspec, gspec,
                _const_spec((D_MODEL, D_MODEL)), _const_spec((D_MODEL, 2 * D_FF)),
                _const_spec((CONV_WIDTH, 1, 2 * D_FF)), _const_spec((1, 1, 2 * D_FF)),
                _const_spec((D_FF, D_MODEL))]
    args = [x, yp, attn, mod, g_post_mix.reshape(1, 1, -1), g_pre_ffn.reshape(1, 1, -1),
            g_post_ffn.reshape(1, 1, -1), w_out, w_up, conv_w.reshape(CONV_WIDTH, 1, -1),
            conv_b.reshape(1, 1, -1), w_down]
    scratch = [pltpu.VMEM((bt, CONV_HALO + tt, FF_CHUNK), F32) for _ in range(4 * ns)]
    if carry:
        halo_rows = CONV_HALO
        scratch.append(pltpu.VMEM((1, halo_rows, 2 * D_FF), F32))
    else:
        halo_rows = prefix.shape[1]
        in_specs.append(per_b(halo_rows, 2 * D_FF))
        args.append(prefix)
    return pl.pallas_call(
        functools.partial(_ffn_kernel, carry=carry, ns=ns, bt=bt, tt=tt),
        out_shape=(jax.ShapeDtypeStruct((nb, t, D_MODEL), F32),
                   jax.ShapeDtypeStruct((nb, halo_rows, 2 * D_FF), F32)),
        grid=grid, in_specs=in_specs,
        out_specs=(tile(D_MODEL), per_b(halo_rows, 2 * D_FF)),
        scratch_shapes=scratch,
        compiler_params=pltpu.CompilerParams(dimension_semantics=sem, vmem_limit_bytes=VMEM_LIMIT),
        name="ffn_prompt" if carry else "ffn_sample",
    )(*args)


def kernel(x_prompt, x_sample, c_prompt, c_sample, cache_k, cache_v, page_table, state_pool, state_conv,
           w_ada, b_ada, g_pre_mix, g_post_mix, g_pre_ffn, g_post_ffn, w_in, w_out, w_pool, pool_scale,
           lam_q1, lam_k1, lam_q2, lam_k2, g_head, rel_bias, w_up, conv_w, conv_b, w_down):
    nbp, seq, _ = x_prompt.shape
    nbs, dec_seq, _ = x_sample.shape
    past_len = page_table.shape[1] * cache_k.shape[2]
    l = 0

    lam, bias_p, bias_s, b31s = _setup(rel_bias, lam_q1[l:l + 1], lam_k1[l:l + 1],
                                             lam_q2[l:l + 1], lam_k2[l:l + 1])
    mod = _modulation(jnp.concatenate([c_prompt, c_sample], axis=0), w_ada[l], b_ada[l])
    mod = mod.reshape(nbp + nbs, N_MOD, D_MODEL)
    mod_p, mod_s = mod[:nbp], mod[nbp:]

    w_in_b = w_in[l].astype(BF16)
    w_kt_b = w_in[l][:, POOL_WIDTH + QK_WIDTH:POOL_WIDTH + 2 * QK_WIDTH].T.astype(BF16)
    w_pool_b = w_pool[l].astype(BF16)
    w_out_b = w_out[l].astype(BF16)
    w_up_b = w_up[l].astype(BF16)
    w_down_b = w_down[l].astype(BF16)

    pool_prefix = jnp.pad(state_pool[l], ((0, 0), (POOL_HALO - POOL_STATE, 0), (0, 0)))

    outs = []
    for (x, md, pprefix, cprefix, pos0) in ((x_prompt, mod_p, None, None, 0),
                                            (x_sample, mod_s, pool_prefix, state_conv[l], past_len)):
        nb, t = x.shape[0], x.shape[1]
        res = _inproj(x, md, g_pre_mix[l], w_in_b, w_kt_b, w_pool_b, pool_scale[l], pprefix, pos0=pos0)
        if pprefix is None:
            q, kt, v, kb, vb, yp, pstate = res
            attn = _attn_prompt(q, kb, vb, bias_p, lam, g_head[l])
            k = jnp.transpose(kt.reshape(nb, N_HEADS, 2, HEAD_DIM, t), (0, 4, 1, 2, 3))
        else:
            q, k, v, yp, pstate = res
            attn = _attn_sample(page_table, q, k, v, cache_k[l], cache_v[l], bias_s, b31s, lam, g_head[l])
        y, cstate = _ffn(x, yp, attn, md, g_post_mix[l], g_pre_ffn[l], g_post_ffn[l],
                         w_out_b, w_up_b, conv_w[l], conv_b[l], w_down_b, cprefix)
        outs.append((y,
                     k.reshape(1, nb, t, N_HEADS, 2, HEAD_DIM),
                     v.reshape(1, nb, t, N_HEADS, V_HEAD_DIM),
                     pstate[None, :, POOL_HALO - POOL_STATE:, :],
                     cstate[None, :, cstate.shape[1] - (CONV_WIDTH - 1):, :]))
    (yp_, kp, vp, pp, cp), (ys_, ks, vs, ps, cs) = outs
    return (yp_, ys_, kp, vp, pp, cp, ks, vs, ps, cs)
```

```python
import functools
import math

import jax
import jax.numpy as jnp
from jax import lax
from jax.experimental import pallas as pl
from jax.experimental.pallas import tpu as pltpu

F32 = jnp.float32
BF16 = jnp.bfloat16

D_MODEL = 1024
POOL_WIDTH = 512
POOL_WINDOWS = (2, 4, 8, 16)
POOL_GROUP_DIM = 128
POOL_STATE = 15
POOL_HALO = 16
N_HEADS = 4
HEAD_DIM = 64
V_HEAD_DIM = 128
QK_WIDTH = 512
ATTN_WIDTH = 512
PROJ_WIDTH = 2048
D_FF = 2816
CONV_WIDTH = 3
CONV_HALO = 8
N_BUCKETS = 32
MAX_EXACT = N_BUCKETS // 2
MAX_DISTANCE = 128
N_MOD = 6
EPS = 1e-6
LAM_INIT = 0.8 - 0.6 * math.exp(-0.3 * 0)
NEG_BIG = -1e30
LOG2E = math.log2(math.e)
SUM_ROWS = 16

LANES = 128
MXU_TILE = 256
VMEM_LIMIT = 56 * 1024 * 1024

ATTN_TQ = 256
ATTN_TK = 256
ATTN_CHAIN_GAP = 4
ATTN_LOOKAHEAD = 2
ROW_TILE = 256
FF_CHUNK = 256
SAMPLE_SLOTS = 2
FFN_STREAMS = 2


def _const_spec(shape):
    nd = len(shape)
    return pl.BlockSpec(shape, lambda *_: (0,) * nd, pipeline_mode=pl.Buffered(1))


def _rms(x, g):
    return x * lax.rsqrt(jnp.mean(x * x, axis=-1, keepdims=True) + EPS) * g


def _gelu_tanh(x):
    c0 = -2.0 * math.sqrt(2.0 / math.pi) * LOG2E
    return x / (1.0 + jnp.exp2(x * (c0 + (c0 * 0.044715) * (x * x))))


def _bucket(dist):
    d = jnp.maximum(dist, 1).astype(F32)
    large = MAX_EXACT + jnp.floor(jnp.log(d / MAX_EXACT) / math.log(MAX_DISTANCE / MAX_EXACT)
                                  * (N_BUCKETS - MAX_EXACT)).astype(jnp.int32)
    large = jnp.minimum(large, N_BUCKETS - 1)
    return jnp.where(dist < MAX_EXACT, dist, large)


def _setup_kernel(rb_ref, lq1_ref, lk1_ref, lq2_ref, lk2_ref,
                  lam_ref, bp_ref, bs_ref, b31s_ref):
    lam = (jnp.exp(jnp.sum(lq1_ref[...] * lk1_ref[...], axis=-1, keepdims=True))
           - jnp.exp(jnp.sum(lq2_ref[...] * lk2_ref[...], axis=-1, keepdims=True)) + LAM_INIT)
    lam_ref[...] = jnp.broadcast_to(lam, lam_ref.shape)

    def lookup(bucket, h):
        out = jnp.full(bucket.shape, rb_ref[N_BUCKETS - 1, h], F32)
        for b in range(N_BUCKETS - 1):
            out = jnp.where(bucket == b, rb_ref[b, h], out)
        return out

    kk = lax.broadcasted_iota(jnp.int32, (ATTN_TK, ATTN_TQ), 0)
    r = lax.broadcasted_iota(jnp.int32, (ATTN_TK, ATTN_TQ), 1)
    for typ in range(2):
        dist = typ * ATTN_TK + r - kk
        bucket = _bucket(jnp.maximum(dist, 0))
        for h in range(N_HEADS):
            shifted = (lookup(bucket, h) - rb_ref[N_BUCKETS - 1, h]) * LOG2E
            bp_ref[h, typ] = jnp.where(dist >= 0, shifted, NEG_BIG)

    rows = 2 * 8
    t = lax.broadcasted_iota(jnp.int32, (rows, 2 * LANES), 0) % 8
    col = lax.broadcasted_iota(jnp.int32, (rows, 2 * LANES), 1)
    dist = jnp.where(col < LANES, LANES + t - col, t - (col - LANES))
    bucket = _bucket(jnp.maximum(dist, 0))
    for h in range(N_HEADS):
        bs_ref[h * rows:(h + 1) * rows, :] = jnp.where(dist >= 0, lookup(bucket, h), NEG_BIG)
        b31s_ref[h * rows:(h + 1) * rows, :] = jnp.full((rows, LANES), rb_ref[N_BUCKETS - 1, h], F32)


def _setup(rel_bias, lq1, lk1, lq2, lk2):
    vspec = pl.BlockSpec(memory_space=pltpu.VMEM)
    return pl.pallas_call(
        _setup_kernel,
        out_shape=(jax.ShapeDtypeStruct((8, LANES), F32),
                   jax.ShapeDtypeStruct((N_HEADS, 2, ATTN_TK, ATTN_TQ), F32),
                   jax.ShapeDtypeStruct((N_HEADS * 16, 2 * LANES), F32),
                   jax.ShapeDtypeStruct((N_HEADS * 16, LANES), F32)),
        in_specs=[pl.BlockSpec(memory_space=pltpu.SMEM), vspec, vspec, vspec, vspec],
        out_specs=(vspec, vspec, vspec, vspec),
        name="setup",
    )(rel_bias, lq1, lk1, lq2, lk2)


def _mod_kernel(c_ref, w_ref, b_ref, o_ref):
    c = c_ref[...]
    s = (c * jax.nn.sigmoid(c)).astype(BF16)
    o_ref[...] = jnp.dot(s, w_ref[...].astype(BF16), preferred_element_type=F32) + b_ref[...]


def _modulation(c_all, w_ada, b_ada):
    n = c_all.shape[0]
    tn = 1024
    return pl.pallas_call(
        _mod_kernel,
        out_shape=jax.ShapeDtypeStruct((n, N_MOD * D_MODEL), F32),
        grid=(N_MOD * D_MODEL // tn,),
        in_specs=[pl.BlockSpec((n, D_MODEL), lambda j: (0, 0)),
                  pl.BlockSpec((D_MODEL, tn), lambda j: (0, j)),
                  pl.BlockSpec((1, tn), lambda j: (0, j))],
        out_specs=pl.BlockSpec((n, tn), lambda j: (0, j)),
        compiler_params=pltpu.CompilerParams(dimension_semantics=("arbitrary",),
                                             vmem_limit_bytes=VMEM_LIMIT),
        name="modulation",
    )(c_all, w_ada, b_ada.reshape(1, -1))


def _inproj_kernel(*refs, carry, pos0, bt, tt):
    if carry:
        (x_ref, mod_ref, g_ref, w_in_ref, w_pool_ref, ps_ref, wkt_ref,
         q_ref, k_ref, v_ref, kb_ref, vb_ref, yp_ref, st_ref, ext_ref, carry_ref) = refs
    else:
        (x_ref, mod_ref, g_ref, w_in_ref, w_pool_ref, ps_ref, prefix_ref,
         q_ref, k_ref, v_ref, yp_ref, st_ref, ext_ref) = refs
    m = bt * tt
    x = x_ref[...]
    h = _rms(x, g_ref[...]) * (1.0 + mod_ref[:, 1:2, :]) + mod_ref[:, 0:1, :]
    h2 = h.reshape(m, D_MODEL).astype(BF16)

    def proj(off, width):
        return jnp.dot(h2, w_in_ref[:, off:off + width], preferred_element_type=F32)

    u3 = proj(0, POOL_WIDTH).reshape(bt, tt, POOL_WIDTH)
    if carry:
        @pl.when(pl.program_id(1) == 0)
        def _():
            carry_ref[...] = jnp.zeros_like(carry_ref)
        ext_ref[:, 0:POOL_HALO, :] = carry_ref[...]
    else:
        ext_ref[:, 0:POOL_HALO, :] = prefix_ref[...]
    ext_ref[:, POOL_HALO:, :] = u3
    if carry:
        carry_ref[...] = u3[:, tt - POOL_HALO:, :]
        t_base = pl.program_id(1) * tt
    else:
        t_base = 0
    st_ref[...] = ext_ref[:, tt:tt + POOL_HALO, :]

    zq = proj(POOL_WIDTH, QK_WIDTH)
    q_scale = HEAD_DIM ** -0.5 * (LOG2E if carry else 1.0)
    q_ref[...] = (zq * q_scale).astype(BF16).reshape(bt, tt, QK_WIDTH)
    zv = proj(POOL_WIDTH + 2 * QK_WIDTH, ATTN_WIDTH)
    if carry:
        vb_ref[...] = zv.astype(BF16).reshape(bt, tt, ATTN_WIDTH)
        for hd in range(N_HEADS):
            v_ref[0, pl.ds(hd, tt, stride=N_HEADS), :] = zv[:, hd * V_HEAD_DIM:(hd + 1) * V_HEAD_DIM]
    else:
        v_ref[...] = zv.reshape(bt, tt, ATTN_WIDTH)

    pos = pos0 + t_base + lax.broadcasted_iota(jnp.int32, (bt, tt, POOL_GROUP_DIM), 1)
    for g, w in enumerate(POOL_WINDOWS):
        sl = slice(g * POOL_GROUP_DIM, (g + 1) * POOL_GROUP_DIM)
        tok = ext_ref[:, POOL_HALO:, sl]
        acc = tok
        for j in range(1, w):
            acc = acc + ext_ref[:, POOL_HALO - j:POOL_HALO - j + tt, sl]
        cnt = jnp.minimum(pos + 1, w).astype(F32)
        d = (acc / cnt - tok).reshape(m, POOL_GROUP_DIM).astype(BF16)
        y = jnp.dot(d, w_pool_ref[g], preferred_element_type=F32) * ps_ref[:, sl]
        yp_ref[:, :, sl] = y.astype(BF16).reshape(bt, tt, POOL_GROUP_DIM)

    if carry:
        zkt = lax.dot_general(wkt_ref[...], h2, (((1,), (1,)), ((), ())), preferred_element_type=F32)
        k_ref[0] = zkt
        kb_ref[...] = zkt.T.astype(BF16).reshape(bt, tt, QK_WIDTH)
    else:
        k_ref[...] = proj(POOL_WIDTH + QK_WIDTH, QK_WIDTH).reshape(bt, tt, QK_WIDTH)


def _inproj(x, mod, g_pre, w_in, w_kt, w_pool, pool_scale, prefix, *, pos0):
    nb, t, _ = x.shape
    carry = prefix is None
    if carry:
        bt, tt = 1, ROW_TILE
        grid = (nb, t // tt)
        tile = lambda w: pl.BlockSpec((bt, tt, w), lambda b, i: (b, i, 0))
        per_b = lambda r, w: pl.BlockSpec((bt, r, w), lambda b, i: (b, 0, 0))
        sem = ("arbitrary", "arbitrary")
    else:
        bt, tt = ROW_TILE // t, t
        grid = (nb // bt,)
        tile = lambda w: pl.BlockSpec((bt, tt, w), lambda i: (i, 0, 0))
        per_b = lambda r, w: pl.BlockSpec((bt, r, w), lambda i: (i, 0, 0))
        sem = ("arbitrary",)
    in_specs = [tile(D_MODEL), per_b(N_MOD, D_MODEL), _const_spec((1, 1, D_MODEL)),
                _const_spec((D_MODEL, PROJ_WIDTH)),
                _const_spec((len(POOL_WINDOWS), POOL_GROUP_DIM, POOL_GROUP_DIM)),
                _const_spec((1, POOL_WIDTH))]
    args = [x, mod, g_pre.reshape(1, 1, D_MODEL), w_in, w_pool, pool_scale.reshape(1, POOL_WIDTH)]
    scratch = [pltpu.VMEM((bt, POOL_HALO + tt, POOL_WIDTH), F32)]
    st_shape = jax.ShapeDtypeStruct((nb, POOL_HALO, POOL_WIDTH), F32)
    act = lambda w, dt: jax.ShapeDtypeStruct((nb, t, w), dt)
    if carry:
        scratch.append(pltpu.VMEM((1, POOL_HALO, POOL_WIDTH), F32))
        in_specs.append(_const_spec((QK_WIDTH, D_MODEL)))
        args.append(w_kt)
        out_shape = (act(QK_WIDTH, BF16),
                     jax.ShapeDtypeStruct((nb, QK_WIDTH, t), F32),
                     jax.ShapeDtypeStruct((nb, t * N_HEADS, V_HEAD_DIM), F32), act(QK_WIDTH, BF16),
                     act(ATTN_WIDTH, BF16), act(POOL_WIDTH, BF16), st_shape)
        out_specs = (tile(QK_WIDTH),
                     pl.BlockSpec((1, QK_WIDTH, tt), lambda b, i: (b, 0, i)),
                     pl.BlockSpec((1, tt * N_HEADS, V_HEAD_DIM), lambda b, i: (b, i, 0)), tile(QK_WIDTH),
                     tile(ATTN_WIDTH), tile(POOL_WIDTH), per_b(POOL_HALO, POOL_WIDTH))
    else:
        in_specs.append(per_b(POOL_HALO, POOL_WIDTH))
        args.append(prefix)
        out_shape = (act(QK_WIDTH, BF16), act(QK_WIDTH, F32), act(ATTN_WIDTH, F32),
                     act(POOL_WIDTH, BF16), st_shape)
        out_specs = (tile(QK_WIDTH), tile(QK_WIDTH), tile(ATTN_WIDTH), tile(POOL_WIDTH),
                     per_b(POOL_HALO, POOL_WIDTH))
    return pl.pallas_call(
        functools.partial(_inproj_kernel, carry=carry, pos0=pos0, bt=bt, tt=tt),
        out_shape=out_shape, grid=grid, in_specs=in_specs, out_specs=out_specs,
        scratch_shapes=scratch,
        compiler_params=pltpu.CompilerParams(dimension_semantics=sem, vmem_limit_bytes=VMEM_LIMIT),
        name="inproj_prompt" if carry else "inproj_sample",
    )(*args)


def _attn_unit_order(nt):
    left = {i: i + 1 for i in range(nt)}
    last = {i: -ATTN_CHAIN_GAP for i in range(nt)}
    order = []
    while any(left.values()):
        n = len(order)
        cands = [i for i in range(nt) if left[i]]
        spaced = [i for i in cands if n - last[i] >= ATTN_CHAIN_GAP]
        i = max(spaced, key=lambda t: (left[t], t)) if spaced else max(cands, key=lambda t: n - last[t])
        order.append((i, i + 1 - left[i]))
        left[i] -= 1
        last[i] = n
    return order


def _attn_prompt_kernel(q_ref, kb_ref, vb_ref, bp_ref, lam_ref, gh_ref, o_ref,
                        w_ref, vt_ref, m_ref, acc_ref):
    tq, tk = ATTN_TQ, ATTN_TK
    nt = q_ref.shape[1] // tq
    zero = jnp.zeros((HEAD_DIM, tq), BF16)
    for i in range(nt):
        qt = q_ref[0, i * tq:(i + 1) * tq, :].astype(F32).T.astype(BF16)
        w_ref[i] = jnp.concatenate([jnp.concatenate([qt[:HEAD_DIM], zero], axis=1),
                                    jnp.concatenate([zero, qt[HEAD_DIM:]], axis=1)], axis=0)
    lam = lam_ref[0:1, 0:1]

    def scores(i, j):
        return jnp.dot(kb_ref[0, j * tk:(j + 1) * tk, :], w_ref[i], preferred_element_type=F32)

    for j in range(nt):
        vt_ref[j] = jnp.concatenate([vb_ref[0, j * tk:(j + 1) * tk, :].astype(F32).T,
                                     jnp.ones((SUM_ROWS, tk), F32)], axis=0).astype(BF16)

    units = _attn_unit_order(nt)
    pending = [scores(*u) for u in units[:ATTN_LOOKAHEAD]]
    for n, (i, j) in enumerate(units):
        st = pending.pop(0)
        if n + ATTN_LOOKAHEAD < len(units):
            pending.append(scores(*units[n + ATTN_LOOKAHEAD]))
        vt = vt_ref[j]
        if i - j < 2:
            b = bp_ref[0, i - j]
            st = jnp.concatenate([b, b], axis=1) + st
        mx = jnp.max(st, axis=0, keepdims=True)
        if j == 0:
            p = jnp.exp2(st - mx)
            acc_ref[i] = jnp.dot(vt, p.astype(BF16), preferred_element_type=F32)
            m_ref[i] = mx
        else:
            m_old = m_ref[i]
            m_new = jnp.maximum(m_old, mx)
            alpha = jnp.exp2(m_old - m_new)
            p = jnp.exp2(st - m_new)
            acc_ref[i] = alpha * acc_ref[i] + jnp.dot(vt, p.astype(BF16), preferred_element_type=F32)
            m_ref[i] = m_new
        if i == j:
            acc = acc_ref[j]
            on = acc[:V_HEAD_DIM] / acc[V_HEAD_DIM:V_HEAD_DIM + 1]
            ot = on[:, :tq] - lam * on[:, tq:]
            ot = ot * lax.rsqrt(jnp.mean(ot * ot, axis=0, keepdims=True) + EPS)
            o_ref[0, j * tq:(j + 1) * tq, :] = (ot.T * gh_ref[...] * (1.0 - LAM_INIT)).astype(BF16)


def _attn_prompt(q, kb, vb, bias_p, lam, g_head):
    nb, t, _ = q.shape
    tq = ATTN_TQ
    nt = t // tq
    hw = V_HEAD_DIM
    head = pl.BlockSpec((1, t, hw), lambda b, h: (b, 0, h))
    return pl.pallas_call(
        _attn_prompt_kernel,
        out_shape=jax.ShapeDtypeStruct((nb, t, ATTN_WIDTH), BF16),
        grid=(nb, N_HEADS),
        in_specs=[head, head, head,
                  pl.BlockSpec((1, 2, ATTN_TK, ATTN_TQ), lambda b, h: (h, 0, 0, 0)),
                  pl.BlockSpec((8, LANES), lambda b, h: (0, 0)),
                  pl.BlockSpec((1, hw), lambda b, h: (0, 0))],
        out_specs=head,
        scratch_shapes=[pltpu.VMEM((nt, 2 * HEAD_DIM, 2 * tq), BF16),
                        pltpu.VMEM((nt, hw + SUM_ROWS, ATTN_TK), BF16),
                        pltpu.VMEM((nt, 1, 2 * tq), F32),
                        pltpu.VMEM((nt, hw + SUM_ROWS, 2 * tq), F32)],
        compiler_params=pltpu.CompilerParams(
            dimension_semantics=("arbitrary", "arbitrary"), vmem_limit_bytes=VMEM_LIMIT),
        name="attn_prompt",
    )(q, kb, vb, bias_p, lam, g_head.reshape(1, hw))


def _attn_sample_kernel(pt_ref, q_ref, kn_ref, vn_ref, bs_ref, b31s_ref, lam_ref, gh_ref, ck_hbm, cv_hbm,
                        o_ref, kraw_ref, vraw_ref, kbf_ref, vbf_ref, sem_ref, *, n_pages, page):
    g = pl.program_id(0)
    n_slots = q_ref.shape[0]

    def page_copies(seq, slot):
        cps = []
        for j in range(n_pages):
            pg = pt_ref[seq, j]
            cps.append(pltpu.make_async_copy(ck_hbm.at[pg], kraw_ref.at[slot, j], sem_ref.at[slot]))
            cps.append(pltpu.make_async_copy(cv_hbm.at[pg], vraw_ref.at[slot, j], sem_ref.at[slot]))
        return cps

    def start(seq, slot):
        for n, cp in enumerate(page_copies(seq, slot)):
            cp.start(priority=n % 2)

    def wait(seq, slot):
        for cp in page_copies(seq, slot):
            cp.wait()

    first = g * n_slots

    @pl.when(g == 0)
    def _():
        start(first, 0)

    for slot in range(n_slots):
        if slot + 1 < n_slots:
            start(first + slot + 1, slot + 1)
        else:
            @pl.when(g + 1 < pl.num_programs(0))
            def _():
                start(first + n_slots, 0)
        wait(first + slot, slot)
        _attn_sample_one(slot, q_ref, kn_ref, vn_ref, bs_ref, b31s_ref, lam_ref, gh_ref, o_ref,
                         kraw_ref, vraw_ref, kbf_ref, vbf_ref, n_pages=n_pages, page=page)


def _attn_sample_one(slot, q_ref, kn_ref, vn_ref, bs_ref, b31s_ref, lam_ref, gh_ref, o_ref,
                     kraw_ref, vraw_ref, kbf_ref, vbf_ref, *, n_pages, page):
    past = n_pages * page
    tnew = q_ref.shape[1]
    n_maps = 2 * N_HEADS

    q = q_ref[slot].astype(F32)
    qt = jnp.concatenate([q] * n_maps, axis=0)
    rg = lax.broadcasted_iota(jnp.int32, qt.shape, 0) // tnew
    cg = lax.broadcasted_iota(jnp.int32, qt.shape, 1) // HEAD_DIM
    qbd = jnp.where(rg == cg, qt, 0.0).astype(BF16)

    for j in range(n_pages):
        kbf_ref[:, j * page:(j + 1) * page] = kraw_ref[slot, j].astype(BF16)
        for h in range(N_HEADS):
            vbf_ref[j * page:(j + 1) * page, h * V_HEAD_DIM:(h + 1) * V_HEAD_DIM] = (
                vraw_ref[slot, j, pl.ds(h, page, stride=N_HEADS), :].astype(BF16))
    pad = jnp.zeros((LANES - tnew, QK_WIDTH), F32)
    k_new = jnp.concatenate([kn_ref[slot], pad], axis=0).astype(BF16)
    vbf_ref[past:, :] = jnp.concatenate([vn_ref[slot], pad], axis=0).astype(BF16)

    s_past = jnp.dot(qbd, kbf_ref[...], preferred_element_type=F32)
    s_new = lax.dot_general(qbd, k_new, (((1,), (1,)), ((), ())), preferred_element_type=F32)
    near = past - page
    s = jnp.concatenate([s_past[:, :near] + b31s_ref[:, 0:1],
                         jnp.concatenate([s_past[:, near:], s_new], axis=1) + bs_ref[...]], axis=1)
    mx = jnp.max(s, axis=-1, keepdims=True)
    p = jnp.exp(s - mx)
    pn = p / jnp.sum(p, axis=-1, keepdims=True)
    lam = lam_ref[0:1, 0:1]
    a = jnp.concatenate(
        [pn[2 * h * tnew:(2 * h + 1) * tnew] - lam * pn[(2 * h + 1) * tnew:(2 * h + 2) * tnew]
         for h in range(N_HEADS)], axis=0).astype(BF16)
    r = jnp.dot(a, vbf_ref[...], preferred_element_type=F32)
    outs = []
    for h in range(N_HEADS):
        o = r[h * tnew:(h + 1) * tnew, h * V_HEAD_DIM:(h + 1) * V_HEAD_DIM]
        outs.append(_rms(o, gh_ref[...]) * (1.0 - LAM_INIT))
    o_ref[slot] = jnp.concatenate(outs, axis=1).astype(BF16)


def _attn_sample(page_table, q, k_new, v_new, cache_k, cache_v, bias_s, b31s, lam, g_head):
    nb, tnew, _ = q.shape
    n_pages = page_table.shape[1]
    n_phys, page = cache_k.shape[0], cache_k.shape[1]
    ck = jnp.transpose(cache_k, (0, 2, 3, 4, 1)).reshape(n_phys, QK_WIDTH, page)
    cv = cache_v.reshape(n_phys, page * N_HEADS, V_HEAD_DIM)
    ns = SAMPLE_SLOTS
    tile = lambda w: pl.BlockSpec((ns, tnew, w), lambda b, pt: (b, 0, 0))
    const = lambda shape: pl.BlockSpec(shape, lambda b, pt: (0,) * len(shape))
    hbm = pl.BlockSpec(memory_space=pl.ANY)
    in_specs = [tile(QK_WIDTH), tile(QK_WIDTH), tile(ATTN_WIDTH),
                const(bias_s.shape), const(b31s.shape), const(lam.shape), const((1, V_HEAD_DIM)), hbm, hbm]
    total = n_pages * page + LANES
    return pl.pallas_call(
        functools.partial(_attn_sample_kernel, n_pages=n_pages, page=page),
        out_shape=jax.ShapeDtypeStruct((nb, tnew, ATTN_WIDTH), BF16),
        grid_spec=pltpu.PrefetchScalarGridSpec(
            num_scalar_prefetch=1, grid=(nb // ns,), in_specs=in_specs,
            out_specs=pl.BlockSpec((ns, tnew, ATTN_WIDTH), lambda b, pt: (b, 0, 0)),
            scratch_shapes=[pltpu.VMEM((ns, n_pages, QK_WIDTH, page), F32),
                            pltpu.VMEM((ns, n_pages, page * N_HEADS, V_HEAD_DIM), F32),
                            pltpu.VMEM((QK_WIDTH, n_pages * page), BF16),
                            pltpu.VMEM((total, ATTN_WIDTH), BF16),
                            pltpu.SemaphoreType.DMA((ns,))]),
        compiler_params=pltpu.CompilerParams(dimension_semantics=("arbitrary",),
                                             vmem_limit_bytes=VMEM_LIMIT),
        name="attn_sample",
    )(page_table, q, k_new, v_new, bias_s, b31s, lam, g_head.reshape(1, V_HEAD_DIM), ck, cv)


def _ffn_kernel(*refs, carry, ns, bt, tt):
    if carry:
        (x_ref, yp_ref, at_ref, mod_ref, gpm_ref, gpf_ref, gpo_ref, w_out_ref, w_up_ref, cw_ref, cb_ref,
         w_down_ref, y_ref, cs_ref, *buf_refs, carry_ref) = refs
    else:
        (x_ref, yp_ref, at_ref, mod_ref, gpm_ref, gpf_ref, gpo_ref, w_out_ref, w_up_ref, cw_ref, cb_ref,
         w_down_ref, prefix_ref, y_ref, cs_ref, *buf_refs) = refs
    m = bt * tt
    n_chunks = D_FF // FF_CHUNK
    streams = range(ns)

    def rows(s):
        return (slice(None), slice(s * tt, (s + 1) * tt)) if carry else (slice(s * bt, (s + 1) * bt),)

    def mod_row(s, i):
        return mod_ref[:, i:i + 1, :] if carry else mod_ref[s * bt:(s + 1) * bt, i:i + 1, :]

    def buf(s, slot):
        return buf_refs[4 * s + slot]

    if carry:
        @pl.when(pl.program_id(1) == 0)
        def _():
            carry_ref[...] = jnp.zeros_like(carry_ref)
    halo_rows = CONV_HALO if carry else prefix_ref.shape[1]

    x1, h2 = {}, {}
    for s in streams:
        mix_in = jnp.concatenate([yp_ref[rows(s)], at_ref[rows(s)]], axis=-1).reshape(m, D_MODEL)
        mix = jnp.dot(mix_in, w_out_ref[...], preferred_element_type=F32).reshape(bt, tt, D_MODEL)
        x1[s] = x_ref[rows(s)] + mod_row(s, 2) * _rms(mix, gpm_ref[...])
        h = _rms(x1[s], gpf_ref[...]) * (1.0 + mod_row(s, 4)) + mod_row(s, 3)
        h2[s] = h.reshape(m, D_MODEL).astype(BF16)

    def up_part(s, slot, off):
        sl = slice(off, off + FF_CHUNK)
        u3 = jnp.dot(h2[s], w_up_ref[:, sl], preferred_element_type=F32).reshape(bt, tt, FF_CHUNK)
        if not carry:
            halo = prefix_ref[s * bt:(s + 1) * bt, :, sl]
        elif s == 0:
            halo = carry_ref[:, :, sl]
        else:
            halo = buf(s - 1, slot)[:, tt:tt + CONV_HALO, :]
        buf(s, slot)[:, CONV_HALO - halo_rows:CONV_HALO, :] = halo
        buf(s, slot)[:, CONV_HALO:, :] = u3
        if not carry:
            cs_ref[s * bt:(s + 1) * bt, :, sl] = u3[:, tt - halo_rows:, :]
        elif s == ns - 1:
            cs_ref[:, :, sl] = u3[:, tt - halo_rows:, :]

    def conv_part(s, slot, off):
        sl = slice(off, off + FF_CHUNK)
        b = buf(s, slot)
        y = (cb_ref[:, :, sl]
             + cw_ref[0:1, :, sl] * b[:, CONV_HALO - 2:CONV_HALO - 2 + tt, :]
             + cw_ref[1:2, :, sl] * b[:, CONV_HALO - 1:CONV_HALO - 1 + tt, :]
             + cw_ref[2:3, :, sl] * b[:, CONV_HALO:, :])
        return y.reshape(m, FF_CHUNK)

    def up_chunk(s, c):
        up_part(s, 2 * (c % 2), c * FF_CHUNK)
        up_part(s, 2 * (c % 2) + 1, D_FF + c * FF_CHUNK)

    f = {s: jnp.zeros((m, D_MODEL), F32) for s in streams}
    for s in streams:
        up_chunk(s, 0)
    for c in range(n_chunks):
        if c + 1 < n_chunks:
            for s in streams:
                up_chunk(s, c + 1)
        for s in streams:
            gate = conv_part(s, 2 * (c % 2), c * FF_CHUNK)
            val = conv_part(s, 2 * (c % 2) + 1, D_FF + c * FF_CHUNK)
            act = (_gelu_tanh(gate) * val).astype(BF16)
            f[s] = f[s] + jnp.dot(act, w_down_ref[c * FF_CHUNK:(c + 1) * FF_CHUNK, :],
                                  preferred_element_type=F32)
    if carry:
        carry_ref[...] = cs_ref[...]
    for s in streams:
        y_ref[rows(s)] = x1[s] + mod_row(s, 5) * _rms(f[s].reshape(bt, tt, D_MODEL), gpo_ref[...])


def _ffn(x, yp, attn, mod, g_post_mix, g_pre_ffn, g_post_ffn, w_out, w_up, conv_w, conv_b, w_down, prefix):
    nb, t, _ = x.shape
    carry = prefix is None
    ns = FFN_STREAMS
    if carry:
        bt, tt = 1, ROW_TILE
        grid = (nb, t // (ns * tt))
        tile = lambda w: pl.BlockSpec((1, ns * tt, w), lambda b, i: (b, i, 0))
        per_b = lambda r, w: pl.BlockSpec((1, r, w), lambda b, i: (b, 0, 0))
        sem = ("arbitrary", "arbitrary")
    else:
        bt, tt = ROW_TILE // t, t
        grid = (nb // (ns * bt),)
        tile = lambda w: pl.BlockSpec((ns * bt, tt, w), lambda i: (i, 0, 0))
        per_b = lambda r, w: pl.BlockSpec((ns * bt, r, w), lambda i: (i, 0, 0))
        sem = ("arbitrary",)
    gspec = _const_spec((1, 1, D_MODEL))
    in_specs = [tile(D_MODEL), tile(POOL_WIDTH), tile(ATTN_WIDTH), per_b(N_MOD, D_MODEL),
                gspec, gspec, gspec,
                _const_spec((D_MODEL, D_MODEL)), _const_spec((D_MODEL, 2 * D_FF)),
                _const_spec((CONV_WIDTH, 1, 2 * D_FF)), _const_spec((1, 1, 2 * D_FF)),
                _const_spec((D_FF, D_MODEL))]
    args = [x, yp, attn, mod, g_post_mix.reshape(1, 1, -1), g_pre_ffn.reshape(1, 1, -1),
            g_post_ffn.reshape(1, 1, -1), w_out, w_up, conv_w.reshape(CONV_WIDTH, 1, -1),
            conv_b.reshape(1, 1, -1), w_down]
    scratch = [pltpu.VMEM((bt, CONV_HALO + tt, FF_CHUNK), F32) for _ in range(4 * ns)]
    if carry:
        halo_rows = CONV_HALO
        scratch.append(pltpu.VMEM((1, halo_rows, 2 * D_FF), F32))
    else:
        halo_rows = prefix.shape[1]
        in_specs.append(per_b(halo_rows, 2 * D_FF))
        args.append(prefix)
    return pl.pallas_call(
        functools.partial(_ffn_kernel, carry=carry, ns=ns, bt=bt, tt=tt),
        out_shape=(jax.ShapeDtypeStruct((nb, t, D_MODEL), F32),
                   jax.ShapeDtypeStruct((nb, halo_rows, 2 * D_FF), F32)),
        grid=grid, in_specs=in_specs,
        out_specs=(tile(D_MODEL), per_b(halo_rows, 2 * D_FF)),
        scratch_shapes=scratch,
        compiler_params=pltpu.CompilerParams(dimension_semantics=sem, vmem_limit_bytes=VMEM_LIMIT),
        name="ffn_prompt" if carry else "ffn_sample",
    )(*args)


def kernel(x_prompt, x_sample, c_prompt, c_sample, cache_k, cache_v, page_table, state_pool, state_conv,
           w_ada, b_ada, g_pre_mix, g_post_mix, g_pre_ffn, g_post_ffn, w_in, w_out, w_pool, pool_scale,
           lam_q1, lam_k1, lam_q2, lam_k2, g_head, rel_bias, w_up, conv_w, conv_b, w_down):
    nbp, seq, _ = x_prompt.shape
    nbs, dec_seq, _ = x_sample.shape
    past_len = page_table.shape[1] * cache_k.shape[2]
    l = 0

    lam, bias_p, bias_s, b31s = _setup(rel_bias, lam_q1[l:l + 1], lam_k1[l:l + 1],
                                             lam_q2[l:l + 1], lam_k2[l:l + 1])
    mod = _modulation(jnp.concatenate([c_prompt, c_sample], axis=0), w_ada[l], b_ada[l])
    mod = mod.reshape(nbp + nbs, N_MOD, D_MODEL)
    mod_p, mod_s = mod[:nbp], mod[nbp:]

    w_in_b = w_in[l].astype(BF16)
    w_kt_b = w_in[l][:, POOL_WIDTH + QK_WIDTH:POOL_WIDTH + 2 * QK_WIDTH].T.astype(BF16)
    w_pool_b = w_pool[l].astype(BF16)
    w_out_b = w_out[l].astype(BF16)
    w_up_b = w_up[l].astype(BF16)
    w_down_b = w_down[l].astype(BF16)

    pool_prefix = jnp.pad(state_pool[l], ((0, 0), (POOL_HALO - POOL_STATE, 0), (0, 0)))

    outs = []
    for (x, md, pprefix, cprefix, pos0) in ((x_prompt, mod_p, None, None, 0),
                                            (x_sample, mod_s, pool_prefix, state_conv[l], past_len)):
        nb, t = x.shape[0], x.shape[1]
        res = _inproj(x, md, g_pre_mix[l], w_in_b, w_kt_b, w_pool_b, pool_scale[l], pprefix, pos0=pos0)
        if pprefix is None:
            q, kt, v, kb, vb, yp, pstate = res
            attn = _attn_prompt(q, kb, vb, bias_p, lam, g_head[l])
            k = jnp.transpose(kt.reshape(nb, N_HEADS, 2, HEAD_DIM, t), (0, 4, 1, 2, 3))
        else:
            q, k, v, yp, pstate = res
            attn = _attn_sample(page_table, q, k, v, cache_k[l], cache_v[l], bias_s, b31s, lam, g_head[l])
        y, cstate = _ffn(x, yp, attn, md, g_post_mix[l], g_pre_ffn[l], g_post_ffn[l],
                         w_out_b, w_up_b, conv_w[l], conv_b[l], w_down_b, cprefix)
        outs.append((y,
                     k.reshape(1, nb, t, N_HEADS, 2, HEAD_DIM),
                     v.reshape(1, nb, t, N_HEADS, V_HEAD_DIM),
                     pstate[None, :, POOL_HALO - POOL_STATE:, :],
                     cstate[None, :, cstate.shape[1] - (CONV_WIDTH - 1):, :]))
    (yp_, kp, vp, pp, cp), (ys_, ks, vs, ps, cs) = outs
    return (yp_, ys_, kp, vp, pp, cp, ks, vs, ps, cs)
```

```python
import functools
import math

import jax
import jax.numpy as jnp
from jax import lax
from jax.experimental import pallas as pl
from jax.experimental.pallas import tpu as pltpu

F32 = jnp.float32
BF16 = jnp.bfloat16

D_MODEL = 1024
POOL_WIDTH = 512
POOL_WINDOWS = (2, 4, 8, 16)
POOL_GROUP_DIM = 128
POOL_STATE = 15
POOL_HALO = 16
N_HEADS = 4
HEAD_DIM = 64
V_HEAD_DIM = 128
QK_WIDTH = 512
ATTN_WIDTH = 512
PROJ_WIDTH = 2048
D_FF = 2816
CONV_WIDTH = 3
CONV_HALO = 8
N_BUCKETS = 32
MAX_EXACT = N_BUCKETS // 2
MAX_DISTANCE = 128
N_MOD = 6
EPS = 1e-6
LAM_INIT = 0.8 - 0.6 * math.exp(-0.3 * 0)
NEG_BIG = -1e30
LOG2E = math.log2(math.e)
SUM_ROWS = 16

LANES = 128
MXU_TILE = 256
VMEM_LIMIT = 56 * 1024 * 1024

ATTN_TQ = 256
ATTN_TK = 256
ATTN_CHAIN_GAP = 4
ATTN_LOOKAHEAD = 2
ROW_TILE = 256
FF_CHUNK = 256
SAMPLE_SLOTS = 2
FFN_STREAMS = 2


def _const_spec(shape):
    nd = len(shape)
    return pl.BlockSpec(shape, lambda *_: (0,) * nd, pipeline_mode=pl.Buffered(1))


def _rms(x, g):
    return x * lax.rsqrt(jnp.mean(x * x, axis=-1, keepdims=True) + EPS) * g


def _gelu_tanh(x):
    c0 = -2.0 * math.sqrt(2.0 / math.pi) * LOG2E
    return x / (1.0 + jnp.exp2(x * (c0 + (c0 * 0.044715) * (x * x))))


def _bucket(dist):
    d = jnp.maximum(dist, 1).astype(F32)
    large = MAX_EXACT + jnp.floor(jnp.log(d / MAX_EXACT) / math.log(MAX_DISTANCE / MAX_EXACT)
                                  * (N_BUCKETS - MAX_EXACT)).astype(jnp.int32)
    large = jnp.minimum(large, N_BUCKETS - 1)
    return jnp.where(dist < MAX_EXACT, dist, large)


def _setup_kernel(rb_ref, lq1_ref, lk1_ref, lq2_ref, lk2_ref,
                  lam_ref, bp_ref, bs_ref, b31s_ref):
    lam = (jnp.exp(jnp.sum(lq1_ref[...] * lk1_ref[...], axis=-1, keepdims=True))
           - jnp.exp(jnp.sum(lq2_ref[...] * lk2_ref[...], axis=-1, keepdims=True)) + LAM_INIT)
    lam_ref[...] = jnp.broadcast_to(lam, lam_ref.shape)

    def lookup(bucket, h):
        out = jnp.full(bucket.shape, rb_ref[N_BUCKETS - 1, h], F32)
        for b in range(N_BUCKETS - 1):
            out = jnp.where(bucket == b, rb_ref[b, h], out)
        return out

    kk = lax.broadcasted_iota(jnp.int32, (ATTN_TK, ATTN_TQ), 0)
    r = lax.broadcasted_iota(jnp.int32, (ATTN_TK, ATTN_TQ), 1)
    for typ in range(2):
        dist = typ * ATTN_TK + r - kk
        bucket = _bucket(jnp.maximum(dist, 0))
        for h in range(N_HEADS):
            shifted = (lookup(bucket, h) - rb_ref[N_BUCKETS - 1, h]) * LOG2E
            bp_ref[h, typ] = jnp.where(dist >= 0, shifted, NEG_BIG)

    rows = 2 * 8
    t = lax.broadcasted_iota(jnp.int32, (rows, 2 * LANES), 0) % 8
    col = lax.broadcasted_iota(jnp.int32, (rows, 2 * LANES), 1)
    dist = jnp.where(col < LANES, LANES + t - col, t - (col - LANES))
    bucket = _bucket(jnp.maximum(dist, 0))
    for h in range(N_HEADS):
        bs_ref[h * rows:(h + 1) * rows, :] = jnp.where(dist >= 0, lookup(bucket, h), NEG_BIG)
        b31s_ref[h * rows:(h + 1) * rows, :] = jnp.full((rows, LANES), rb_ref[N_BUCKETS - 1, h], F32)


def _setup(rel_bias, lq1, lk1, lq2, lk2):
    vspec = pl.BlockSpec(memory_space=pltpu.VMEM)
    return pl.pallas_call(
        _setup_kernel,
        out_shape=(jax.ShapeDtypeStruct((8, LANES), F32),
                   jax.ShapeDtypeStruct((N_HEADS, 2, ATTN_TK, ATTN_TQ), F32),
                   jax.ShapeDtypeStruct((N_HEADS * 16, 2 * LANES), F32),
                   jax.ShapeDtypeStruct((N_HEADS * 16, LANES), F32)),
        in_specs=[pl.BlockSpec(memory_space=pltpu.SMEM), vspec, vspec, vspec, vspec],
        out_specs=(vspec, vspec, vspec, vspec),
        name="setup",
    )(rel_bias, lq1, lk1, lq2, lk2)


def _mod_kernel(c_ref, w_ref, b_ref, o_ref):
    c = c_ref[...]
    s = (c * jax.nn.sigmoid(c)).astype(BF16)
    o_ref[...] = jnp.dot(s, w_ref[...].astype(BF16), preferred_element_type=F32) + b_ref[...]


def _modulation(c_all, w_ada, b_ada):
    n = c_all.shape[0]
    tn = 1024
    return pl.pallas_call(
        _mod_kernel,
        out_shape=jax.ShapeDtypeStruct((n, N_MOD * D_MODEL), F32),
        grid=(N_MOD * D_MODEL // tn,),
        in_specs=[pl.BlockSpec((n, D_MODEL), lambda j: (0, 0)),
                  pl.BlockSpec((D_MODEL, tn), lambda j: (0, j)),
                  pl.BlockSpec((1, tn), lambda j: (0, j))],
        out_specs=pl.BlockSpec((n, tn), lambda j: (0, j)),
        compiler_params=pltpu.CompilerParams(dimension_semantics=("arbitrary",),
                                             vmem_limit_bytes=VMEM_LIMIT),
        name="modulation",
    )(c_all, w_ada, b_ada.reshape(1, -1))


def _inproj_kernel(*refs, carry, pos0, bt, tt):
    if carry:
        (x_ref, mod_ref, g_ref, w_in_ref, w_pool_ref, ps_ref, wkt_ref,
         q_ref, k_ref, v_ref, kb_ref, vb_ref, yp_ref, st_ref, ext_ref, carry_ref) = refs
    else:
        (x_ref, mod_ref, g_ref, w_in_ref, w_pool_ref, ps_ref, prefix_ref,
         q_ref, k_ref, v_ref, yp_ref, st_ref, ext_ref) = refs
    m = bt * tt
    x = x_ref[...]
    h = _rms(x, g_ref[...]) * (1.0 + mod_ref[:, 1:2, :]) + mod_ref[:, 0:1, :]
    h2 = h.reshape(m, D_MODEL).astype(BF16)

    def proj(off, width):
        return jnp.dot(h2, w_in_ref[:, off:off + width], preferred_element_type=F32)

    u3 = proj(0, POOL_WIDTH).reshape(bt, tt, POOL_WIDTH)
    if carry:
        @pl.when(pl.program_id(1) == 0)
        def _():
            carry_ref[...] = jnp.zeros_like(carry_ref)
        ext_ref[:, 0:POOL_HALO, :] = carry_ref[...]
    else:
        ext_ref[:, 0:POOL_HALO, :] = prefix_ref[...]
    ext_ref[:, POOL_HALO:, :] = u3
    if carry:
        carry_ref[...] = u3[:, tt - POOL_HALO:, :]
        t_base = pl.program_id(1) * tt
    else:
        t_base = 0
    st_ref[...] = ext_ref[:, tt:tt + POOL_HALO, :]

    zq = proj(POOL_WIDTH, QK_WIDTH)
    q_scale = HEAD_DIM ** -0.5 * (LOG2E if carry else 1.0)
    q_ref[...] = (zq * q_scale).astype(BF16).reshape(bt, tt, QK_WIDTH)
    zv = proj(POOL_WIDTH + 2 * QK_WIDTH, ATTN_WIDTH)
    if carry:
        vb_ref[...] = zv.astype(BF16).reshape(bt, tt, ATTN_WIDTH)
        for hd in range(N_HEADS):
            v_ref[0, pl.ds(hd, tt, stride=N_HEADS), :] = zv[:, hd * V_HEAD_DIM:(hd + 1) * V_HEAD_DIM]
    else:
        v_ref[...] = zv.reshape(bt, tt, ATTN_WIDTH)

    pos = pos0 + t_base + lax.broadcasted_iota(jnp.int32, (bt, tt, POOL_GROUP_DIM), 1)
    for g, w in enumerate(POOL_WINDOWS):
        sl = slice(g * POOL_GROUP_DIM, (g + 1) * POOL_GROUP_DIM)
        tok = ext_ref[:, POOL_HALO:, sl]
        acc = tok
        for j in range(1, w):
            acc = acc + ext_ref[:, POOL_HALO - j:POOL_HALO - j + tt, sl]
        cnt = jnp.minimum(pos + 1, w).astype(F32)
        d = (acc / cnt - tok).reshape(m, POOL_GROUP_DIM).astype(BF16)
        y = jnp.dot(d, w_pool_ref[g], preferred_element_type=F32) * ps_ref[:, sl]
        yp_ref[:, :, sl] = y.astype(BF16).reshape(bt, tt, POOL_GROUP_DIM)

    if carry:
        zkt = lax.dot_general(wkt_ref[...], h2, (((1,), (1,)), ((), ())), preferred_element_type=F32)
        k_ref[0] = zkt
        kb_ref[...] = zkt.T.astype(BF16).reshape(bt, tt, QK_WIDTH)
    else:
        k_ref[...] = proj(POOL_WIDTH + QK_WIDTH, QK_WIDTH).reshape(bt, tt, QK_WIDTH)


def _inproj(x, mod, g_pre, w_in, w_kt, w_pool, pool_scale, prefix, *, pos0):
    nb, t, _ = x.shape
    carry = prefix is None
    if carry:
        bt, tt = 1, ROW_TILE
        grid = (nb, t // tt)
        tile = lambda w: pl.BlockSpec((bt, tt, w), lambda b, i: (b, i, 0))
        per_b = lambda r, w: pl.BlockSpec((bt, r, w), lambda b, i: (b, 0, 0))
        sem = ("arbitrary", "arbitrary")
    else:
        bt, tt = ROW_TILE // t, t
        grid = (nb // bt,)
        tile = lambda w: pl.BlockSpec((bt, tt, w), lambda i: (i, 0, 0))
        per_b = lambda r, w: pl.BlockSpec((bt, r, w), lambda i: (i, 0, 0))
        sem = ("arbitrary",)
    in_specs = [tile(D_MODEL), per_b(N_MOD, D_MODEL), _const_spec((1, 1, D_MODEL)),
                _const_spec((D_MODEL, PROJ_WIDTH)),
                _const_spec((len(POOL_WINDOWS), POOL_GROUP_DIM, POOL_GROUP_DIM)),
                _const_spec((1, POOL_WIDTH))]
    args = [x, mod, g_pre.reshape(1, 1, D_MODEL), w_in, w_pool, pool_scale.reshape(1, POOL_WIDTH)]
    scratch = [pltpu.VMEM((bt, POOL_HALO + tt, POOL_WIDTH), F32)]
    st_shape = jax.ShapeDtypeStruct((nb, POOL_HALO, POOL_WIDTH), F32)
    act = lambda w, dt: jax.ShapeDtypeStruct((nb, t, w), dt)
    if carry:
        scratch.append(pltpu.VMEM((1, POOL_HALO, POOL_WIDTH), F32))
        in_specs.append(_const_spec((QK_WIDTH, D_MODEL)))
        args.append(w_kt)
        out_shape = (act(QK_WIDTH, BF16),
                     jax.ShapeDtypeStruct((nb, QK_WIDTH, t), F32),
                     jax.ShapeDtypeStruct((nb, t * N_HEADS, V_HEAD_DIM), F32), act(QK_WIDTH, BF16),
                     act(ATTN_WIDTH, BF16), act(POOL_WIDTH, BF16), st_shape)
        out_specs = (tile(QK_WIDTH),
                     pl.BlockSpec((1, QK_WIDTH, tt), lambda b, i: (b, 0, i)),
                     pl.BlockSpec((1, tt * N_HEADS, V_HEAD_DIM), lambda b, i: (b, i, 0)), tile(QK_WIDTH),
                     tile(ATTN_WIDTH), tile(POOL_WIDTH), per_b(POOL_HALO, POOL_WIDTH))
    else:
        in_specs.append(per_b(POOL_HALO, POOL_WIDTH))
        args.append(prefix)
        out_shape = (act(QK_WIDTH, BF16), act(QK_WIDTH, F32), act(ATTN_WIDTH, F32),
                     act(POOL_WIDTH, BF16), st_shape)
        out_specs = (tile(QK_WIDTH), tile(QK_WIDTH), tile(ATTN_WIDTH), tile(POOL_WIDTH),
                     per_b(POOL_HALO, POOL_WIDTH))
    return pl.pallas_call(
        functools.partial(_inproj_kernel, carry=carry, pos0=pos0, bt=bt, tt=tt),
        out_shape=out_shape, grid=grid, in_specs=in_specs, out_specs=out_specs,
        scratch_shapes=scratch,
        compiler_params=pltpu.CompilerParams(dimension_semantics=sem, vmem_limit_bytes=VMEM_LIMIT),
        name="inproj_prompt" if carry else "inproj_sample",
    )(*args)


def _attn_unit_order(nt):
    left = {i: i + 1 for i in range(nt)}
    last = {i: -ATTN_CHAIN_GAP for i in range(nt)}
    order = []
    while any(left.values()):
        n = len(order)
        cands = [i for i in range(nt) if left[i]]
        spaced = [i for i in cands if n - last[i] >= ATTN_CHAIN_GAP]
        i = max(spaced, key=lambda t: (left[t], t)) if spaced else max(cands, key=lambda t: n - last[t])
        order.append((i, i + 1 - left[i]))
        left[i] -= 1
        last[i] = n
    return order


def _attn_prompt_head(q_ref, kb_ref, vb_ref, bp_ref, lam_ref, gh_ref, o_ref,
                      w_ref, vt_ref, m_ref, acc_ref, after_pair):
    tq, tk = ATTN_TQ, ATTN_TK
    nt = q_ref.shape[1] // tq
    zero = jnp.zeros((HEAD_DIM, tq), BF16)
    for i in range(nt):
        qt = q_ref[0, i * tq:(i + 1) * tq, :].astype(F32).T.astype(BF16)
        w_ref[i] = jnp.concatenate([jnp.concatenate([qt[:HEAD_DIM], zero], axis=1),
                                    jnp.concatenate([zero, qt[HEAD_DIM:]], axis=1)], axis=0)
    lam = lam_ref[0:1, 0:1]

    def scores(i, j):
        return jnp.dot(kb_ref[0, j * tk:(j + 1) * tk, :], w_ref[i], preferred_element_type=F32)

    for j in range(nt):
        vt_ref[j] = jnp.concatenate([vb_ref[0, j * tk:(j + 1) * tk, :].astype(F32).T,
                                     jnp.ones((SUM_ROWS, tk), F32)], axis=0).astype(BF16)

    units = _attn_unit_order(nt)
    pending = [scores(*u) for u in units[:ATTN_LOOKAHEAD]]
    for n, (i, j) in enumerate(units):
        st = pending.pop(0)
        if n + ATTN_LOOKAHEAD < len(units):
            pending.append(scores(*units[n + ATTN_LOOKAHEAD]))
        vt = vt_ref[j]
        if i - j < 2:
            b = bp_ref[0, i - j]
            st = jnp.concatenate([b, b], axis=1) + st
        mx = jnp.max(st, axis=0, keepdims=True)
        if j == 0:
            p = jnp.exp2(st - mx)
            acc_ref[i] = jnp.dot(vt, p.astype(BF16), preferred_element_type=F32)
            m_ref[i] = mx
        else:
            m_old = m_ref[i]
            m_new = jnp.maximum(m_old, mx)
            alpha = jnp.exp2(m_old - m_new)
            p = jnp.exp2(st - m_new)
            acc_ref[i] = alpha * acc_ref[i] + jnp.dot(vt, p.astype(BF16), preferred_element_type=F32)
            m_ref[i] = m_new
        if i == j:
            acc = acc_ref[j]
            on = acc[:V_HEAD_DIM] / acc[V_HEAD_DIM:V_HEAD_DIM + 1]
            ot = on[:, :tq] - lam * on[:, tq:]
            ot = ot * lax.rsqrt(jnp.mean(ot * ot, axis=0, keepdims=True) + EPS)
            o_ref[0, j * tq:(j + 1) * tq, :] = (ot.T * gh_ref[...] * (1.0 - LAM_INIT)).astype(BF16)
        after_pair(n, len(units))


def _attn_kernel(pt_ref, q_ref, kb_ref, vb_ref, bp_ref, lam_ref, gh_ref,
                 qs_ref, kn_ref, vn_ref, bs_ref, b31s_ref, ck_hbm, cv_hbm,
                 o_ref, os_ref,
                 w_ref, vt_ref, m_ref, acc_ref, kraw_ref, vraw_ref, kbf_ref, vbf_ref, sem_ref,
                 *, n_pages, page):
    step = pl.program_id(0) * pl.num_programs(1) + pl.program_id(1)
    n_steps = pl.num_programs(0) * pl.num_programs(1)
    n_seq = qs_ref.shape[0]
    first = step * n_seq

    def page_copies(seq, slot):
        cps = []
        for j in range(n_pages):
            pg = pt_ref[seq, j]
            cps.append(pltpu.make_async_copy(ck_hbm.at[pg], kraw_ref.at[slot, j], sem_ref.at[slot]))
            cps.append(pltpu.make_async_copy(cv_hbm.at[pg], vraw_ref.at[slot, j], sem_ref.at[slot]))
        return cps

    def start(seq, slot):
        for cp in page_copies(seq, slot):
            cp.start()

    def wait(seq, slot):
        for cp in page_copies(seq, slot):
            cp.wait()

    @pl.when(step == 0)
    def _():
        start(first, 0)

    def sample_seq(r):
        slot = r % SAMPLE_SLOTS
        if r + 1 < n_seq:
            start(first + r + 1, (r + 1) % SAMPLE_SLOTS)
        else:
            @pl.when(step + 1 < n_steps)
            def _():
                start(first + n_seq, 0)
        wait(first + r, slot)
        _attn_sample_one(r, slot, qs_ref, kn_ref, vn_ref, bs_ref, b31s_ref, lam_ref, gh_ref, os_ref,
                         kraw_ref, vraw_ref, kbf_ref, vbf_ref, n_pages=n_pages, page=page)

    def after_pair(n, total):
        for r in range(n_seq):
            if n == (r + 1) * total // (n_seq + 1):
                sample_seq(r)

    _attn_prompt_head(q_ref, kb_ref, vb_ref, bp_ref, lam_ref, gh_ref, o_ref,
                      w_ref, vt_ref, m_ref, acc_ref, after_pair)


def _attention(q, kb, vb, bias_p, lam, g_head, page_table, q_s, k_s, v_s, cache_k, cache_v, bias_s, b31s):
    nb, t, _ = q.shape
    nbs, tnew, _ = q_s.shape
    tq = ATTN_TQ
    nt = t // tq
    hw = V_HEAD_DIM
    n_steps = nb * N_HEADS
    n_seq = nbs // n_steps
    assert n_seq * n_steps == nbs and n_seq % SAMPLE_SLOTS == 0
    n_pages = page_table.shape[1]
    n_phys, page = cache_k.shape[0], cache_k.shape[1]
    ck = jnp.transpose(cache_k, (0, 2, 3, 4, 1)).reshape(n_phys, QK_WIDTH, page)
    cv = cache_v.reshape(n_phys, page * N_HEADS, V_HEAD_DIM)

    head = pl.BlockSpec((1, t, hw), lambda b, h, pt: (b, 0, h))
    seqs = lambda w: pl.BlockSpec((n_seq, tnew, w), lambda b, h, pt: (b * N_HEADS + h, 0, 0))
    const = lambda shape: pl.BlockSpec(shape, lambda b, h, pt: (0,) * len(shape))
    hbm = pl.BlockSpec(memory_space=pl.ANY)
    total = n_pages * page + LANES
    return pl.pallas_call(
        functools.partial(_attn_kernel, n_pages=n_pages, page=page),
        out_shape=(jax.ShapeDtypeStruct((nb, t, ATTN_WIDTH), BF16),
                   jax.ShapeDtypeStruct((nbs, tnew, ATTN_WIDTH), BF16)),
        grid_spec=pltpu.PrefetchScalarGridSpec(
            num_scalar_prefetch=1, grid=(nb, N_HEADS),
            in_specs=[head, head, head,
                      pl.BlockSpec((1, 2, ATTN_TK, ATTN_TQ), lambda b, h, pt: (h, 0, 0, 0)),
                      const((8, LANES)), const((1, hw)),
                      seqs(QK_WIDTH), seqs(QK_WIDTH), seqs(ATTN_WIDTH),
                      const(bias_s.shape), const(b31s.shape), hbm, hbm],
            out_specs=(head, seqs(ATTN_WIDTH)),
            scratch_shapes=[pltpu.VMEM((nt, 2 * HEAD_DIM, 2 * tq), BF16),
                            pltpu.VMEM((nt, hw + SUM_ROWS, ATTN_TK), BF16),
                            pltpu.VMEM((nt, 1, 2 * tq), F32),
                            pltpu.VMEM((nt, hw + SUM_ROWS, 2 * tq), F32),
                            pltpu.VMEM((SAMPLE_SLOTS, n_pages, QK_WIDTH, page), F32),
                            pltpu.VMEM((SAMPLE_SLOTS, n_pages, page * N_HEADS, V_HEAD_DIM), F32),
                            pltpu.VMEM((QK_WIDTH, n_pages * page), BF16),
                            pltpu.VMEM((total, ATTN_WIDTH), BF16),
                            pltpu.SemaphoreType.DMA((SAMPLE_SLOTS,))]),
        compiler_params=pltpu.CompilerParams(
            dimension_semantics=("arbitrary", "arbitrary"), vmem_limit_bytes=VMEM_LIMIT),
        name="attention",
    )(page_table, q, kb, vb, bias_p, lam, g_head.reshape(1, hw), q_s, k_s, v_s, bias_s, b31s, ck, cv)


def _attn_sample_one(row, slot, q_ref, kn_ref, vn_ref, bs_ref, b31s_ref, lam_ref, gh_ref, o_ref,
                     kraw_ref, vraw_ref, kbf_ref, vbf_ref, *, n_pages, page):
    past = n_pages * page
    tnew = q_ref.shape[1]
    n_maps = 2 * N_HEADS

    q = q_ref[row].astype(F32)
    qt = jnp.concatenate([q] * n_maps, axis=0)
    rg = lax.broadcasted_iota(jnp.int32, qt.shape, 0) // tnew
    cg = lax.broadcasted_iota(jnp.int32, qt.shape, 1) // HEAD_DIM
    qbd = jnp.where(rg == cg, qt, 0.0).astype(BF16)

    for j in range(n_pages):
        kbf_ref[:, j * page:(j + 1) * page] = kraw_ref[slot, j].astype(BF16)
        for h in range(N_HEADS):
            vbf_ref[j * page:(j + 1) * page, h * V_HEAD_DIM:(h + 1) * V_HEAD_DIM] = (
                vraw_ref[slot, j, pl.ds(h, page, stride=N_HEADS), :].astype(BF16))
    pad = jnp.zeros((LANES - tnew, QK_WIDTH), F32)
    k_new = jnp.concatenate([kn_ref[row], pad], axis=0).astype(BF16)
    vbf_ref[past:, :] = jnp.concatenate([vn_ref[row], pad], axis=0).astype(BF16)

    s_past = jnp.dot(qbd, kbf_ref[...], preferred_element_type=F32)
    s_new = lax.dot_general(qbd, k_new, (((1,), (1,)), ((), ())), preferred_element_type=F32)
    near = past - page
    s = jnp.concatenate([s_past[:, :near] + b31s_ref[:, 0:1],
                         jnp.concatenate([s_past[:, near:], s_new], axis=1) + bs_ref[...]], axis=1)
    mx = jnp.max(s, axis=-1, keepdims=True)
    p = jnp.exp(s - mx)
    pn = p / jnp.sum(p, axis=-1, keepdims=True)
    lam = lam_ref[0:1, 0:1]
    a = jnp.concatenate(
        [pn[2 * h * tnew:(2 * h + 1) * tnew] - lam * pn[(2 * h + 1) * tnew:(2 * h + 2) * tnew]
         for h in range(N_HEADS)], axis=0).astype(BF16)
    r = jnp.dot(a, vbf_ref[...], preferred_element_type=F32)
    outs = []
    for h in range(N_HEADS):
        o = r[h * tnew:(h + 1) * tnew, h * V_HEAD_DIM:(h + 1) * V_HEAD_DIM]
        outs.append(_rms(o, gh_ref[...]) * (1.0 - LAM_INIT))
    o_ref[row] = jnp.concatenate(outs, axis=1).astype(BF16)


def _ffn_kernel(*refs, carry, ns, bt, tt):
    if carry:
        (x_ref, yp_ref, at_ref, mod_ref, gpm_ref, gpf_ref, gpo_ref, w_out_ref, w_up_ref, cw_ref, cb_ref,
         w_down_ref, y_ref, cs_ref, *buf_refs, carry_ref) = refs
    else:
        (x_ref, yp_ref, at_ref, mod_ref, gpm_ref, gpf_ref, gpo_ref, w_out_ref, w_up_ref, cw_ref, cb_ref,
         w_down_ref, prefix_ref, y_ref, cs_ref, *buf_refs) = refs
    m = bt * tt
    n_chunks = D_FF // FF_CHUNK
    streams = range(ns)

    def rows(s):
        return (slice(None), slice(s * tt, (s + 1) * tt)) if carry else (slice(s * bt, (s + 1) * bt),)

    def mod_row(s, i):
        return mod_ref[:, i:i + 1, :] if carry else mod_ref[s * bt:(s + 1) * bt, i:i + 1, :]

    def buf(s, slot):
        return buf_refs[4 * s + slot]

    if carry:
        @pl.when(pl.program_id(1) == 0)
        def _():
            carry_ref[...] = jnp.zeros_like(carry_ref)
    halo_rows = CONV_HALO if carry else prefix_ref.shape[1]

    x1, h2 = {}, {}
    for s in streams:
        mix_in = jnp.concatenate([yp_ref[rows(s)], at_ref[rows(s)]], axis=-1).reshape(m, D_MODEL)
        mix = jnp.dot(mix_in, w_out_ref[...], preferred_element_type=F32).reshape(bt, tt, D_MODEL)
        x1[s] = x_ref[rows(s)] + mod_row(s, 2) * _rms(mix, gpm_ref[...])
        h = _rms(x1[s], gpf_ref[...]) * (1.0 + mod_row(s, 4)) + mod_row(s, 3)
        h2[s] = h.reshape(m, D_MODEL).astype(BF16)

    def up_part(s, slot, off):
        sl = slice(off, off + FF_CHUNK)
        u3 = jnp.dot(h2[s], w_up_ref[:, sl], preferred_element_type=F32).reshape(bt, tt, FF_CHUNK)
        if not carry:
            halo = prefix_ref[s * bt:(s + 1) * bt, :, sl]
        elif s == 0:
            halo = carry_ref[:, :, sl]
        else:
            halo = buf(s - 1, slot)[:, tt:tt + CONV_HALO, :]
        buf(s, slot)[:, CONV_HALO - halo_rows:CONV_HALO, :] = halo
        buf(s, slot)[:, CONV_HALO:, :] = u3
        if not carry:
            cs_ref[s * bt:(s + 1) * bt, :, sl] = u3[:, tt - halo_rows:, :]
        elif s == ns - 1:
            cs_ref[:, :, sl] = u3[:, tt - halo_rows:, :]

    def conv_part(s, slot, off):
        sl = slice(off, off + FF_CHUNK)
        b = buf(s, slot)
        y = (cb_ref[:, :, sl]
             + cw_ref[0:1, :, sl] * b[:, CONV_HALO - 2:CONV_HALO - 2 + tt, :]
             + cw_ref[1:2, :, sl] * b[:, CONV_HALO - 1:CONV_HALO - 1 + tt, :]
             + cw_ref[2:3, :, sl] * b[:, CONV_HALO:, :])
        return y.reshape(m, FF_CHUNK)

    def up_chunk(s, c):
        up_part(s, 2 * (c % 2), c * FF_CHUNK)
        up_part(s, 2 * (c % 2) + 1, D_FF + c * FF_CHUNK)

    f = {s: jnp.zeros((m, D_MODEL), F32) for s in streams}
    for s in streams:
        up_chunk(s, 0)
    for c in range(n_chunks):
        if c + 1 < n_chunks:
            for s in streams:
                up_chunk(s, c + 1)
        for s in streams:
            gate = conv_part(s, 2 * (c % 2), c * FF_CHUNK)
            val = conv_part(s, 2 * (c % 2) + 1, D_FF + c * FF_CHUNK)
            act = (_gelu_tanh(gate) * val).astype(BF16)
            f[s] = f[s] + jnp.dot(act, w_down_ref[c * FF_CHUNK:(c + 1) * FF_CHUNK, :],
                                  preferred_element_type=F32)
    if carry:
        carry_ref[...] = cs_ref[...]
    for s in streams:
        y_ref[rows(s)] = x1[s] + mod_row(s, 5) * _rms(f[s].reshape(bt, tt, D_MODEL), gpo_ref[...])


def _ffn(x, yp, attn, mod, g_post_mix, g_pre_ffn, g_post_ffn, w_out, w_up, conv_w, conv_b, w_down, prefix):
    nb, t, _ = x.shape
    carry = prefix is None
    ns = FFN_STREAMS
    if carry:
        bt, tt = 1, ROW_TILE
        grid = (nb, t // (ns * tt))
        tile = lambda w: pl.BlockSpec((1, ns * tt, w), lambda b, i: (b, i, 0))
        per_b = lambda r, w: pl.BlockSpec((1, r, w), lambda b, i: (b, 0, 0))
        sem = ("arbitrary", "arbitrary")
    else:
        bt, tt = ROW_TILE // t, t
        grid = (nb // (ns * bt),)
        tile = lambda w: pl.BlockSpec((ns * bt, tt, w), lambda i: (i, 0, 0))
        per_b = lambda r, w: pl.BlockSpec((ns * bt, r, w), lambda i: (i, 0, 0))
        sem = ("arbitrary",)
    gspec = _const_spec((1, 1, D_MODEL))
    in_specs = [tile(D_MODEL), tile(POOL_WIDTH), tile(ATTN_WIDTH), per_b(N_MOD, D_MODEL),
                gspec, gspec, gspec,
                _const_spec((D_MODEL, D_MODEL)), _const_spec((D_MODEL, 2 * D_FF)),
                _const_spec((CONV_WIDTH, 1, 2 * D_FF)), _const_spec((1, 1, 2 * D_FF)),
                _const_spec((D_FF, D_MODEL))]
    args = [x, yp, attn, mod, g_post_mix.reshape(1, 1, -1), g_pre_ffn.reshape(1, 1, -1),
            g_post_ffn.reshape(1, 1, -1), w_out, w_up, conv_w.reshape(CONV_WIDTH, 1, -1),
            conv_b.reshape(1, 1, -1), w_down]
    scratch = [pltpu.VMEM((bt, CONV_HALO + tt, FF_CHUNK), F32) for _ in range(4 * ns)]
    if carry:
        halo_rows = CONV_HALO
        scratch.append(pltpu.VMEM((1, halo_rows, 2 * D_FF), F32))
    else:
        halo_rows = prefix.shape[1]
        in_specs.append(per_b(halo_rows, 2 * D_FF))
        args.append(prefix)
    return pl.pallas_call(
        functools.partial(_ffn_kernel, carry=carry, ns=ns, bt=bt, tt=tt),
        out_shape=(jax.ShapeDtypeStruct((nb, t, D_MODEL), F32),
                   jax.ShapeDtypeStruct((nb, halo_rows, 2 * D_FF), F32)),
        grid=grid, in_specs=in_specs,
        out_specs=(tile(D_MODEL), per_b(halo_rows, 2 * D_FF)),
        scratch_shapes=scratch,
        compiler_params=pltpu.CompilerParams(dimension_semantics=sem, vmem_limit_bytes=VMEM_LIMIT),
        name="ffn_prompt" if carry else "ffn_sample",
    )(*args)


def kernel(x_prompt, x_sample, c_prompt, c_sample, cache_k, cache_v, page_table, state_pool, state_conv,
           w_ada, b_ada, g_pre_mix, g_post_mix, g_pre_ffn, g_post_ffn, w_in, w_out, w_pool, pool_scale,
           lam_q1, lam_k1, lam_q2, lam_k2, g_head, rel_bias, w_up, conv_w, conv_b, w_down):
    nbp, seq, _ = x_prompt.shape
    nbs, dec_seq, _ = x_sample.shape
    past_len = page_table.shape[1] * cache_k.shape[2]
    l = 0

    lam, bias_p, bias_s, b31s = _setup(rel_bias, lam_q1[l:l + 1], lam_k1[l:l + 1],
                                             lam_q2[l:l + 1], lam_k2[l:l + 1])
    mod = _modulation(jnp.concatenate([c_prompt, c_sample], axis=0), w_ada[l], b_ada[l])
    mod = mod.reshape(nbp + nbs, N_MOD, D_MODEL)
    mod_p, mod_s = mod[:nbp], mod[nbp:]

    w_in_b = w_in[l].astype(BF16)
    w_kt_b = w_in[l][:, POOL_WIDTH + QK_WIDTH:POOL_WIDTH + 2 * QK_WIDTH].T.astype(BF16)
    w_pool_b = w_pool[l].astype(BF16)
    w_out_b = w_out[l].astype(BF16)
    w_up_b = w_up[l].astype(BF16)
    w_down_b = w_down[l].astype(BF16)

    pool_prefix = jnp.pad(state_pool[l], ((0, 0), (POOL_HALO - POOL_STATE, 0), (0, 0)))

    q_p, kt_p, v_p, kb_p, vb_p, yp_p, pstate_p = _inproj(
        x_prompt, mod_p, g_pre_mix[l], w_in_b, w_kt_b, w_pool_b, pool_scale[l], None, pos0=0)
    q_s, k_s, v_s, yp_s, pstate_s = _inproj(
        x_sample, mod_s, g_pre_mix[l], w_in_b, w_kt_b, w_pool_b, pool_scale[l], pool_prefix, pos0=past_len)
    attn_p, attn_s = _attention(q_p, kb_p, vb_p, bias_p, lam, g_head[l], page_table, q_s, k_s, v_s,
                                cache_k[l], cache_v[l], bias_s, b31s)
    k_p = jnp.transpose(kt_p.reshape(nbp, N_HEADS, 2, HEAD_DIM, seq), (0, 4, 1, 2, 3))

    outs = []
    for (x, md, yp, attn, k, v, pstate, cprefix) in (
            (x_prompt, mod_p, yp_p, attn_p, k_p, v_p, pstate_p, None),
            (x_sample, mod_s, yp_s, attn_s, k_s, v_s, pstate_s, state_conv[l])):
        nb, t = x.shape[0], x.shape[1]
        y, cstate = _ffn(x, yp, attn, md, g_post_mix[l], g_pre_ffn[l], g_post_ffn[l],
                         w_out_b, w_up_b, conv_w[l], conv_b[l], w_down_b, cprefix)
        outs.append((y,
                     k.reshape(1, nb, t, N_HEADS, 2, HEAD_DIM),
                     v.reshape(1, nb, t, N_HEADS, V_HEAD_DIM),
                     pstate[None, :, POOL_HALO - POOL_STATE:, :],
                     cstate[None, :, cstate.shape[1] - (CONV_WIDTH - 1):, :]))
    (yp_, kp, vp, pp, cp), (ys_, ks, vs, ps, cs) = outs
    return (yp_, ys_, kp, vp, pp, cp, ks, vs, ps, cs)
```

```python
import functools
import math

import jax
import jax.numpy as jnp
from jax import lax
from jax.experimental import pallas as pl
from jax.experimental.pallas import tpu as pltpu

F32 = jnp.float32
BF16 = jnp.bfloat16

D_MODEL = 1024
POOL_WIDTH = 512
POOL_WINDOWS = (2, 4, 8, 16)
POOL_GROUP_DIM = 128
POOL_STATE = 15
POOL_HALO = 16
N_HEADS = 4
HEAD_DIM = 64
V_HEAD_DIM = 128
QK_WIDTH = 512
ATTN_WIDTH = 512
PROJ_WIDTH = 2048
D_FF = 2816
CONV_WIDTH = 3
CONV_HALO = 8
N_BUCKETS = 32
MAX_EXACT = N_BUCKETS // 2
MAX_DISTANCE = 128
N_MOD = 6
EPS = 1e-6
LAM_INIT = 0.8 - 0.6 * math.exp(-0.3 * 0)
NEG_BIG = -1e30
LOG2E = math.log2(math.e)
SUM_ROWS = 16

LANES = 128
MXU_TILE = 256
VMEM_LIMIT = 56 * 1024 * 1024

ATTN_TQ = 256
ATTN_TK = 256
ATTN_CHAIN_GAP = 4
ATTN_LOOKAHEAD = 2
ROW_TILE = 256
FF_CHUNK = 256
SAMPLE_SLOTS = 2
FFN_STREAMS = 2


def _const_spec(shape):
    nd = len(shape)
    return pl.BlockSpec(shape, lambda *_: (0,) * nd, pipeline_mode=pl.Buffered(1))


def _rms(x, g):
    return x * lax.rsqrt(jnp.mean(x * x, axis=-1, keepdims=True) + EPS) * g


def _gelu_tanh(x):
    c0 = -2.0 * math.sqrt(2.0 / math.pi) * LOG2E
    return x / (1.0 + jnp.exp2(x * (c0 + (c0 * 0.044715) * (x * x))))


def _bucket(dist):
    d = jnp.maximum(dist, 1).astype(F32)
    large = MAX_EXACT + jnp.floor(jnp.log(d / MAX_EXACT) / math.log(MAX_DISTANCE / MAX_EXACT)
                                  * (N_BUCKETS - MAX_EXACT)).astype(jnp.int32)
    large = jnp.minimum(large, N_BUCKETS - 1)
    return jnp.where(dist < MAX_EXACT, dist, large)


def _setup_kernel(rb_ref, lq1_ref, lk1_ref, lq2_ref, lk2_ref,
                  lam_ref, bp_ref, bs_ref, b31s_ref):
    lam = (jnp.exp(jnp.sum(lq1_ref[...] * lk1_ref[...], axis=-1, keepdims=True))
           - jnp.exp(jnp.sum(lq2_ref[...] * lk2_ref[...], axis=-1, keepdims=True)) + LAM_INIT)
    lam_ref[...] = jnp.broadcast_to(lam, lam_ref.shape)

    def lookup(bucket, h):
        out = jnp.full(bucket.shape, rb_ref[N_BUCKETS - 1, h], F32)
        for b in range(N_BUCKETS - 1):
            out = jnp.where(bucket == b, rb_ref[b, h], out)
        return out

    kk = lax.broadcasted_iota(jnp.int32, (ATTN_TK, ATTN_TQ), 0)
    r = lax.broadcasted_iota(jnp.int32, (ATTN_TK, ATTN_TQ), 1)
    for typ in range(2):
        dist = typ * ATTN_TK + r - kk
        bucket = _bucket(jnp.maximum(dist, 0))
        for h in range(N_HEADS):
            shifted = (lookup(bucket, h) - rb_ref[N_BUCKETS - 1, h]) * LOG2E
            bp_ref[h, typ] = jnp.where(dist >= 0, shifted, NEG_BIG)

    rows = 2 * 8
    t = lax.broadcasted_iota(jnp.int32, (rows, 2 * LANES), 0) % 8
    col = lax.broadcasted_iota(jnp.int32, (rows, 2 * LANES), 1)
    dist = jnp.where(col < LANES, LANES + t - col, t - (col - LANES))
    bucket = _bucket(jnp.maximum(dist, 0))
    for h in range(N_HEADS):
        bs_ref[h * rows:(h + 1) * rows, :] = jnp.where(dist >= 0, lookup(bucket, h), NEG_BIG)
        b31s_ref[h * rows:(h + 1) * rows, :] = jnp.full((rows, LANES), rb_ref[N_BUCKETS - 1, h], F32)


def _setup(rel_bias, lq1, lk1, lq2, lk2):
    vspec = pl.BlockSpec(memory_space=pltpu.VMEM)
    return pl.pallas_call(
        _setup_kernel,
        out_shape=(jax.ShapeDtypeStruct((8, LANES), F32),
                   jax.ShapeDtypeStruct((N_HEADS, 2, ATTN_TK, ATTN_TQ), F32),
                   jax.ShapeDtypeStruct((N_HEADS * 16, 2 * LANES), F32),
                   jax.ShapeDtypeStruct((N_HEADS * 16, LANES), F32)),
        in_specs=[pl.BlockSpec(memory_space=pltpu.SMEM), vspec, vspec, vspec, vspec],
        out_specs=(vspec, vspec, vspec, vspec),
        name="setup",
    )(rel_bias, lq1, lk1, lq2, lk2)


def _mod_kernel(c_ref, w_ref, b_ref, o_ref):
    c = c_ref[...]
    s = (c * jax.nn.sigmoid(c)).astype(BF16)
    o_ref[...] = jnp.dot(s, w_ref[...].astype(BF16), preferred_element_type=F32) + b_ref[...]


def _modulation(c_all, w_ada, b_ada):
    n = c_all.shape[0]
    tn = 1024
    return pl.pallas_call(
        _mod_kernel,
        out_shape=jax.ShapeDtypeStruct((n, N_MOD * D_MODEL), F32),
        grid=(N_MOD * D_MODEL // tn,),
        in_specs=[pl.BlockSpec((n, D_MODEL), lambda j: (0, 0)),
                  pl.BlockSpec((D_MODEL, tn), lambda j: (0, j)),
                  pl.BlockSpec((1, tn), lambda j: (0, j))],
        out_specs=pl.BlockSpec((n, tn), lambda j: (0, j)),
        compiler_params=pltpu.CompilerParams(dimension_semantics=("arbitrary",),
                                             vmem_limit_bytes=VMEM_LIMIT),
        name="modulation",
    )(c_all, w_ada, b_ada.reshape(1, -1))


def _inproj_kernel(*refs, carry, pos0, bt, tt, share=None):
    sample_seq = None
    if carry and share is not None:
        pt_ref, refs = refs[0], refs[1:]
        share_in, os_ref = refs[7:7 + N_SHARE_IN], refs[7 + N_SHARE_IN + 7]
        share_scratch = refs[-N_SHARE_SCRATCH:]
        refs = refs[:7] + refs[7 + N_SHARE_IN:7 + N_SHARE_IN + 7] + refs[7 + N_SHARE_IN + 8:-N_SHARE_SCRATCH]
        step = pl.program_id(0) * pl.num_programs(1) + pl.program_id(1)
        n_steps = pl.num_programs(0) * pl.num_programs(1)
        sample_seq = _sample_share(step, n_steps, pt_ref, *share_in, os_ref, *share_scratch, **share)
    if carry:
        (x_ref, mod_ref, g_ref, w_in_ref, w_pool_ref, ps_ref, wkt_ref,
         q_ref, k_ref, v_ref, kb_ref, vb_ref, yp_ref, st_ref, ext_ref, carry_ref) = refs
    else:
        (x_ref, mod_ref, g_ref, w_in_ref, w_pool_ref, ps_ref, prefix_ref,
         q_ref, k_ref, v_ref, yp_ref, st_ref, ext_ref) = refs
    m = bt * tt
    x = x_ref[...]
    h = _rms(x, g_ref[...]) * (1.0 + mod_ref[:, 1:2, :]) + mod_ref[:, 0:1, :]
    h2 = h.reshape(m, D_MODEL).astype(BF16)

    def proj(off, width):
        return jnp.dot(h2, w_in_ref[:, off:off + width], preferred_element_type=F32)

    u3 = proj(0, POOL_WIDTH).reshape(bt, tt, POOL_WIDTH)
    if carry:
        @pl.when(pl.program_id(1) == 0)
        def _():
            carry_ref[...] = jnp.zeros_like(carry_ref)
        ext_ref[:, 0:POOL_HALO, :] = carry_ref[...]
    else:
        ext_ref[:, 0:POOL_HALO, :] = prefix_ref[...]
    ext_ref[:, POOL_HALO:, :] = u3
    if carry:
        carry_ref[...] = u3[:, tt - POOL_HALO:, :]
        t_base = pl.program_id(1) * tt
    else:
        t_base = 0
    st_ref[...] = ext_ref[:, tt:tt + POOL_HALO, :]

    zq = proj(POOL_WIDTH, QK_WIDTH)
    q_scale = HEAD_DIM ** -0.5 * (LOG2E if carry else 1.0)
    q_ref[...] = (zq * q_scale).astype(BF16).reshape(bt, tt, QK_WIDTH)
    zv = proj(POOL_WIDTH + 2 * QK_WIDTH, ATTN_WIDTH)
    if carry:
        vb_ref[...] = zv.astype(BF16).reshape(bt, tt, ATTN_WIDTH)
        for hd in range(N_HEADS):
            v_ref[0, pl.ds(hd, tt, stride=N_HEADS), :] = zv[:, hd * V_HEAD_DIM:(hd + 1) * V_HEAD_DIM]
    else:
        v_ref[...] = zv.reshape(bt, tt, ATTN_WIDTH)

    pos = pos0 + t_base + lax.broadcasted_iota(jnp.int32, (bt, tt, POOL_GROUP_DIM), 1)
    for g, w in enumerate(POOL_WINDOWS):
        sl = slice(g * POOL_GROUP_DIM, (g + 1) * POOL_GROUP_DIM)
        tok = ext_ref[:, POOL_HALO:, sl]
        acc = tok
        for j in range(1, w):
            acc = acc + ext_ref[:, POOL_HALO - j:POOL_HALO - j + tt, sl]
        cnt = jnp.minimum(pos + 1, w).astype(F32)
        d = (acc / cnt - tok).reshape(m, POOL_GROUP_DIM).astype(BF16)
        y = jnp.dot(d, w_pool_ref[g], preferred_element_type=F32) * ps_ref[:, sl]
        yp_ref[:, :, sl] = y.astype(BF16).reshape(bt, tt, POOL_GROUP_DIM)

    if carry:
        zkt = lax.dot_general(wkt_ref[...], h2, (((1,), (1,)), ((), ())), preferred_element_type=F32)
        k_ref[0] = zkt
        kb_ref[...] = zkt.T.astype(BF16).reshape(bt, tt, QK_WIDTH)
    else:
        k_ref[...] = proj(POOL_WIDTH + QK_WIDTH, QK_WIDTH).reshape(bt, tt, QK_WIDTH)

    if sample_seq is not None:
        for r in range(os_ref.shape[0]):
            sample_seq(r)


def _inproj(x, mod, g_pre, w_in, w_kt, w_pool, pool_scale, prefix, *, pos0, page_table=None, sample=None,
            seq0=0, n_seq=0):
    nb, t, _ = x.shape
    carry = prefix is None
    if carry:
        bt, tt = 1, ROW_TILE
        grid = (nb, t // tt)
        tile = lambda w: pl.BlockSpec((bt, tt, w), lambda b, i, *_: (b, i, 0))
        per_b = lambda r, w: pl.BlockSpec((bt, r, w), lambda b, i, *_: (b, 0, 0))
        sem = ("arbitrary", "arbitrary")
    else:
        bt, tt = ROW_TILE // t, t
        grid = (nb // bt,)
        tile = lambda w: pl.BlockSpec((bt, tt, w), lambda i: (i, 0, 0))
        per_b = lambda r, w: pl.BlockSpec((bt, r, w), lambda i: (i, 0, 0))
        sem = ("arbitrary",)
    in_specs = [tile(D_MODEL), per_b(N_MOD, D_MODEL), _const_spec((1, 1, D_MODEL)),
                _const_spec((D_MODEL, PROJ_WIDTH)),
                _const_spec((len(POOL_WINDOWS), POOL_GROUP_DIM, POOL_GROUP_DIM)),
                _const_spec((1, POOL_WIDTH))]
    args = [x, mod, g_pre.reshape(1, 1, D_MODEL), w_in, w_pool, pool_scale.reshape(1, POOL_WIDTH)]
    scratch = [pltpu.VMEM((bt, POOL_HALO + tt, POOL_WIDTH), F32)]
    st_shape = jax.ShapeDtypeStruct((nb, POOL_HALO, POOL_WIDTH), F32)
    act = lambda w, dt: jax.ShapeDtypeStruct((nb, t, w), dt)
    if carry:
        scratch.append(pltpu.VMEM((1, POOL_HALO, POOL_WIDTH), F32))
        in_specs.append(_const_spec((QK_WIDTH, D_MODEL)))
        args.append(w_kt)
        out_shape = (act(QK_WIDTH, BF16),
                     jax.ShapeDtypeStruct((nb, QK_WIDTH, t), F32),
                     jax.ShapeDtypeStruct((nb, t * N_HEADS, V_HEAD_DIM), F32), act(QK_WIDTH, BF16),
                     act(ATTN_WIDTH, BF16), act(POOL_WIDTH, BF16), st_shape)
        out_specs = (tile(QK_WIDTH),
                     pl.BlockSpec((1, QK_WIDTH, tt), lambda b, i, *_: (b, 0, i)),
                     pl.BlockSpec((1, tt * N_HEADS, V_HEAD_DIM), lambda b, i, *_: (b, i, 0)), tile(QK_WIDTH),
                     tile(ATTN_WIDTH), tile(POOL_WIDTH), per_b(POOL_HALO, POOL_WIDTH))
        if sample is not None:
            steps_per_b = t // tt
            sh_in, sh_args, sh_out, sh_out_spec, sh_scratch, sh_static = _share_plumbing(
                sample, lambda b, i: b * steps_per_b + i, seq0, n_seq, nb * steps_per_b)
            return pl.pallas_call(
                functools.partial(_inproj_kernel, carry=carry, pos0=pos0, bt=bt, tt=tt, share=sh_static),
                out_shape=out_shape + (sh_out,),
                grid_spec=pltpu.PrefetchScalarGridSpec(
                    num_scalar_prefetch=1, grid=grid, in_specs=in_specs + sh_in,
                    out_specs=out_specs + (sh_out_spec,), scratch_shapes=scratch + sh_scratch),
                compiler_params=pltpu.CompilerParams(dimension_semantics=sem, vmem_limit_bytes=VMEM_LIMIT),
                name="inproj_prompt",
            )(page_table, *args, *sh_args)
    else:
        in_specs.append(per_b(POOL_HALO, POOL_WIDTH))
        args.append(prefix)
        out_shape = (act(QK_WIDTH, BF16), act(QK_WIDTH, F32), act(ATTN_WIDTH, F32),
                     act(POOL_WIDTH, BF16), st_shape)
        out_specs = (tile(QK_WIDTH), tile(QK_WIDTH), tile(ATTN_WIDTH), tile(POOL_WIDTH),
                     per_b(POOL_HALO, POOL_WIDTH))
    return pl.pallas_call(
        functools.partial(_inproj_kernel, carry=carry, pos0=pos0, bt=bt, tt=tt),
        out_shape=out_shape, grid=grid, in_specs=in_specs, out_specs=out_specs,
        scratch_shapes=scratch,
        compiler_params=pltpu.CompilerParams(dimension_semantics=sem, vmem_limit_bytes=VMEM_LIMIT),
        name="inproj_prompt" if carry else "inproj_sample",
    )(*args)


def _attn_unit_order(nt):
    left = {i: i + 1 for i in range(nt)}
    last = {i: -ATTN_CHAIN_GAP for i in range(nt)}
    order = []
    while any(left.values()):
        n = len(order)
        cands = [i for i in range(nt) if left[i]]
        spaced = [i for i in cands if n - last[i] >= ATTN_CHAIN_GAP]
        i = max(spaced, key=lambda t: (left[t], t)) if spaced else max(cands, key=lambda t: n - last[t])
        order.append((i, i + 1 - left[i]))
        left[i] -= 1
        last[i] = n
    return order


def _attn_prompt_head(q_ref, kb_ref, vb_ref, bp_ref, lam_ref, gh_ref, o_ref,
                      w_ref, vt_ref, m_ref, acc_ref, after_pair):
    tq, tk = ATTN_TQ, ATTN_TK
    nt = q_ref.shape[1] // tq
    zero = jnp.zeros((HEAD_DIM, tq), BF16)
    for i in range(nt):
        qt = q_ref[0, i * tq:(i + 1) * tq, :].astype(F32).T.astype(BF16)
        w_ref[i] = jnp.concatenate([jnp.concatenate([qt[:HEAD_DIM], zero], axis=1),
                                    jnp.concatenate([zero, qt[HEAD_DIM:]], axis=1)], axis=0)
    lam = lam_ref[0:1, 0:1]

    def scores(i, j):
        return jnp.dot(kb_ref[0, j * tk:(j + 1) * tk, :], w_ref[i], preferred_element_type=F32)

    for j in range(nt):
        vt_ref[j] = jnp.concatenate([vb_ref[0, j * tk:(j + 1) * tk, :].astype(F32).T,
                                     jnp.ones((SUM_ROWS, tk), F32)], axis=0).astype(BF16)

    units = _attn_unit_order(nt)
    pending = [scores(*u) for u in units[:ATTN_LOOKAHEAD]]
    for n, (i, j) in enumerate(units):
        st = pending.pop(0)
        if n + ATTN_LOOKAHEAD < len(units):
            pending.append(scores(*units[n + ATTN_LOOKAHEAD]))
        vt = vt_ref[j]
        if i - j < 2:
            b = bp_ref[0, i - j]
            st = jnp.concatenate([b, b], axis=1) + st
        mx = jnp.max(st, axis=0, keepdims=True)
        if j == 0:
            p = jnp.exp2(st - mx)
            acc_ref[i] = jnp.dot(vt, p.astype(BF16), preferred_element_type=F32)
            m_ref[i] = mx
        else:
            m_old = m_ref[i]
            m_new = jnp.maximum(m_old, mx)
            alpha = jnp.exp2(m_old - m_new)
            p = jnp.exp2(st - m_new)
            acc_ref[i] = alpha * acc_ref[i] + jnp.dot(vt, p.astype(BF16), preferred_element_type=F32)
            m_ref[i] = m_new
        if i == j:
            acc = acc_ref[j]
            on = acc[:V_HEAD_DIM] / acc[V_HEAD_DIM:V_HEAD_DIM + 1]
            ot = on[:, :tq] - lam * on[:, tq:]
            ot = ot * lax.rsqrt(jnp.mean(ot * ot, axis=0, keepdims=True) + EPS)
            o_ref[0, j * tq:(j + 1) * tq, :] = (ot.T * gh_ref[...] * (1.0 - LAM_INIT)).astype(BF16)
        after_pair(n, len(units))


N_SHARE_IN, N_SHARE_SCRATCH = 9, 5


def _attn_kernel(pt_ref, q_ref, kb_ref, vb_ref, bp_ref, lam_ref, gh_ref, *rest, share):
    share_in = rest[:N_SHARE_IN]
    o_ref, os_ref, w_ref, vt_ref, m_ref, acc_ref = rest[N_SHARE_IN:N_SHARE_IN + 6]
    share_scratch = rest[N_SHARE_IN + 6:]
    step = pl.program_id(0) * pl.num_programs(1) + pl.program_id(1)
    n_steps = pl.num_programs(0) * pl.num_programs(1)
    sample_seq = _sample_share(step, n_steps, pt_ref, *share_in, os_ref, *share_scratch, **share)
    n_seq = os_ref.shape[0]

    def after_pair(n, total):
        for r in range(n_seq):
            if n == (r + 1) * total // (n_seq + 1):
                sample_seq(r)

    _attn_prompt_head(q_ref, kb_ref, vb_ref, bp_ref, lam_ref, gh_ref, o_ref,
                      w_ref, vt_ref, m_ref, acc_ref, after_pair)


def _share_plumbing(sample, step_index, seq0, n_seq, n_steps):
    n_pages, q_s, k_s, v_s, bias_s, b31s, lam, g_head, ck, cv = sample
    tnew = q_s.shape[1]
    page = ck.shape[2]
    past = n_pages * page
    assert seq0 % n_seq == 0
    seqs = lambda w: pl.BlockSpec((n_seq, tnew, w), lambda *g: (seq0 // n_seq + step_index(*g[:-1]), 0, 0))
    out_spec = pl.BlockSpec((n_seq, tnew, ATTN_WIDTH), lambda *g: (step_index(*g[:-1]), 0, 0))
    const = lambda a: pl.BlockSpec(a.shape, lambda *g: (0,) * a.ndim)
    hbm = pl.BlockSpec(memory_space=pl.ANY)
    gh = g_head.reshape(1, V_HEAD_DIM)
    in_specs = [seqs(QK_WIDTH), seqs(QK_WIDTH), seqs(ATTN_WIDTH), const(bias_s), const(b31s), const(lam),
                const(gh), hbm, hbm]
    args = [q_s, k_s, v_s, bias_s, b31s, lam, gh, ck, cv]
    out_shape = jax.ShapeDtypeStruct((n_steps * n_seq, tnew, ATTN_WIDTH), BF16)
    scratch = [pltpu.VMEM((SAMPLE_SLOTS, n_pages, QK_WIDTH, page), F32),
               pltpu.VMEM((SAMPLE_SLOTS, n_pages, page * N_HEADS, V_HEAD_DIM), F32),
               pltpu.VMEM((QK_WIDTH, past), BF16),
               pltpu.VMEM((past + LANES, ATTN_WIDTH), BF16),
               pltpu.SemaphoreType.DMA((SAMPLE_SLOTS,))]
    assert len(in_specs) == N_SHARE_IN and len(scratch) == N_SHARE_SCRATCH
    return in_specs, args, out_shape, out_spec, scratch, dict(n_pages=n_pages, page=page, seq0=seq0)


def _sample_share(step, n_steps, pt_ref, qs_ref, kn_ref, vn_ref, bs_ref, b31s_ref, lam_ref, gh_ref,
                  ck_hbm, cv_hbm, os_ref, kraw_ref, vraw_ref, kbf_ref, vbf_ref, sem_ref,
                  *, n_pages, page, seq0):
    n_seq = qs_ref.shape[0]
    first = seq0 + step * n_seq

    def page_copies(seq, slot):
        cps = []
        for j in range(n_pages):
            pg = pt_ref[seq, j]
            cps.append(pltpu.make_async_copy(ck_hbm.at[pg], kraw_ref.at[slot, j], sem_ref.at[slot]))
            cps.append(pltpu.make_async_copy(cv_hbm.at[pg], vraw_ref.at[slot, j], sem_ref.at[slot]))
        return cps

    def start(seq):
        for cp in page_copies(seq, seq & (SAMPLE_SLOTS - 1)):
            cp.start()

    def wait(seq):
        for cp in page_copies(seq, seq & (SAMPLE_SLOTS - 1)):
            cp.wait()

    @pl.when(step == 0)
    def _():
        start(first)

    def sample_seq(r):
        seq = first + r
        if r + 1 < n_seq:
            start(seq + 1)
        else:
            @pl.when(step + 1 < n_steps)
            def _():
                start(seq + 1)
        wait(seq)
        _attn_sample_one(r, seq & (SAMPLE_SLOTS - 1), qs_ref, kn_ref, vn_ref, bs_ref, b31s_ref, lam_ref, gh_ref,
                         os_ref, kraw_ref, vraw_ref, kbf_ref, vbf_ref, n_pages=n_pages, page=page)

    return sample_seq


def _attention(q, kb, vb, bias_p, lam, g_head, page_table, sample, *, seq0, n_seq):
    nb, t, _ = q.shape
    tq = ATTN_TQ
    nt = t // tq
    hw = V_HEAD_DIM
    n_steps = nb * N_HEADS
    sh_in, sh_args, sh_out, sh_out_spec, sh_scratch, sh_static = _share_plumbing(
        sample, lambda b, h: b * N_HEADS + h, seq0, n_seq, n_steps)
    head = pl.BlockSpec((1, t, hw), lambda b, h, pt: (b, 0, h))
    const = lambda shape: pl.BlockSpec(shape, lambda b, h, pt: (0,) * len(shape))
    return pl.pallas_call(
        functools.partial(_attn_kernel, share=sh_static),
        out_shape=(jax.ShapeDtypeStruct((nb, t, ATTN_WIDTH), BF16), sh_out),
        grid_spec=pltpu.PrefetchScalarGridSpec(
            num_scalar_prefetch=1, grid=(nb, N_HEADS),
            in_specs=[head, head, head,
                      pl.BlockSpec((1, 2, ATTN_TK, ATTN_TQ), lambda b, h, pt: (h, 0, 0, 0)),
                      const((8, LANES)), const((1, hw))] + sh_in,
            out_specs=(head, sh_out_spec),
            scratch_shapes=[pltpu.VMEM((nt, 2 * HEAD_DIM, 2 * tq), BF16),
                            pltpu.VMEM((nt, hw + SUM_ROWS, ATTN_TK), BF16),
                            pltpu.VMEM((nt, 1, 2 * tq), F32),
                            pltpu.VMEM((nt, hw + SUM_ROWS, 2 * tq), F32)] + sh_scratch),
        compiler_params=pltpu.CompilerParams(
            dimension_semantics=("arbitrary", "arbitrary"), vmem_limit_bytes=VMEM_LIMIT),
        name="attention",
    )(page_table, q, kb, vb, bias_p, lam, g_head.reshape(1, hw), *sh_args)


def _attn_sample_one(row, slot, q_ref, kn_ref, vn_ref, bs_ref, b31s_ref, lam_ref, gh_ref, o_ref,
                     kraw_ref, vraw_ref, kbf_ref, vbf_ref, *, n_pages, page):
    past = n_pages * page
    tnew = q_ref.shape[1]
    n_maps = 2 * N_HEADS

    q = q_ref[row].astype(F32)
    qt = jnp.concatenate([q] * n_maps, axis=0)
    rg = lax.broadcasted_iota(jnp.int32, qt.shape, 0) // tnew
    cg = lax.broadcasted_iota(jnp.int32, qt.shape, 1) // HEAD_DIM
    qbd = jnp.where(rg == cg, qt, 0.0).astype(BF16)

    for j in range(n_pages):
        kbf_ref[:, j * page:(j + 1) * page] = kraw_ref[slot, j].astype(BF16)
        for h in range(N_HEADS):
            vbf_ref[j * page:(j + 1) * page, h * V_HEAD_DIM:(h + 1) * V_HEAD_DIM] = (
                vraw_ref[slot, j, pl.ds(h, page, stride=N_HEADS), :].astype(BF16))
    pad = jnp.zeros((LANES - tnew, QK_WIDTH), F32)
    k_new = jnp.concatenate([kn_ref[row], pad], axis=0).astype(BF16)
    vbf_ref[past:, :] = jnp.concatenate([vn_ref[row], pad], axis=0).astype(BF16)

    s_past = jnp.dot(qbd, kbf_ref[...], preferred_element_type=F32)
    s_new = lax.dot_general(qbd, k_new, (((1,), (1,)), ((), ())), preferred_element_type=F32)
    near = past - page
    s = jnp.concatenate([s_past[:, :near] + b31s_ref[:, 0:1],
                         jnp.concatenate([s_past[:, near:], s_new], axis=1) + bs_ref[...]], axis=1)
    mx = jnp.max(s, axis=-1, keepdims=True)
    p = jnp.exp(s - mx)
    pn = p / jnp.sum(p, axis=-1, keepdims=True)
    lam = lam_ref[0:1, 0:1]
    a = jnp.concatenate(
        [pn[2 * h * tnew:(2 * h + 1) * tnew] - lam * pn[(2 * h + 1) * tnew:(2 * h + 2) * tnew]
         for h in range(N_HEADS)], axis=0).astype(BF16)
    r = jnp.dot(a, vbf_ref[...], preferred_element_type=F32)
    outs = []
    for h in range(N_HEADS):
        o = r[h * tnew:(h + 1) * tnew, h * V_HEAD_DIM:(h + 1) * V_HEAD_DIM]
        outs.append(_rms(o, gh_ref[...]) * (1.0 - LAM_INIT))
    o_ref[row] = jnp.concatenate(outs, axis=1).astype(BF16)


def _ffn_kernel(*refs, carry, ns, bt, tt):
    if carry:
        (x_ref, yp_ref, at_ref, mod_ref, gpm_ref, gpf_ref, gpo_ref, w_out_ref, w_up_ref, cw_ref, cb_ref,
         w_down_ref, y_ref, cs_ref, *buf_refs, carry_ref) = refs
    else:
        (x_ref, yp_ref, at_ref, mod_ref, gpm_ref, gpf_ref, gpo_ref, w_out_ref, w_up_ref, cw_ref, cb_ref,
         w_down_ref, prefix_ref, y_ref, cs_ref, *buf_refs) = refs
    m = bt * tt
    n_chunks = D_FF // FF_CHUNK
    streams = range(ns)

    def rows(s):
        return (slice(None), slice(s * tt, (s + 1) * tt)) if carry else (slice(s * bt, (s + 1) * bt),)

    def mod_row(s, i):
        return mod_ref[:, i:i + 1, :] if carry else mod_ref[s * bt:(s + 1) * bt, i:i + 1, :]

    def buf(s, slot):
        return buf_refs[4 * s + slot]

    if carry:
        @pl.when(pl.program_id(1) == 0)
        def _():
            carry_ref[...] = jnp.zeros_like(carry_ref)
    halo_rows = CONV_HALO if carry else prefix_ref.shape[1]

    x1, h2 = {}, {}
    for s in streams:
        mix_in = jnp.concatenate([yp_ref[rows(s)], at_ref[rows(s)]], axis=-1).reshape(m, D_MODEL)
        mix = jnp.dot(mix_in, w_out_ref[...], preferred_element_type=F32).reshape(bt, tt, D_MODEL)
        x1[s] = x_ref[rows(s)] + mod_row(s, 2) * _rms(mix, gpm_ref[...])
        h = _rms(x1[s], gpf_ref[...]) * (1.0 + mod_row(s, 4)) + mod_row(s, 3)
        h2[s] = h.reshape(m, D_MODEL).astype(BF16)

    def up_part(s, slot, off):
        sl = slice(off, off + FF_CHUNK)
        u3 = jnp.dot(h2[s], w_up_ref[:, sl], preferred_element_type=F32).reshape(bt, tt, FF_CHUNK)
        if not carry:
            halo = prefix_ref[s * bt:(s + 1) * bt, :, sl]
        elif s == 0:
            halo = carry_ref[:, :, sl]
        else:
            halo = buf(s - 1, slot)[:, tt:tt + CONV_HALO, :]
        buf(s, slot)[:, CONV_HALO - halo_rows:CONV_HALO, :] = halo
        buf(s, slot)[:, CONV_HALO:, :] = u3
        if not carry:
            cs_ref[s * bt:(s + 1) * bt, :, sl] = u3[:, tt - halo_rows:, :]
        elif s == ns - 1:
            cs_ref[:, :, sl] = u3[:, tt - halo_rows:, :]

    def conv_part(s, slot, off):
        sl = slice(off, off + FF_CHUNK)
        b = buf(s, slot)
        y = (cb_ref[:, :, sl]
             + cw_ref[0:1, :, sl] * b[:, CONV_HALO - 2:CONV_HALO - 2 + tt, :]
             + cw_ref[1:2, :, sl] * b[:, CONV_HALO - 1:CONV_HALO - 1 + tt, :]
             + cw_ref[2:3, :, sl] * b[:, CONV_HALO:, :])
        return y.reshape(m, FF_CHUNK)

    def up_chunk(s, c):
        up_part(s, 2 * (c % 2), c * FF_CHUNK)
        up_part(s, 2 * (c % 2) + 1, D_FF + c * FF_CHUNK)

    f = {s: jnp.zeros((m, D_MODEL), F32) for s in streams}
    for s in streams:
        up_chunk(s, 0)
    for c in range(n_chunks):
        if c + 1 < n_chunks:
            for s in streams:
                up_chunk(s, c + 1)
        for s in streams:
            gate = conv_part(s, 2 * (c % 2), c * FF_CHUNK)
            val = conv_part(s, 2 * (c % 2) + 1, D_FF + c * FF_CHUNK)
            act = (_gelu_tanh(gate) * val).astype(BF16)
            f[s] = f[s] + jnp.dot(act, w_down_ref[c * FF_CHUNK:(c + 1) * FF_CHUNK, :],
                                  preferred_element_type=F32)
    if carry:
        carry_ref[...] = cs_ref[...]
    for s in streams:
        y_ref[rows(s)] = x1[s] + mod_row(s, 5) * _rms(f[s].reshape(bt, tt, D_MODEL), gpo_ref[...])


def _ffn(x, yp, attn, mod, g_post_mix, g_pre_ffn, g_post_ffn, w_out, w_up, conv_w, conv_b, w_down, prefix):
    nb, t, _ = x.shape
    carry = prefix is None
    ns = FFN_STREAMS
    if carry:
        bt, tt = 1, ROW_TILE
        grid = (nb, t // (ns * tt))
        tile = lambda w: pl.BlockSpec((1, ns * tt, w), lambda b, i: (b, i, 0))
        per_b = lambda r, w: pl.BlockSpec((1, r, w), lambda b, i: (b, 0, 0))
        sem = ("arbitrary", "arbitrary")
    else:
        bt, tt = ROW_TILE // t, t
        grid = (nb // (ns * bt),)
        tile = lambda w: pl.BlockSpec((ns * bt, tt, w), lambda i: (i, 0, 0))
        per_b = lambda r, w: pl.BlockSpec((ns * bt, r, w), lambda i: (i, 0, 0))
        sem = ("arbitrary",)
    gspec = _const_spec((1, 1, D_MODEL))
    in_specs = [tile(D_MODEL), tile(POOL_WIDTH), tile(ATTN_WIDTH), per_b(N_MOD, D_MODEL),
                gspec, gspec, gspec,
                _const_spec((D_MODEL, D_MODEL)), _const_spec((D_MODEL, 2 * D_FF)),
                _const_spec((CONV_WIDTH, 1, 2 * D_FF)), _const_spec((1, 1, 2 * D_FF)),
                _const_spec((D_FF, D_MODEL))]
    args = [x, yp, attn, mod, g_post_mix.reshape(1, 1, -1), g_pre_ffn.reshape(1, 1, -1),
            g_post_ffn.reshape(1, 1, -1), w_out, w_up, conv_w.reshape(CONV_WIDTH, 1, -1),
            conv_b.reshape(1, 1, -1), w_down]
    scratch = [pltpu.VMEM((bt, CONV_HALO + tt, FF_CHUNK), F32) for _ in range(4 * ns)]
    if carry:
        halo_rows = CONV_HALO
        scratch.append(pltpu.VMEM((1, halo_rows, 2 * D_FF), F32))
    else:
        halo_rows = prefix.shape[1]
        in_specs.append(per_b(halo_rows, 2 * D_FF))
        args.append(prefix)
    return pl.pallas_call(
        functools.partial(_ffn_kernel, carry=carry, ns=ns, bt=bt, tt=tt),
        out_shape=(jax.ShapeDtypeStruct((nb, t, D_MODEL), F32),
                   jax.ShapeDtypeStruct((nb, halo_rows, 2 * D_FF), F32)),
        grid=grid, in_specs=in_specs,
        out_specs=(tile(D_MODEL), per_b(halo_rows, 2 * D_FF)),
        scratch_shapes=scratch,
        compiler_params=pltpu.CompilerParams(dimension_semantics=sem, vmem_limit_bytes=VMEM_LIMIT),
        name="ffn_prompt" if carry else "ffn_sample",
    )(*args)


def kernel(x_prompt, x_sample, c_prompt, c_sample, cache_k, cache_v, page_table, state_pool, state_conv,
           w_ada, b_ada, g_pre_mix, g_post_mix, g_pre_ffn, g_post_ffn, w_in, w_out, w_pool, pool_scale,
           lam_q1, lam_k1, lam_q2, lam_k2, g_head, rel_bias, w_up, conv_w, conv_b, w_down):
    nbp, seq, _ = x_prompt.shape
    nbs, dec_seq, _ = x_sample.shape
    past_len = page_table.shape[1] * cache_k.shape[2]
    l = 0

    lam, bias_p, bias_s, b31s = _setup(rel_bias, lam_q1[l:l + 1], lam_k1[l:l + 1],
                                             lam_q2[l:l + 1], lam_k2[l:l + 1])
    mod = _modulation(jnp.concatenate([c_prompt, c_sample], axis=0), w_ada[l], b_ada[l])
    mod = mod.reshape(nbp + nbs, N_MOD, D_MODEL)
    mod_p, mod_s = mod[:nbp], mod[nbp:]

    w_in_b = w_in[l].astype(BF16)
    w_kt_b = w_in[l][:, POOL_WIDTH + QK_WIDTH:POOL_WIDTH + 2 * QK_WIDTH].T.astype(BF16)
    w_pool_b = w_pool[l].astype(BF16)
    w_out_b = w_out[l].astype(BF16)
    w_up_b = w_up[l].astype(BF16)
    w_down_b = w_down[l].astype(BF16)

    pool_prefix = jnp.pad(state_pool[l], ((0, 0), (POOL_HALO - POOL_STATE, 0), (0, 0)))

    q_s, k_s, v_s, yp_s, pstate_s = _inproj(
        x_sample, mod_s, g_pre_mix[l], w_in_b, w_kt_b, w_pool_b, pool_scale[l], pool_prefix, pos0=past_len)

    n_phys, page = cache_k.shape[1], cache_k.shape[2]
    ck = jnp.transpose(cache_k[l], (0, 2, 3, 4, 1)).reshape(n_phys, QK_WIDTH, page)
    cv = cache_v[l].reshape(n_phys, page * N_HEADS, V_HEAD_DIM)
    sample = (page_table.shape[1], q_s, k_s, v_s, bias_s, b31s, lam, g_head[l], ck, cv)

    n_inproj_steps = nbp * (seq // ROW_TILE)
    n_attn_steps = nbp * N_HEADS
    seqs_in_attn = nbs - n_inproj_steps
    assert seqs_in_attn % (n_attn_steps * SAMPLE_SLOTS) == 0
    q_p, kt_p, v_p, kb_p, vb_p, yp_p, pstate_p, attn_s2 = _inproj(
        x_prompt, mod_p, g_pre_mix[l], w_in_b, w_kt_b, w_pool_b, pool_scale[l], None, pos0=0,
        page_table=page_table, sample=sample, seq0=seqs_in_attn, n_seq=1)
    attn_p, attn_s1 = _attention(q_p, kb_p, vb_p, bias_p, lam, g_head[l], page_table, sample,
                                 seq0=0, n_seq=seqs_in_attn // n_attn_steps)
    attn_s = jnp.concatenate([attn_s1, attn_s2], axis=0)
    k_p = jnp.transpose(kt_p.reshape(nbp, N_HEADS, 2, HEAD_DIM, seq), (0, 4, 1, 2, 3))

    outs = []
    for (x, md, yp, attn, k, v, pstate, cprefix) in (
            (x_prompt, mod_p, yp_p, attn_p, k_p, v_p, pstate_p, None),
            (x_sample, mod_s, yp_s, attn_s, k_s, v_s, pstate_s, state_conv[l])):
        nb, t = x.shape[0], x.shape[1]
        y, cstate = _ffn(x, yp, attn, md, g_post_mix[l], g_pre_ffn[l], g_post_ffn[l],
                         w_out_b, w_up_b, conv_w[l], conv_b[l], w_down_b, cprefix)
        outs.append((y,
                     k.reshape(1, nb, t, N_HEADS, 2, HEAD_DIM),
                     v.reshape(1, nb, t, N_HEADS, V_HEAD_DIM),
                     pstate[None, :, POOL_HALO - POOL_STATE:, :],
                     cstate[None, :, cstate.shape[1] - (CONV_WIDTH - 1):, :]))
    (yp_, kp, vp, pp, cp), (ys_, ks, vs, ps, cs) = outs
    return (yp_, ys_, kp, vp, pp, cp, ks, vs, ps, cs)
```

```python
import functools
import math

import jax
import jax.numpy as jnp
from jax import lax
from jax.experimental import pallas as pl
from jax.experimental.pallas import tpu as pltpu

F32 = jnp.float32
BF16 = jnp.bfloat16

D_MODEL = 1024
POOL_WIDTH = 512
POOL_WINDOWS = (2, 4, 8, 16)
POOL_GROUP_DIM = 128
POOL_STATE = 15
POOL_HALO = 16
N_HEADS = 4
HEAD_DIM = 64
V_HEAD_DIM = 128
QK_WIDTH = 512
ATTN_WIDTH = 512
PROJ_WIDTH = 2048
D_FF = 2816
CONV_WIDTH = 3
CONV_HALO = 8
N_BUCKETS = 32
MAX_EXACT = N_BUCKETS // 2
MAX_DISTANCE = 128
N_MOD = 6
EPS = 1e-6
LAM_INIT = 0.8 - 0.6 * math.exp(-0.3 * 0)
NEG_BIG = -1e30
LOG2E = math.log2(math.e)
SUM_ROWS = 16

LANES = 128
MXU_TILE = 256
VMEM_LIMIT = 56 * 1024 * 1024

ATTN_TQ = 256
ATTN_TK = 256
ATTN_CHAIN_GAP = 4
ATTN_LOOKAHEAD = 2
ROW_TILE = 256
FF_CHUNK = 256
SAMPLE_SLOTS = 2
FFN_STREAMS = 2


def _const_spec(shape):
    nd = len(shape)
    return pl.BlockSpec(shape, lambda *_: (0,) * nd, pipeline_mode=pl.Buffered(1))


def _rms(x, g):
    return x * lax.rsqrt(jnp.mean(x * x, axis=-1, keepdims=True) + EPS) * g


def _gelu_tanh(x):
    c0 = -2.0 * math.sqrt(2.0 / math.pi) * LOG2E
    return x / (1.0 + jnp.exp2(x * (c0 + (c0 * 0.044715) * (x * x))))


def _bucket(dist):
    d = jnp.maximum(dist, 1).astype(F32)
    large = MAX_EXACT + jnp.floor(jnp.log(d / MAX_EXACT) / math.log(MAX_DISTANCE / MAX_EXACT)
                                  * (N_BUCKETS - MAX_EXACT)).astype(jnp.int32)
    large = jnp.minimum(large, N_BUCKETS - 1)
    return jnp.where(dist < MAX_EXACT, dist, large)


def _setup_kernel(rb_ref, lq1_ref, lk1_ref, lq2_ref, lk2_ref,
                  lam_ref, bp_ref, bs_ref, b31s_ref):
    lam = (jnp.exp(jnp.sum(lq1_ref[...] * lk1_ref[...], axis=-1, keepdims=True))
           - jnp.exp(jnp.sum(lq2_ref[...] * lk2_ref[...], axis=-1, keepdims=True)) + LAM_INIT)
    lam_ref[...] = jnp.broadcast_to(lam, lam_ref.shape)

    def lookup(bucket, h):
        out = jnp.full(bucket.shape, rb_ref[N_BUCKETS - 1, h], F32)
        for b in range(N_BUCKETS - 1):
            out = jnp.where(bucket == b, rb_ref[b, h], out)
        return out

    kk = lax.broadcasted_iota(jnp.int32, (ATTN_TK, ATTN_TQ), 0)
    r = lax.broadcasted_iota(jnp.int32, (ATTN_TK, ATTN_TQ), 1)
    for typ in range(2):
        dist = typ * ATTN_TK + r - kk
        bucket = _bucket(jnp.maximum(dist, 0))
        for h in range(N_HEADS):
            shifted = (lookup(bucket, h) - rb_ref[N_BUCKETS - 1, h]) * LOG2E
            bp_ref[h, typ] = jnp.where(dist >= 0, shifted, NEG_BIG)

    rows = 2 * 8
    t = lax.broadcasted_iota(jnp.int32, (rows, 2 * LANES), 0) % 8
    col = lax.broadcasted_iota(jnp.int32, (rows, 2 * LANES), 1)
    dist = jnp.where(col < LANES, LANES + t - col, t - (col - LANES))
    bucket = _bucket(jnp.maximum(dist, 0))
    for h in range(N_HEADS):
        bs_ref[h * rows:(h + 1) * rows, :] = jnp.where(dist >= 0, lookup(bucket, h), NEG_BIG)
        b31s_ref[h * rows:(h + 1) * rows, :] = jnp.full((rows, LANES), rb_ref[N_BUCKETS - 1, h], F32)


def _setup(rel_bias, lq1, lk1, lq2, lk2):
    vspec = pl.BlockSpec(memory_space=pltpu.VMEM)
    return pl.pallas_call(
        _setup_kernel,
        out_shape=(jax.ShapeDtypeStruct((8, LANES), F32),
                   jax.ShapeDtypeStruct((N_HEADS, 2, ATTN_TK, ATTN_TQ), F32),
                   jax.ShapeDtypeStruct((N_HEADS * 16, 2 * LANES), F32),
                   jax.ShapeDtypeStruct((N_HEADS * 16, LANES), F32)),
        in_specs=[pl.BlockSpec(memory_space=pltpu.SMEM), vspec, vspec, vspec, vspec],
        out_specs=(vspec, vspec, vspec, vspec),
        name="setup",
    )(rel_bias, lq1, lk1, lq2, lk2)


def _mod_kernel(c_ref, w_ref, b_ref, o_ref):
    c = c_ref[...]
    s = (c * jax.nn.sigmoid(c)).astype(BF16)
    o_ref[...] = jnp.dot(s, w_ref[...].astype(BF16), preferred_element_type=F32) + b_ref[...]


def _modulation(c_all, w_ada, b_ada):
    n = c_all.shape[0]
    tn = 1024
    return pl.pallas_call(
        _mod_kernel,
        out_shape=jax.ShapeDtypeStruct((n, N_MOD * D_MODEL), F32),
        grid=(N_MOD * D_MODEL // tn,),
        in_specs=[pl.BlockSpec((n, D_MODEL), lambda j: (0, 0)),
                  pl.BlockSpec((D_MODEL, tn), lambda j: (0, j)),
                  pl.BlockSpec((1, tn), lambda j: (0, j))],
        out_specs=pl.BlockSpec((n, tn), lambda j: (0, j)),
        compiler_params=pltpu.CompilerParams(dimension_semantics=("arbitrary",),
                                             vmem_limit_bytes=VMEM_LIMIT),
        name="modulation",
    )(c_all, w_ada, b_ada.reshape(1, -1))


def _inproj_kernel(*refs, carry, pos0, bt, tt, share=None):
    sample_seq = None
    if carry and share is not None:
        pt_ref, refs = refs[0], refs[1:]
        share_in, os_ref = refs[7:7 + N_SHARE_IN], refs[7 + N_SHARE_IN + 7]
        share_scratch = refs[-N_SHARE_SCRATCH:]
        refs = refs[:7] + refs[7 + N_SHARE_IN:7 + N_SHARE_IN + 7] + refs[7 + N_SHARE_IN + 8:-N_SHARE_SCRATCH]
        step = pl.program_id(0) * pl.num_programs(1) + pl.program_id(1)
        n_steps = pl.num_programs(0) * pl.num_programs(1)
        sample_seq = _sample_share(step, n_steps, pt_ref, *share_in, os_ref, *share_scratch, **share)
    if carry:
        (x_ref, mod_ref, g_ref, w_in_ref, w_pool_ref, ps_ref, wkt_ref,
         q_ref, k_ref, v_ref, kb_ref, vb_ref, yp_ref, st_ref, ext_ref, carry_ref) = refs
    else:
        (x_ref, mod_ref, g_ref, w_in_ref, w_pool_ref, ps_ref, prefix_ref,
         q_ref, k_ref, v_ref, yp_ref, st_ref, ext_ref) = refs
    m = bt * tt
    x = x_ref[...]
    h = _rms(x, g_ref[...]) * (1.0 + mod_ref[:, 1:2, :]) + mod_ref[:, 0:1, :]
    h2 = h.reshape(m, D_MODEL).astype(BF16)

    def proj(off, width):
        return jnp.dot(h2, w_in_ref[:, off:off + width], preferred_element_type=F32)

    u3 = proj(0, POOL_WIDTH).reshape(bt, tt, POOL_WIDTH)
    if carry:
        @pl.when(pl.program_id(1) == 0)
        def _():
            carry_ref[...] = jnp.zeros_like(carry_ref)
        ext_ref[:, 0:POOL_HALO, :] = carry_ref[...]
    else:
        ext_ref[:, 0:POOL_HALO, :] = prefix_ref[...]
    ext_ref[:, POOL_HALO:, :] = u3
    if carry:
        carry_ref[...] = u3[:, tt - POOL_HALO:, :]
        t_base = pl.program_id(1) * tt
    else:
        t_base = 0
    st_ref[...] = ext_ref[:, tt:tt + POOL_HALO, :]

    if sample_seq is not None:
        for r in range(os_ref.shape[0]):
            sample_seq(r)

    zq = proj(POOL_WIDTH, QK_WIDTH)
    q_scale = HEAD_DIM ** -0.5 * (LOG2E if carry else 1.0)
    q_ref[...] = (zq * q_scale).astype(BF16).reshape(bt, tt, QK_WIDTH)
    zv = proj(POOL_WIDTH + 2 * QK_WIDTH, ATTN_WIDTH)
    if carry:
        vb_ref[...] = zv.astype(BF16).reshape(bt, tt, ATTN_WIDTH)
        for hd in range(N_HEADS):
            v_ref[0, pl.ds(hd, tt, stride=N_HEADS), :] = zv[:, hd * V_HEAD_DIM:(hd + 1) * V_HEAD_DIM]
    else:
        v_ref[...] = zv.reshape(bt, tt, ATTN_WIDTH)

    pos = pos0 + t_base + lax.broadcasted_iota(jnp.int32, (bt, tt, POOL_GROUP_DIM), 1)
    for g, w in enumerate(POOL_WINDOWS):
        sl = slice(g * POOL_GROUP_DIM, (g + 1) * POOL_GROUP_DIM)
        tok = ext_ref[:, POOL_HALO:, sl]
        acc = tok
        for j in range(1, w):
            acc = acc + ext_ref[:, POOL_HALO - j:POOL_HALO - j + tt, sl]
        cnt = jnp.minimum(pos + 1, w).astype(F32)
        d = (acc / cnt - tok).reshape(m, POOL_GROUP_DIM).astype(BF16)
        y = jnp.dot(d, w_pool_ref[g], preferred_element_type=F32) * ps_ref[:, sl]
        yp_ref[:, :, sl] = y.astype(BF16).reshape(bt, tt, POOL_GROUP_DIM)

    if carry:
        zkt = lax.dot_general(wkt_ref[...], h2, (((1,), (1,)), ((), ())), preferred_element_type=F32)
        k_ref[0] = zkt
        kb_ref[...] = zkt.T.astype(BF16).reshape(bt, tt, QK_WIDTH)
    else:
        k_ref[...] = proj(POOL_WIDTH + QK_WIDTH, QK_WIDTH).reshape(bt, tt, QK_WIDTH)


def _inproj(x, mod, g_pre, w_in, w_kt, w_pool, pool_scale, prefix, *, pos0, page_table=None, sample=None,
            seq0=0, n_seq=0):
    nb, t, _ = x.shape
    carry = prefix is None
    if carry:
        bt, tt = 1, ROW_TILE
        grid = (nb, t // tt)
        tile = lambda w: pl.BlockSpec((bt, tt, w), lambda b, i, *_: (b, i, 0))
        per_b = lambda r, w: pl.BlockSpec((bt, r, w), lambda b, i, *_: (b, 0, 0))
        sem = ("arbitrary", "arbitrary")
    else:
        bt, tt = ROW_TILE // t, t
        grid = (nb // bt,)
        tile = lambda w: pl.BlockSpec((bt, tt, w), lambda i: (i, 0, 0))
        per_b = lambda r, w: pl.BlockSpec((bt, r, w), lambda i: (i, 0, 0))
        sem = ("arbitrary",)
    in_specs = [tile(D_MODEL), per_b(N_MOD, D_MODEL), _const_spec((1, 1, D_MODEL)),
                _const_spec((D_MODEL, PROJ_WIDTH)),
                _const_spec((len(POOL_WINDOWS), POOL_GROUP_DIM, POOL_GROUP_DIM)),
                _const_spec((1, POOL_WIDTH))]
    args = [x, mod, g_pre.reshape(1, 1, D_MODEL), w_in, w_pool, pool_scale.reshape(1, POOL_WIDTH)]
    scratch = [pltpu.VMEM((bt, POOL_HALO + tt, POOL_WIDTH), F32)]
    st_shape = jax.ShapeDtypeStruct((nb, POOL_HALO, POOL_WIDTH), F32)
    act = lambda w, dt: jax.ShapeDtypeStruct((nb, t, w), dt)
    if carry:
        scratch.append(pltpu.VMEM((1, POOL_HALO, POOL_WIDTH), F32))
        in_specs.append(_const_spec((QK_WIDTH, D_MODEL)))
        args.append(w_kt)
        out_shape = (act(QK_WIDTH, BF16),
                     jax.ShapeDtypeStruct((nb, QK_WIDTH, t), F32),
                     jax.ShapeDtypeStruct((nb, t * N_HEADS, V_HEAD_DIM), F32), act(QK_WIDTH, BF16),
                     act(ATTN_WIDTH, BF16), act(POOL_WIDTH, BF16), st_shape)
        out_specs = (tile(QK_WIDTH),
                     pl.BlockSpec((1, QK_WIDTH, tt), lambda b, i, *_: (b, 0, i)),
                     pl.BlockSpec((1, tt * N_HEADS, V_HEAD_DIM), lambda b, i, *_: (b, i, 0)), tile(QK_WIDTH),
                     tile(ATTN_WIDTH), tile(POOL_WIDTH), per_b(POOL_HALO, POOL_WIDTH))
        if sample is not None:
            steps_per_b = t // tt
            sh_in, sh_args, sh_out, sh_out_spec, sh_scratch, sh_static = _share_plumbing(
                sample, lambda b, i: b * steps_per_b + i, seq0, n_seq, nb * steps_per_b)
            return pl.pallas_call(
                functools.partial(_inproj_kernel, carry=carry, pos0=pos0, bt=bt, tt=tt, share=sh_static),
                out_shape=out_shape + (sh_out,),
                grid_spec=pltpu.PrefetchScalarGridSpec(
                    num_scalar_prefetch=1, grid=grid, in_specs=in_specs + sh_in,
                    out_specs=out_specs + (sh_out_spec,), scratch_shapes=scratch + sh_scratch),
                compiler_params=pltpu.CompilerParams(dimension_semantics=sem, vmem_limit_bytes=VMEM_LIMIT),
                name="inproj_prompt",
            )(page_table, *args, *sh_args)
    else:
        in_specs.append(per_b(POOL_HALO, POOL_WIDTH))
        args.append(prefix)
        out_shape = (act(QK_WIDTH, BF16), act(QK_WIDTH, F32), act(ATTN_WIDTH, F32),
                     act(POOL_WIDTH, BF16), st_shape)
        out_specs = (tile(QK_WIDTH), tile(QK_WIDTH), tile(ATTN_WIDTH), tile(POOL_WIDTH),
                     per_b(POOL_HALO, POOL_WIDTH))
    return pl.pallas_call(
        functools.partial(_inproj_kernel, carry=carry, pos0=pos0, bt=bt, tt=tt),
        out_shape=out_shape, grid=grid, in_specs=in_specs, out_specs=out_specs,
        scratch_shapes=scratch,
        compiler_params=pltpu.CompilerParams(dimension_semantics=sem, vmem_limit_bytes=VMEM_LIMIT),
        name="inproj_prompt" if carry else "inproj_sample",
    )(*args)


def _attn_unit_order(nt):
    left = {i: i + 1 for i in range(nt)}
    last = {i: -ATTN_CHAIN_GAP for i in range(nt)}
    order = []
    while any(left.values()):
        n = len(order)
        cands = [i for i in range(nt) if left[i]]
        spaced = [i for i in cands if n - last[i] >= ATTN_CHAIN_GAP]
        i = max(spaced, key=lambda t: (left[t], t)) if spaced else max(cands, key=lambda t: n - last[t])
        order.append((i, i + 1 - left[i]))
        left[i] -= 1
        last[i] = n
    return order


def _attn_prompt_head(q_ref, kb_ref, vb_ref, bp_ref, lam_ref, gh_ref, o_ref,
                      w_ref, vt_ref, m_ref, acc_ref, after_pair):
    tq, tk = ATTN_TQ, ATTN_TK
    nt = q_ref.shape[1] // tq
    zero = jnp.zeros((HEAD_DIM, tq), BF16)
    for i in range(nt):
        qt = q_ref[0, i * tq:(i + 1) * tq, :].astype(F32).T.astype(BF16)
        w_ref[i] = jnp.concatenate([jnp.concatenate([qt[:HEAD_DIM], zero], axis=1),
                                    jnp.concatenate([zero, qt[HEAD_DIM:]], axis=1)], axis=0)
    lam = lam_ref[0:1, 0:1]

    def scores(i, j):
        return jnp.dot(kb_ref[0, j * tk:(j + 1) * tk, :], w_ref[i], preferred_element_type=F32)

    for j in range(nt):
        vt_ref[j] = jnp.concatenate([vb_ref[0, j * tk:(j + 1) * tk, :].astype(F32).T,
                                     jnp.ones((SUM_ROWS, tk), F32)], axis=0).astype(BF16)

    units = _attn_unit_order(nt)
    pending = [scores(*u) for u in units[:ATTN_LOOKAHEAD]]
    for n, (i, j) in enumerate(units):
        st = pending.pop(0)
        if n + ATTN_LOOKAHEAD < len(units):
            pending.append(scores(*units[n + ATTN_LOOKAHEAD]))
        vt = vt_ref[j]
        if i - j < 2:
            b = bp_ref[0, i - j]
            st = jnp.concatenate([b, b], axis=1) + st
        mx = jnp.max(st, axis=0, keepdims=True)
        if j == 0:
            p = jnp.exp2(st - mx)
            acc_ref[i] = jnp.dot(vt, p.astype(BF16), preferred_element_type=F32)
            m_ref[i] = mx
        else:
            m_old = m_ref[i]
            m_new = jnp.maximum(m_old, mx)
            alpha = jnp.exp2(m_old - m_new)
            p = jnp.exp2(st - m_new)
            acc_ref[i] = alpha * acc_ref[i] + jnp.dot(vt, p.astype(BF16), preferred_element_type=F32)
            m_ref[i] = m_new
        if i == j:
            acc = acc_ref[j]
            on = acc[:V_HEAD_DIM] / acc[V_HEAD_DIM:V_HEAD_DIM + 1]
            ot = on[:, :tq] - lam * on[:, tq:]
            ot = ot * lax.rsqrt(jnp.mean(ot * ot, axis=0, keepdims=True) + EPS)
            o_ref[0, j * tq:(j + 1) * tq, :] = (ot.T * gh_ref[...] * (1.0 - LAM_INIT)).astype(BF16)
        after_pair(n, len(units))


N_SHARE_IN, N_SHARE_SCRATCH = 9, 4


def _attn_kernel(pt_ref, q_ref, kb_ref, vb_ref, bp_ref, lam_ref, gh_ref, *rest, share):
    share_in = rest[:N_SHARE_IN]
    o_ref, os_ref, w_ref, vt_ref, m_ref, acc_ref = rest[N_SHARE_IN:N_SHARE_IN + 6]
    share_scratch = rest[N_SHARE_IN + 6:]
    step = pl.program_id(0) * pl.num_programs(1) + pl.program_id(1)
    n_steps = pl.num_programs(0) * pl.num_programs(1)
    sample_seq = _sample_share(step, n_steps, pt_ref, *share_in, os_ref, *share_scratch, **share)
    n_seq = os_ref.shape[0]

    def after_pair(n, total):
        for r in range(n_seq):
            if n == (r + 1) * total // (n_seq + 1):
                sample_seq(r)

    _attn_prompt_head(q_ref, kb_ref, vb_ref, bp_ref, lam_ref, gh_ref, o_ref,
                      w_ref, vt_ref, m_ref, acc_ref, after_pair)


def _share_plumbing(sample, step_index, seq0, n_seq, n_steps):
    n_pages, q_s, k_s, v_s, bias_s, b31s, lam, g_head, ck, cv = sample
    tnew = q_s.shape[1]
    page = ck.shape[2]
    past = n_pages * page
    assert seq0 % n_seq == 0
    seqs = lambda w: pl.BlockSpec((n_seq, tnew, w), lambda *g: (seq0 // n_seq + step_index(*g[:-1]), 0, 0))
    out_spec = pl.BlockSpec((n_seq, tnew, ATTN_WIDTH), lambda *g: (step_index(*g[:-1]), 0, 0))
    const = lambda a: pl.BlockSpec(a.shape, lambda *g: (0,) * a.ndim)
    hbm = pl.BlockSpec(memory_space=pl.ANY)
    gh = g_head.reshape(1, V_HEAD_DIM)
    in_specs = [seqs(QK_WIDTH), seqs(QK_WIDTH), seqs(ATTN_WIDTH), const(bias_s), const(b31s), const(lam),
                const(gh), hbm, hbm]
    args = [q_s, k_s, v_s, bias_s, b31s, lam, gh, ck, cv]
    out_shape = jax.ShapeDtypeStruct((n_steps * n_seq, tnew, ATTN_WIDTH), BF16)
    scratch = [pltpu.VMEM((SAMPLE_SLOTS, n_pages, QK_WIDTH, page), F32),
               pltpu.VMEM((SAMPLE_SLOTS, n_pages, page * N_HEADS, V_HEAD_DIM), F32),
               pltpu.VMEM((past + LANES, ATTN_WIDTH), BF16),
               pltpu.SemaphoreType.DMA((SAMPLE_SLOTS,))]
    assert len(in_specs) == N_SHARE_IN and len(scratch) == N_SHARE_SCRATCH
    return in_specs, args, out_shape, out_spec, scratch, dict(n_pages=n_pages, page=page, seq0=seq0)


def _sample_share(step, n_steps, pt_ref, qs_ref, kn_ref, vn_ref, bs_ref, b31s_ref, lam_ref, gh_ref,
                  ck_hbm, cv_hbm, os_ref, kraw_ref, vraw_ref, vbf_ref, sem_ref,
                  *, n_pages, page, seq0):
    n_seq = qs_ref.shape[0]
    first = seq0 + step * n_seq

    def page_copies(seq, slot):
        cps = []
        for j in range(n_pages):
            pg = pt_ref[seq, j]
            cps.append(pltpu.make_async_copy(ck_hbm.at[pg], kraw_ref.at[slot, j], sem_ref.at[slot]))
            cps.append(pltpu.make_async_copy(cv_hbm.at[pg], vraw_ref.at[slot, j], sem_ref.at[slot]))
        return cps

    def start(seq):
        for cp in page_copies(seq, seq & (SAMPLE_SLOTS - 1)):
            cp.start()

    def wait(seq):
        for cp in page_copies(seq, seq & (SAMPLE_SLOTS - 1)):
            cp.wait()

    @pl.when(step == 0)
    def _():
        start(first)

    def sample_seq(r):
        seq = first + r
        if r + 1 < n_seq:
            start(seq + 1)
        else:
            @pl.when(step + 1 < n_steps)
            def _():
                start(seq + 1)
        wait(seq)
        _attn_sample_one(r, seq & (SAMPLE_SLOTS - 1), qs_ref, kn_ref, vn_ref, bs_ref, b31s_ref, lam_ref, gh_ref,
                         os_ref, kraw_ref, vraw_ref, vbf_ref, n_pages=n_pages, page=page)

    return sample_seq


def _attention(q, kb, vb, bias_p, lam, g_head, page_table, sample, *, seq0, n_seq):
    nb, t, _ = q.shape
    tq = ATTN_TQ
    nt = t // tq
    hw = V_HEAD_DIM
    n_steps = nb * N_HEADS
    sh_in, sh_args, sh_out, sh_out_spec, sh_scratch, sh_static = _share_plumbing(
        sample, lambda b, h: b * N_HEADS + h, seq0, n_seq, n_steps)
    head = pl.BlockSpec((1, t, hw), lambda b, h, pt: (b, 0, h))
    const = lambda shape: pl.BlockSpec(shape, lambda b, h, pt: (0,) * len(shape))
    return pl.pallas_call(
        functools.partial(_attn_kernel, share=sh_static),
        out_shape=(jax.ShapeDtypeStruct((nb, t, ATTN_WIDTH), BF16), sh_out),
        grid_spec=pltpu.PrefetchScalarGridSpec(
            num_scalar_prefetch=1, grid=(nb, N_HEADS),
            in_specs=[head, head, head,
                      pl.BlockSpec((1, 2, ATTN_TK, ATTN_TQ), lambda b, h, pt: (h, 0, 0, 0)),
                      const((8, LANES)), const((1, hw))] + sh_in,
            out_specs=(head, sh_out_spec),
            scratch_shapes=[pltpu.VMEM((nt, 2 * HEAD_DIM, 2 * tq), BF16),
                            pltpu.VMEM((nt, hw + SUM_ROWS, ATTN_TK), BF16),
                            pltpu.VMEM((nt, 1, 2 * tq), F32),
                            pltpu.VMEM((nt, hw + SUM_ROWS, 2 * tq), F32)] + sh_scratch),
        compiler_params=pltpu.CompilerParams(
            dimension_semantics=("arbitrary", "arbitrary"), vmem_limit_bytes=VMEM_LIMIT),
        name="attention",
    )(page_table, q, kb, vb, bias_p, lam, g_head.reshape(1, hw), *sh_args)


def _attn_sample_one(row, slot, q_ref, kn_ref, vn_ref, bs_ref, b31s_ref, lam_ref, gh_ref, o_ref,
                     kraw_ref, vraw_ref, vbf_ref, *, n_pages, page):
    past = n_pages * page
    tnew = q_ref.shape[1]
    n_maps = 2 * N_HEADS

    q = q_ref[row].astype(F32)
    qt = jnp.concatenate([q] * n_maps, axis=0)
    rg = lax.broadcasted_iota(jnp.int32, qt.shape, 0) // tnew
    cg = lax.broadcasted_iota(jnp.int32, qt.shape, 1) // HEAD_DIM
    qbd = jnp.where(rg == cg, qt, 0.0)

    for j in range(n_pages):
        for h in range(N_HEADS):
            vbf_ref[j * page:(j + 1) * page, h * V_HEAD_DIM:(h + 1) * V_HEAD_DIM] = (
                vraw_ref[slot, j, pl.ds(h, page, stride=N_HEADS), :].astype(BF16))
    pad = jnp.zeros((LANES - tnew, QK_WIDTH), F32)
    k_new = jnp.concatenate([kn_ref[row], pad], axis=0)
    vbf_ref[past:, :] = jnp.concatenate([vn_ref[row], pad], axis=0).astype(BF16)

    k_past = jnp.concatenate([kraw_ref[slot, j] for j in range(n_pages)], axis=1)
    s_past = jnp.dot(qbd, k_past, preferred_element_type=F32)
    s_new = lax.dot_general(qbd, k_new, (((1,), (1,)), ((), ())), preferred_element_type=F32)
    near = past - page
    s = jnp.concatenate([s_past[:, :near] + b31s_ref[:, 0:1],
                         jnp.concatenate([s_past[:, near:], s_new], axis=1) + bs_ref[...]], axis=1)
    mx = jnp.max(s, axis=-1, keepdims=True)
    p = jnp.exp(s - mx)
    pn = p / jnp.sum(p, axis=-1, keepdims=True)
    lam = lam_ref[0:1, 0:1]
    a = jnp.concatenate(
        [pn[2 * h * tnew:(2 * h + 1) * tnew] - lam * pn[(2 * h + 1) * tnew:(2 * h + 2) * tnew]
         for h in range(N_HEADS)], axis=0).astype(BF16)
    r = jnp.dot(a, vbf_ref[...], preferred_element_type=F32)
    outs = []
    for h in range(N_HEADS):
        o = r[h * tnew:(h + 1) * tnew, h * V_HEAD_DIM:(h + 1) * V_HEAD_DIM]
        outs.append(_rms(o, gh_ref[...]) * (1.0 - LAM_INIT))
    o_ref[row] = jnp.concatenate(outs, axis=1).astype(BF16)


def _ffn_kernel(*refs, carry, ns, bt, tt):
    if carry:
        (x_ref, yp_ref, at_ref, mod_ref, gpm_ref, gpf_ref, gpo_ref, w_out_ref, w_up_ref, cw_ref, cb_ref,
         w_down_ref, y_ref, cs_ref, *buf_refs, carry_ref) = refs
    else:
        (x_ref, yp_ref, at_ref, mod_ref, gpm_ref, gpf_ref, gpo_ref, w_out_ref, w_up_ref, cw_ref, cb_ref,
         w_down_ref, prefix_ref, y_ref, cs_ref, *buf_refs) = refs
    m = bt * tt
    n_chunks = D_FF // FF_CHUNK
    streams = range(ns)

    def rows(s):
        return (slice(None), slice(s * tt, (s + 1) * tt)) if carry else (slice(s * bt, (s + 1) * bt),)

    def mod_row(s, i):
        return mod_ref[:, i:i + 1, :] if carry else mod_ref[s * bt:(s + 1) * bt, i:i + 1, :]

    def buf(s, slot):
        return buf_refs[4 * s + slot]

    if carry:
        @pl.when(pl.program_id(1) == 0)
        def _():
            carry_ref[...] = jnp.zeros_like(carry_ref)
    halo_rows = CONV_HALO if carry else prefix_ref.shape[1]

    x1, h2 = {}, {}
    for s in streams:
        mix_in = jnp.concatenate([yp_ref[rows(s)], at_ref[rows(s)]], axis=-1).reshape(m, D_MODEL)
        mix = jnp.dot(mix_in, w_out_ref[...], preferred_element_type=F32).reshape(bt, tt, D_MODEL)
        x1[s] = x_ref[rows(s)] + mod_row(s, 2) * _rms(mix, gpm_ref[...])
        h = _rms(x1[s], gpf_ref[...]) * (1.0 + mod_row(s, 4)) + mod_row(s, 3)
        h2[s] = h.reshape(m, D_MODEL).astype(BF16)

    def up_part(s, slot, off):
        sl = slice(off, off + FF_CHUNK)
        u3 = jnp.dot(h2[s], w_up_ref[:, sl], preferred_element_type=F32).reshape(bt, tt, FF_CHUNK)
        if not carry:
            halo = prefix_ref[s * bt:(s + 1) * bt, :, sl]
        elif s == 0:
            halo = carry_ref[:, :, sl]
        else:
            halo = buf(s - 1, slot)[:, tt:tt + CONV_HALO, :]
        buf(s, slot)[:, CONV_HALO - halo_rows:CONV_HALO, :] = halo
        buf(s, slot)[:, CONV_HALO:, :] = u3
        if not carry:
            cs_ref[s * bt:(s + 1) * bt, :, sl] = u3[:, tt - halo_rows:, :]
        elif s == ns - 1:
            cs_ref[:, :, sl] = u3[:, tt - halo_rows:, :]

    def conv_part(s, slot, off):
        sl = slice(off, off + FF_CHUNK)
        b = buf(s, slot)
        y = (cb_ref[:, :, sl]
             + cw_ref[0:1, :, sl] * b[:, CONV_HALO - 2:CONV_HALO - 2 + tt, :]
             + cw_ref[1:2, :, sl] * b[:, CONV_HALO - 1:CONV_HALO - 1 + tt, :]
             + cw_ref[2:3, :, sl] * b[:, CONV_HALO:, :])
        return y.reshape(m, FF_CHUNK)

    def up_chunk(s, c):
        up_part(s, 2 * (c % 2), c * FF_CHUNK)
        up_part(s, 2 * (c % 2) + 1, D_FF + c * FF_CHUNK)

    f = {s: jnp.zeros((m, D_MODEL), F32) for s in streams}
    for s in streams:
        up_chunk(s, 0)
    for c in range(n_chunks):
        if c + 1 < n_chunks:
            for s in streams:
                up_chunk(s, c + 1)
        for s in streams:
            gate = conv_part(s, 2 * (c % 2), c * FF_CHUNK)
            val = conv_part(s, 2 * (c % 2) + 1, D_FF + c * FF_CHUNK)
            act = (_gelu_tanh(gate) * val).astype(BF16)
            f[s] = f[s] + jnp.dot(act, w_down_ref[c * FF_CHUNK:(c + 1) * FF_CHUNK, :],
                                  preferred_element_type=F32)
    if carry:
        carry_ref[...] = cs_ref[...]
    for s in streams:
        y_ref[rows(s)] = x1[s] + mod_row(s, 5) * _rms(f[s].reshape(bt, tt, D_MODEL), gpo_ref[...])


def _ffn(x, yp, attn, mod, g_post_mix, g_pre_ffn, g_post_ffn, w_out, w_up, conv_w, conv_b, w_down, prefix):
    nb, t, _ = x.shape
    carry = prefix is None
    ns = FFN_STREAMS
    if carry:
        bt, tt = 1, ROW_TILE
        grid = (nb, t // (ns * tt))
        tile = lambda w: pl.BlockSpec((1, ns * tt, w), lambda b, i: (b, i, 0))
        per_b = lambda r, w: pl.BlockSpec((1, r, w), lambda b, i: (b, 0, 0))
        sem = ("arbitrary", "arbitrary")
    else:
        bt, tt = ROW_TILE // t, t
        grid = (nb // (ns * bt),)
        tile = lambda w: pl.BlockSpec((ns * bt, tt, w), lambda i: (i, 0, 0))
        per_b = lambda r, w: pl.BlockSpec((ns * bt, r, w), lambda i: (i, 0, 0))
        sem = ("arbitrary",)
    gspec = _const_spec((1, 1, D_MODEL))
    in_specs = [tile(D_MODEL), tile(POOL_WIDTH), tile(ATTN_WIDTH), per_b(N_MOD, D_MODEL),
                gspec, gspec, gspec,
                _const_spec((D_MODEL, D_MODEL)), _const_spec((D_MODEL, 2 * D_FF)),
                _const_spec((CONV_WIDTH, 1, 2 * D_FF)), _const_spec((1, 1, 2 * D_FF)),
                _const_spec((D_FF, D_MODEL))]
    args = [x, yp, attn, mod, g_post_mix.reshape(1, 1, -1), g_pre_ffn.reshape(1, 1, -1),
            g_post_ffn.reshape(1, 1, -1), w_out, w_up, conv_w.reshape(CONV_WIDTH, 1, -1),
            conv_b.reshape(1, 1, -1), w_down]
    scratch = [pltpu.VMEM((bt, CONV_HALO + tt, FF_CHUNK), F32) for _ in range(4 * ns)]
    if carry:
        halo_rows = CONV_HALO
        scratch.append(pltpu.VMEM((1, halo_rows, 2 * D_FF), F32))
    else:
        halo_rows = prefix.shape[1]
        in_specs.append(per_b(halo_rows, 2 * D_FF))
        args.append(prefix)
    return pl.pallas_call(
        functools.partial(_ffn_kernel, carry=carry, ns=ns, bt=bt, tt=tt),
        out_shape=(jax.ShapeDtypeStruct((nb, t, D_MODEL), F32),
                   jax.ShapeDtypeStruct((nb, halo_rows, 2 * D_FF), F32)),
        grid=grid, in_specs=in_specs,
        out_specs=(tile(D_MODEL), per_b(halo_rows, 2 * D_FF)),
        scratch_shapes=scratch,
        compiler_params=pltpu.CompilerParams(dimension_semantics=sem, vmem_limit_bytes=VMEM_LIMIT),
        name="ffn_prompt" if carry else "ffn_sample",
    )(*args)


def kernel(x_prompt, x_sample, c_prompt, c_sample, cache_k, cache_v, page_table, state_pool, state_conv,
           w_ada, b_ada, g_pre_mix, g_post_mix, g_pre_ffn, g_post_ffn, w_in, w_out, w_pool, pool_scale,
           lam_q1, lam_k1, lam_q2, lam_k2, g_head, rel_bias, w_up, conv_w, conv_b, w_down):
    nbp, seq, _ = x_prompt.shape
    nbs, dec_seq, _ = x_sample.shape
    past_len = page_table.shape[1] * cache_k.shape[2]
    l = 0

    lam, bias_p, bias_s, b31s = _setup(rel_bias, lam_q1[l:l + 1], lam_k1[l:l + 1],
                                             lam_q2[l:l + 1], lam_k2[l:l + 1])
    mod = _modulation(jnp.concatenate([c_prompt, c_sample], axis=0), w_ada[l], b_ada[l])
    mod = mod.reshape(nbp + nbs, N_MOD, D_MODEL)
    mod_p, mod_s = mod[:nbp], mod[nbp:]

    w_in_b = w_in[l].astype(BF16)
    w_kt_b = w_in[l][:, POOL_WIDTH + QK_WIDTH:POOL_WIDTH + 2 * QK_WIDTH].T.astype(BF16)
    w_pool_b = w_pool[l].astype(BF16)
    w_out_b = w_out[l].astype(BF16)
    w_up_b = w_up[l].astype(BF16)
    w_down_b = w_down[l].astype(BF16)

    pool_prefix = jnp.pad(state_pool[l], ((0, 0), (POOL_HALO - POOL_STATE, 0), (0, 0)))

    q_s, k_s, v_s, yp_s, pstate_s = _inproj(
        x_sample, mod_s, g_pre_mix[l], w_in_b, w_kt_b, w_pool_b, pool_scale[l], pool_prefix, pos0=past_len)

    n_phys, page = cache_k.shape[1], cache_k.shape[2]
    ck = jnp.transpose(cache_k[l], (0, 2, 3, 4, 1)).reshape(n_phys, QK_WIDTH, page)
    cv = cache_v[l].reshape(n_phys, page * N_HEADS, V_HEAD_DIM)
    sample = (page_table.shape[1], q_s, k_s, v_s, bias_s, b31s, lam, g_head[l], ck, cv)

    n_inproj_steps = nbp * (seq // ROW_TILE)
    n_attn_steps = nbp * N_HEADS
    seqs_in_attn = nbs - n_inproj_steps
    assert seqs_in_attn % (n_attn_steps * SAMPLE_SLOTS) == 0
    q_p, kt_p, v_p, kb_p, vb_p, yp_p, pstate_p, attn_s2 = _inproj(
        x_prompt, mod_p, g_pre_mix[l], w_in_b, w_kt_b, w_pool_b, pool_scale[l], None, pos0=0,
        page_table=page_table, sample=sample, seq0=seqs_in_attn, n_seq=1)
    attn_p, attn_s1 = _attention(q_p, kb_p, vb_p, bias_p, lam, g_head[l], page_table, sample,
                                 seq0=0, n_seq=seqs_in_attn // n_attn_steps)
    attn_s = jnp.concatenate([attn_s1, attn_s2], axis=0)
    k_p = jnp.transpose(kt_p.reshape(nbp, N_HEADS, 2, HEAD_DIM, seq), (0, 4, 1, 2, 3))

    outs = []
    for (x, md, yp, attn, k, v, pstate, cprefix) in (
            (x_prompt, mod_p, yp_p, attn_p, k_p, v_p, pstate_p, None),
            (x_sample, mod_s, yp_s, attn_s, k_s, v_s, pstate_s, state_conv[l])):
        nb, t = x.shape[0], x.shape[1]
        y, cstate = _ffn(x, yp, attn, md, g_post_mix[l], g_pre_ffn[l], g_post_ffn[l],
                         w_out_b, w_up_b, conv_w[l], conv_b[l], w_down_b, cprefix)
        outs.append((y,
                     k.reshape(1, nb, t, N_HEADS, 2, HEAD_DIM),
                     v.reshape(1, nb, t, N_HEADS, V_HEAD_DIM),
                     pstate[None, :, POOL_HALO - POOL_STATE:, :],
                     cstate[None, :, cstate.shape[1] - (CONV_WIDTH - 1):, :]))
    (yp_, kp, vp, pp, cp), (ys_, ks, vs, ps, cs) = outs
    return (yp_, ys_, kp, vp, pp, cp, ks, vs, ps, cs)
```

```python
import functools
import math

import jax
import jax.numpy as jnp
from jax import lax
from jax.experimental import pallas as pl
from jax.experimental.pallas import tpu as pltpu

F32 = jnp.float32
BF16 = jnp.bfloat16

D_MODEL = 1024
POOL_WIDTH = 512
POOL_WINDOWS = (2, 4, 8, 16)
POOL_GROUP_DIM = 128
POOL_STATE = 15
POOL_HALO = 16
N_HEADS = 4
HEAD_DIM = 64
V_HEAD_DIM = 128
QK_WIDTH = 512
ATTN_WIDTH = 512
PROJ_WIDTH = 2048
D_FF = 2816
CONV_WIDTH = 3
CONV_HALO = 8
N_BUCKETS = 32
MAX_EXACT = N_BUCKETS // 2
MAX_DISTANCE = 128
N_MOD = 6
EPS = 1e-6
LAM_INIT = 0.8 - 0.6 * math.exp(-0.3 * 0)
NEG_BIG = -1e30
LOG2E = math.log2(math.e)
SUM_ROWS = 16

LANES = 128
MXU_TILE = 256
VMEM_LIMIT = 56 * 1024 * 1024

ATTN_TQ = 256
ATTN_TK = 256
ATTN_CHAIN_GAP = 4
ATTN_LOOKAHEAD = 2
ROW_TILE = 256
FF_CHUNK = 256
SAMPLE_SLOTS = 2
FFN_STREAMS = 2


def _const_spec(shape):
    nd = len(shape)
    return pl.BlockSpec(shape, lambda *_: (0,) * nd, pipeline_mode=pl.Buffered(1))


def _rms(x, g):
    return x * lax.rsqrt(jnp.mean(x * x, axis=-1, keepdims=True) + EPS) * g


def _gelu_tanh(x):
    c0 = -2.0 * math.sqrt(2.0 / math.pi) * LOG2E
    return x / (1.0 + jnp.exp2(x * (c0 + (c0 * 0.044715) * (x * x))))


def _bucket(dist):
    d = jnp.maximum(dist, 1).astype(F32)
    large = MAX_EXACT + jnp.floor(jnp.log(d / MAX_EXACT) / math.log(MAX_DISTANCE / MAX_EXACT)
                                  * (N_BUCKETS - MAX_EXACT)).astype(jnp.int32)
    large = jnp.minimum(large, N_BUCKETS - 1)
    return jnp.where(dist < MAX_EXACT, dist, large)


def _setup_kernel(rb_ref, lq1_ref, lk1_ref, lq2_ref, lk2_ref,
                  lam_ref, bp_ref, bs_ref, b31s_ref):
    lam = (jnp.exp(jnp.sum(lq1_ref[...] * lk1_ref[...], axis=-1, keepdims=True))
           - jnp.exp(jnp.sum(lq2_ref[...] * lk2_ref[...], axis=-1, keepdims=True)) + LAM_INIT)
    lam_ref[...] = jnp.broadcast_to(lam, lam_ref.shape)

    def lookup(bucket, h):
        out = jnp.full(bucket.shape, rb_ref[N_BUCKETS - 1, h], F32)
        for b in range(N_BUCKETS - 1):
            out = jnp.where(bucket == b, rb_ref[b, h], out)
        return out

    kk = lax.broadcasted_iota(jnp.int32, (ATTN_TK, ATTN_TQ), 0)
    r = lax.broadcasted_iota(jnp.int32, (ATTN_TK, ATTN_TQ), 1)
    for typ in range(2):
        dist = typ * ATTN_TK + r - kk
        bucket = _bucket(jnp.maximum(dist, 0))
        for h in range(N_HEADS):
            shifted = (lookup(bucket, h) - rb_ref[N_BUCKETS - 1, h]) * LOG2E
            bp_ref[h, typ] = jnp.where(dist >= 0, shifted, NEG_BIG)

    rows = 2 * 8
    t = lax.broadcasted_iota(jnp.int32, (rows, 2 * LANES), 0) % 8
    col = lax.broadcasted_iota(jnp.int32, (rows, 2 * LANES), 1)
    dist = jnp.where(col < LANES, LANES + t - col, t - (col - LANES))
    bucket = _bucket(jnp.maximum(dist, 0))
    for h in range(N_HEADS):
        bs_ref[h * rows:(h + 1) * rows, :] = jnp.where(dist >= 0, lookup(bucket, h), NEG_BIG)
        b31s_ref[h * rows:(h + 1) * rows, :] = jnp.full((rows, LANES), rb_ref[N_BUCKETS - 1, h], F32)


def _setup(rel_bias, lq1, lk1, lq2, lk2):
    vspec = pl.BlockSpec(memory_space=pltpu.VMEM)
    return pl.pallas_call(
        _setup_kernel,
        out_shape=(jax.ShapeDtypeStruct((8, LANES), F32),
                   jax.ShapeDtypeStruct((N_HEADS, 2, ATTN_TK, ATTN_TQ), F32),
                   jax.ShapeDtypeStruct((N_HEADS * 16, 2 * LANES), F32),
                   jax.ShapeDtypeStruct((N_HEADS * 16, LANES), F32)),
        in_specs=[pl.BlockSpec(memory_space=pltpu.SMEM), vspec, vspec, vspec, vspec],
        out_specs=(vspec, vspec, vspec, vspec),
        name="setup",
    )(rel_bias, lq1, lk1, lq2, lk2)


def _mod_kernel(c_ref, w_ref, b_ref, o_ref):
    c = c_ref[...]
    s = (c * jax.nn.sigmoid(c)).astype(BF16)
    o_ref[...] = jnp.dot(s, w_ref[...].astype(BF16), preferred_element_type=F32) + b_ref[...]


def _modulation(c_all, w_ada, b_ada):
    n = c_all.shape[0]
    tn = 1024
    return pl.pallas_call(
        _mod_kernel,
        out_shape=jax.ShapeDtypeStruct((n, N_MOD * D_MODEL), F32),
        grid=(N_MOD * D_MODEL // tn,),
        in_specs=[pl.BlockSpec((n, D_MODEL), lambda j: (0, 0)),
                  pl.BlockSpec((D_MODEL, tn), lambda j: (0, j)),
                  pl.BlockSpec((1, tn), lambda j: (0, j))],
        out_specs=pl.BlockSpec((n, tn), lambda j: (0, j)),
        compiler_params=pltpu.CompilerParams(dimension_semantics=("arbitrary",),
                                             vmem_limit_bytes=VMEM_LIMIT),
        name="modulation",
    )(c_all, w_ada, b_ada.reshape(1, -1))


def _inproj_kernel(*refs, carry, pos0, bt, tt, share=None):
    sample_seq = None
    if carry and share is not None:
        pt_ref, refs = refs[0], refs[1:]
        share_in, os_ref = refs[7:7 + N_SHARE_IN], refs[7 + N_SHARE_IN + 7]
        share_scratch = refs[-N_SHARE_SCRATCH:]
        refs = refs[:7] + refs[7 + N_SHARE_IN:7 + N_SHARE_IN + 7] + refs[7 + N_SHARE_IN + 8:-N_SHARE_SCRATCH]
        step = pl.program_id(0) * pl.num_programs(1) + pl.program_id(1)
        n_steps = pl.num_programs(0) * pl.num_programs(1)
        sample_seq = _sample_share(step, n_steps, pt_ref, *share_in, os_ref, *share_scratch, **share)
    if carry:
        (x_ref, mod_ref, g_ref, w_in_ref, w_pool_ref, ps_ref, wkt_ref,
         q_ref, k_ref, v_ref, kb_ref, vb_ref, yp_ref, st_ref, ext_ref, carry_ref) = refs
    else:
        (x_ref, mod_ref, g_ref, w_in_ref, w_pool_ref, ps_ref, prefix_ref,
         q_ref, k_ref, v_ref, yp_ref, st_ref, ext_ref) = refs
    m = bt * tt
    x = x_ref[...]
    h = _rms(x, g_ref[...]) * (1.0 + mod_ref[:, 1:2, :]) + mod_ref[:, 0:1, :]
    h2 = h.reshape(m, D_MODEL).astype(BF16)

    def proj(off, width):
        return jnp.dot(h2, w_in_ref[:, off:off + width], preferred_element_type=F32)

    u3 = proj(0, POOL_WIDTH).reshape(bt, tt, POOL_WIDTH)
    if carry:
        @pl.when(pl.program_id(1) == 0)
        def _():
            carry_ref[...] = jnp.zeros_like(carry_ref)
        ext_ref[:, 0:POOL_HALO, :] = carry_ref[...]
    else:
        ext_ref[:, 0:POOL_HALO, :] = prefix_ref[...]
    ext_ref[:, POOL_HALO:, :] = u3
    if carry:
        carry_ref[...] = u3[:, tt - POOL_HALO:, :]
        t_base = pl.program_id(1) * tt
    else:
        t_base = 0
    st_ref[...] = ext_ref[:, tt:tt + POOL_HALO, :]

    if sample_seq is not None:
        for r in range(os_ref.shape[0]):
            sample_seq(r)

    zq = proj(POOL_WIDTH, QK_WIDTH)
    q_scale = HEAD_DIM ** -0.5 * (LOG2E if carry else 1.0)
    q_ref[...] = (zq * q_scale).astype(BF16).reshape(bt, tt, QK_WIDTH)
    zv = proj(POOL_WIDTH + 2 * QK_WIDTH, ATTN_WIDTH)
    if carry:
        vb_ref[...] = zv.astype(BF16).reshape(bt, tt, ATTN_WIDTH)
        for hd in range(N_HEADS):
            v_ref[0, pl.ds(hd, tt, stride=N_HEADS), :] = zv[:, hd * V_HEAD_DIM:(hd + 1) * V_HEAD_DIM]
    else:
        v_ref[...] = zv.reshape(bt, tt, ATTN_WIDTH)

    pos = pos0 + t_base + lax.broadcasted_iota(jnp.int32, (bt, tt, POOL_GROUP_DIM), 1)
    for g, w in enumerate(POOL_WINDOWS):
        sl = slice(g * POOL_GROUP_DIM, (g + 1) * POOL_GROUP_DIM)
        tok = ext_ref[:, POOL_HALO:, sl]
        acc = tok
        for j in range(1, w):
            acc = acc + ext_ref[:, POOL_HALO - j:POOL_HALO - j + tt, sl]
        cnt = jnp.minimum(pos + 1, w).astype(F32)
        d = (acc / cnt - tok).reshape(m, POOL_GROUP_DIM).astype(BF16)
        y = jnp.dot(d, w_pool_ref[g], preferred_element_type=F32) * ps_ref[:, sl]
        yp_ref[:, :, sl] = y.astype(BF16).reshape(bt, tt, POOL_GROUP_DIM)

    if carry:
        zkt = lax.dot_general(wkt_ref[...], h2, (((1,), (1,)), ((), ())), preferred_element_type=F32)
        k_ref[0] = zkt
        kb_ref[...] = zkt.T.astype(BF16).reshape(bt, tt, QK_WIDTH)
    else:
        k_ref[...] = proj(POOL_WIDTH + QK_WIDTH, QK_WIDTH).reshape(bt, tt, QK_WIDTH)


def _inproj(x, mod, g_pre, w_in, w_kt, w_pool, pool_scale, prefix, *, pos0, page_table=None, sample=None,
            seq0=0, n_seq=0):
    nb, t, _ = x.shape
    carry = prefix is None
    if carry:
        bt, tt = 1, ROW_TILE
        grid = (nb, t // tt)
        tile = lambda w: pl.BlockSpec((bt, tt, w), lambda b, i, *_: (b, i, 0))
        per_b = lambda r, w: pl.BlockSpec((bt, r, w), lambda b, i, *_: (b, 0, 0))
        sem = ("arbitrary", "arbitrary")
    else:
        bt, tt = ROW_TILE // t, t
        grid = (nb // bt,)
        tile = lambda w: pl.BlockSpec((bt, tt, w), lambda i: (i, 0, 0))
        per_b = lambda r, w: pl.BlockSpec((bt, r, w), lambda i: (i, 0, 0))
        sem = ("arbitrary",)
    in_specs = [tile(D_MODEL), per_b(N_MOD, D_MODEL), _const_spec((1, 1, D_MODEL)),
                _const_spec((D_MODEL, PROJ_WIDTH)),
                _const_spec((len(POOL_WINDOWS), POOL_GROUP_DIM, POOL_GROUP_DIM)),
                _const_spec((1, POOL_WIDTH))]
    args = [x, mod, g_pre.reshape(1, 1, D_MODEL), w_in, w_pool, pool_scale.reshape(1, POOL_WIDTH)]
    scratch = [pltpu.VMEM((bt, POOL_HALO + tt, POOL_WIDTH), F32)]
    st_shape = jax.ShapeDtypeStruct((nb, POOL_HALO, POOL_WIDTH), F32)
    act = lambda w, dt: jax.ShapeDtypeStruct((nb, t, w), dt)
    if carry:
        scratch.append(pltpu.VMEM((1, POOL_HALO, POOL_WIDTH), F32))
        in_specs.append(_const_spec((QK_WIDTH, D_MODEL)))
        args.append(w_kt)
        out_shape = (act(QK_WIDTH, BF16),
                     jax.ShapeDtypeStruct((nb, QK_WIDTH, t), F32),
                     jax.ShapeDtypeStruct((nb, t * N_HEADS, V_HEAD_DIM), F32), act(QK_WIDTH, BF16),
                     act(ATTN_WIDTH, BF16), act(POOL_WIDTH, BF16), st_shape)
        out_specs = (tile(QK_WIDTH),
                     pl.BlockSpec((1, QK_WIDTH, tt), lambda b, i, *_: (b, 0, i)),
                     pl.BlockSpec((1, tt * N_HEADS, V_HEAD_DIM), lambda b, i, *_: (b, i, 0)), tile(QK_WIDTH),
                     tile(ATTN_WIDTH), tile(POOL_WIDTH), per_b(POOL_HALO, POOL_WIDTH))
        if sample is not None:
            steps_per_b = t // tt
            sh_in, sh_args, sh_out, sh_out_spec, sh_scratch, sh_static = _share_plumbing(
                sample, lambda b, i: b * steps_per_b + i, seq0, n_seq, nb * steps_per_b)
            return pl.pallas_call(
                functools.partial(_inproj_kernel, carry=carry, pos0=pos0, bt=bt, tt=tt, share=sh_static),
                out_shape=out_shape + (sh_out,),
                grid_spec=pltpu.PrefetchScalarGridSpec(
                    num_scalar_prefetch=1, grid=grid, in_specs=in_specs + sh_in,
                    out_specs=out_specs + (sh_out_spec,), scratch_shapes=scratch + sh_scratch),
                compiler_params=pltpu.CompilerParams(dimension_semantics=sem, vmem_limit_bytes=VMEM_LIMIT),
                name="inproj_prompt",
            )(page_table, *args, *sh_args)
    else:
        in_specs.append(per_b(POOL_HALO, POOL_WIDTH))
        args.append(prefix)
        out_shape = (act(QK_WIDTH, BF16), act(QK_WIDTH, F32), act(ATTN_WIDTH, F32),
                     act(POOL_WIDTH, BF16), st_shape)
        out_specs = (tile(QK_WIDTH), tile(QK_WIDTH), tile(ATTN_WIDTH), tile(POOL_WIDTH),
                     per_b(POOL_HALO, POOL_WIDTH))
    return pl.pallas_call(
        functools.partial(_inproj_kernel, carry=carry, pos0=pos0, bt=bt, tt=tt),
        out_shape=out_shape, grid=grid, in_specs=in_specs, out_specs=out_specs,
        scratch_shapes=scratch,
        compiler_params=pltpu.CompilerParams(dimension_semantics=sem, vmem_limit_bytes=VMEM_LIMIT),
        name="inproj_prompt" if carry else "inproj_sample",
    )(*args)


def _attn_unit_order(nt):
    left = {i: i + 1 for i in range(nt)}
    last = {i: -ATTN_CHAIN_GAP for i in range(nt)}
    order = []
    while any(left.values()):
        n = len(order)
        cands = [i for i in range(nt) if left[i]]
        spaced = [i for i in cands if n - last[i] >= ATTN_CHAIN_GAP]
        i = max(spaced, key=lambda t: (left[t], t)) if spaced else max(cands, key=lambda t: n - last[t])
        order.append((i, i + 1 - left[i]))
        left[i] -= 1
        last[i] = n
    return order


def _attn_prompt_head(q_ref, kb_ref, vb_ref, bp_ref, lam_ref, gh_ref, o_ref,
                      w_ref, vt_ref, m_ref, acc_ref, after_pair):
    tq, tk = ATTN_TQ, ATTN_TK
    nt = q_ref.shape[1] // tq
    zero = jnp.zeros((HEAD_DIM, tq), BF16)
    for i in range(nt):
        qt = q_ref[0, i * tq:(i + 1) * tq, :].astype(F32).T.astype(BF16)
        w_ref[i] = jnp.concatenate([jnp.concatenate([qt[:HEAD_DIM], zero], axis=1),
                                    jnp.concatenate([zero, qt[HEAD_DIM:]], axis=1)], axis=0)
    lam = lam_ref[0:1, 0:1]

    def scores(i, j):
        return jnp.dot(kb_ref[0, j * tk:(j + 1) * tk, :], w_ref[i], preferred_element_type=F32)

    for j in range(nt):
        vt_ref[j] = jnp.concatenate([vb_ref[0, j * tk:(j + 1) * tk, :].astype(F32).T,
                                     jnp.ones((SUM_ROWS, tk), F32)], axis=0).astype(BF16)

    units = _attn_unit_order(nt)
    pending = [scores(*u) for u in units[:ATTN_LOOKAHEAD]]
    for n, (i, j) in enumerate(units):
        st = pending.pop(0)
        if n + ATTN_LOOKAHEAD < len(units):
            pending.append(scores(*units[n + ATTN_LOOKAHEAD]))
        vt = vt_ref[j]
        if i - j < 2:
            b = bp_ref[0, i - j]
            st = jnp.concatenate([b, b], axis=1) + st
        mx = jnp.max(st, axis=0, keepdims=True)
        if j == 0:
            p = jnp.exp2(st - mx)
            acc_ref[i] = jnp.dot(vt, p.astype(BF16), preferred_element_type=F32)
            m_ref[i] = mx
        else:
            m_old = m_ref[i]
            m_new = jnp.maximum(m_old, mx)
            alpha = jnp.exp2(m_old - m_new)
            p = jnp.exp2(st - m_new)
            acc_ref[i] = alpha * acc_ref[i] + jnp.dot(vt, p.astype(BF16), preferred_element_type=F32)
            m_ref[i] = m_new
        if i == j:
            acc = acc_ref[j]
            on = acc[:V_HEAD_DIM] / acc[V_HEAD_DIM:V_HEAD_DIM + 1]
            ot = on[:, :tq] - lam * on[:, tq:]
            ot = ot * lax.rsqrt(jnp.mean(ot * ot, axis=0, keepdims=True) + EPS)
            o_ref[0, j * tq:(j + 1) * tq, :] = (ot.T * gh_ref[...] * (1.0 - LAM_INIT)).astype(BF16)
        after_pair(n, len(units))


N_SHARE_IN, N_SHARE_SCRATCH = 9, 4


def _attn_kernel(pt_ref, q_ref, kb_ref, vb_ref, bp_ref, lam_ref, gh_ref, *rest, share):
    share_in = rest[:N_SHARE_IN]
    o_ref, os_ref, w_ref, vt_ref, m_ref, acc_ref = rest[N_SHARE_IN:N_SHARE_IN + 6]
    share_scratch = rest[N_SHARE_IN + 6:]
    step = pl.program_id(0) * pl.num_programs(1) + pl.program_id(1)
    n_steps = pl.num_programs(0) * pl.num_programs(1)
    sample_seq = _sample_share(step, n_steps, pt_ref, *share_in, os_ref, *share_scratch, **share)
    n_seq = os_ref.shape[0]

    def after_pair(n, total):
        for r in range(n_seq):
            if n == (2 * r + 1) * total // (2 * n_seq):
                sample_seq(r)

    _attn_prompt_head(q_ref, kb_ref, vb_ref, bp_ref, lam_ref, gh_ref, o_ref,
                      w_ref, vt_ref, m_ref, acc_ref, after_pair)


def _share_plumbing(sample, step_index, seq0, n_seq, n_steps):
    n_pages, q_s, k_s, v_s, bias_s, b31s, lam, g_head, ck, cv = sample
    tnew = q_s.shape[1]
    page = ck.shape[2]
    past = n_pages * page
    assert seq0 % n_seq == 0
    seqs = lambda w: pl.BlockSpec((n_seq, tnew, w), lambda *g: (seq0 // n_seq + step_index(*g[:-1]), 0, 0))
    out_spec = pl.BlockSpec((n_seq, tnew, ATTN_WIDTH), lambda *g: (step_index(*g[:-1]), 0, 0))
    const = lambda a: pl.BlockSpec(a.shape, lambda *g: (0,) * a.ndim)
    hbm = pl.BlockSpec(memory_space=pl.ANY)
    gh = g_head.reshape(1, V_HEAD_DIM)
    in_specs = [seqs(QK_WIDTH), seqs(QK_WIDTH), seqs(ATTN_WIDTH), const(bias_s), const(b31s), const(lam),
                const(gh), hbm, hbm]
    args = [q_s, k_s, v_s, bias_s, b31s, lam, gh, ck, cv]
    out_shape = jax.ShapeDtypeStruct((n_steps * n_seq, tnew, ATTN_WIDTH), BF16)
    scratch = [pltpu.VMEM((SAMPLE_SLOTS, n_pages, QK_WIDTH, page), F32),
               pltpu.VMEM((SAMPLE_SLOTS, n_pages, page * N_HEADS, V_HEAD_DIM), F32),
               pltpu.VMEM((past + LANES, ATTN_WIDTH), BF16),
               pltpu.SemaphoreType.DMA((SAMPLE_SLOTS,))]
    assert len(in_specs) == N_SHARE_IN and len(scratch) == N_SHARE_SCRATCH
    return in_specs, args, out_shape, out_spec, scratch, dict(n_pages=n_pages, page=page, seq0=seq0)


def _sample_share(step, n_steps, pt_ref, qs_ref, kn_ref, vn_ref, bs_ref, b31s_ref, lam_ref, gh_ref,
                  ck_hbm, cv_hbm, os_ref, kraw_ref, vraw_ref, vbf_ref, sem_ref,
                  *, n_pages, page, seq0):
    n_seq = qs_ref.shape[0]
    first = seq0 + step * n_seq

    def page_copies(seq, slot):
        cps = []
        for j in range(n_pages):
            pg = pt_ref[seq, j]
            cps.append(pltpu.make_async_copy(ck_hbm.at[pg], kraw_ref.at[slot, j], sem_ref.at[slot]))
            cps.append(pltpu.make_async_copy(cv_hbm.at[pg], vraw_ref.at[slot, j], sem_ref.at[slot]))
        return cps

    def start(seq):
        for cp in page_copies(seq, seq & (SAMPLE_SLOTS - 1)):
            cp.start()

    def wait(seq):
        for cp in page_copies(seq, seq & (SAMPLE_SLOTS - 1)):
            cp.wait()

    @pl.when(step == 0)
    def _():
        start(first)

    def sample_seq(r):
        seq = first + r
        if r + 1 < n_seq:
            start(seq + 1)
        else:
            @pl.when(step + 1 < n_steps)
            def _():
                start(seq + 1)
        wait(seq)
        _attn_sample_one(r, seq & (SAMPLE_SLOTS - 1), qs_ref, kn_ref, vn_ref, bs_ref, b31s_ref, lam_ref, gh_ref,
                         os_ref, kraw_ref, vraw_ref, vbf_ref, n_pages=n_pages, page=page)

    return sample_seq


def _attention(q, kb, vb, bias_p, lam, g_head, page_table, sample, *, seq0, n_seq):
    nb, t, _ = q.shape
    tq = ATTN_TQ
    nt = t // tq
    hw = V_HEAD_DIM
    n_steps = nb * N_HEADS
    sh_in, sh_args, sh_out, sh_out_spec, sh_scratch, sh_static = _share_plumbing(
        sample, lambda b, h: b * N_HEADS + h, seq0, n_seq, n_steps)
    head = pl.BlockSpec((1, t, hw), lambda b, h, pt: (b, 0, h))
    const = lambda shape: pl.BlockSpec(shape, lambda b, h, pt: (0,) * len(shape))
    return pl.pallas_call(
        functools.partial(_attn_kernel, share=sh_static),
        out_shape=(jax.ShapeDtypeStruct((nb, t, ATTN_WIDTH), BF16), sh_out),
        grid_spec=pltpu.PrefetchScalarGridSpec(
            num_scalar_prefetch=1, grid=(nb, N_HEADS),
            in_specs=[head, head, head,
                      pl.BlockSpec((1, 2, ATTN_TK, ATTN_TQ), lambda b, h, pt: (h, 0, 0, 0)),
                      const((8, LANES)), const((1, hw))] + sh_in,
            out_specs=(head, sh_out_spec),
            scratch_shapes=[pltpu.VMEM((nt, 2 * HEAD_DIM, 2 * tq), BF16),
                            pltpu.VMEM((nt, hw + SUM_ROWS, ATTN_TK), BF16),
                            pltpu.VMEM((nt, 1, 2 * tq), F32),
                            pltpu.VMEM((nt, hw + SUM_ROWS, 2 * tq), F32)] + sh_scratch),
        compiler_params=pltpu.CompilerParams(
            dimension_semantics=("arbitrary", "arbitrary"), vmem_limit_bytes=VMEM_LIMIT),
        name="attention",
    )(page_table, q, kb, vb, bias_p, lam, g_head.reshape(1, hw), *sh_args)


def _attn_sample_one(row, slot, q_ref, kn_ref, vn_ref, bs_ref, b31s_ref, lam_ref, gh_ref, o_ref,
                     kraw_ref, vraw_ref, vbf_ref, *, n_pages, page):
    past = n_pages * page
    tnew = q_ref.shape[1]
    n_maps = 2 * N_HEADS

    q = q_ref[row].astype(F32)
    qt = jnp.concatenate([q] * n_maps, axis=0)
    rg = lax.broadcasted_iota(jnp.int32, qt.shape, 0) // tnew
    cg = lax.broadcasted_iota(jnp.int32, qt.shape, 1) // HEAD_DIM
    qbd = jnp.where(rg == cg, qt, 0.0)

    for j in range(n_pages):
        for h in range(N_HEADS):
            vbf_ref[j * page:(j + 1) * page, h * V_HEAD_DIM:(h + 1) * V_HEAD_DIM] = (
                vraw_ref[slot, j, pl.ds(h, page, stride=N_HEADS), :].astype(BF16))
    pad = jnp.zeros((LANES - tnew, QK_WIDTH), F32)
    k_new = jnp.concatenate([kn_ref[row], pad], axis=0)
    vbf_ref[past:, :] = jnp.concatenate([vn_ref[row], pad], axis=0).astype(BF16)

    k_past = jnp.concatenate([kraw_ref[slot, j] for j in range(n_pages)], axis=1)
    s_past = jnp.dot(qbd, k_past, preferred_element_type=F32)
    s_new = lax.dot_general(qbd, k_new, (((1,), (1,)), ((), ())), preferred_element_type=F32)
    near = past - page
    s = jnp.concatenate([s_past[:, :near] + b31s_ref[:, 0:1],
                         jnp.concatenate([s_past[:, near:], s_new], axis=1) + bs_ref[...]], axis=1)
    mx = jnp.max(s, axis=-1, keepdims=True)
    p = jnp.exp(s - mx)
    pn = p / jnp.sum(p, axis=-1, keepdims=True)
    lam = lam_ref[0:1, 0:1]
    a = jnp.concatenate(
        [pn[2 * h * tnew:(2 * h + 1) * tnew] - lam * pn[(2 * h + 1) * tnew:(2 * h + 2) * tnew]
         for h in range(N_HEADS)], axis=0).astype(BF16)
    r = jnp.dot(a, vbf_ref[...], preferred_element_type=F32)
    outs = []
    for h in range(N_HEADS):
        o = r[h * tnew:(h + 1) * tnew, h * V_HEAD_DIM:(h + 1) * V_HEAD_DIM]
        outs.append(_rms(o, gh_ref[...]) * (1.0 - LAM_INIT))
    o_ref[row] = jnp.concatenate(outs, axis=1).astype(BF16)


def _ffn_kernel(*refs, carry, ns, bt, tt):
    if carry:
        (x_ref, yp_ref, at_ref, mod_ref, gpm_ref, gpf_ref, gpo_ref, w_out_ref, w_up_ref, cw_ref, cb_ref,
         w_down_ref, y_ref, cs_ref, *buf_refs, carry_ref) = refs
    else:
        (x_ref, yp_ref, at_ref, mod_ref, gpm_ref, gpf_ref, gpo_ref, w_out_ref, w_up_ref, cw_ref, cb_ref,
         w_down_ref, prefix_ref, y_ref, cs_ref, *buf_refs) = refs
    m = bt * tt
    n_chunks = D_FF // FF_CHUNK
    streams = range(ns)

    def rows(s):
        return (slice(None), slice(s * tt, (s + 1) * tt)) if carry else (slice(s * bt, (s + 1) * bt),)

    def mod_row(s, i):
        return mod_ref[:, i:i + 1, :] if carry else mod_ref[s * bt:(s + 1) * bt, i:i + 1, :]

    def buf(s, slot):
        return buf_refs[4 * s + slot]

    if carry:
        @pl.when(pl.program_id(1) == 0)
        def _():
            carry_ref[...] = jnp.zeros_like(carry_ref)
    halo_rows = CONV_HALO if carry else prefix_ref.shape[1]

    x1, h2 = {}, {}
    for s in streams:
        mix_in = jnp.concatenate([yp_ref[rows(s)], at_ref[rows(s)]], axis=-1).reshape(m, D_MODEL)
        mix = jnp.dot(mix_in, w_out_ref[...], preferred_element_type=F32).reshape(bt, tt, D_MODEL)
        x1[s] = x_ref[rows(s)] + mod_row(s, 2) * _rms(mix, gpm_ref[...])
        h = _rms(x1[s], gpf_ref[...]) * (1.0 + mod_row(s, 4)) + mod_row(s, 3)
        h2[s] = h.reshape(m, D_MODEL).astype(BF16)

    def up_part(s, slot, off):
        sl = slice(off, off + FF_CHUNK)
        u3 = jnp.dot(h2[s], w_up_ref[:, sl], preferred_element_type=F32).reshape(bt, tt, FF_CHUNK)
        if not carry:
            halo = prefix_ref[s * bt:(s + 1) * bt, :, sl]
        elif s == 0:
            halo = carry_ref[:, :, sl]
        else:
            halo = buf(s - 1, slot)[:, tt:tt + CONV_HALO, :]
        buf(s, slot)[:, CONV_HALO - halo_rows:CONV_HALO, :] = halo
        buf(s, slot)[:, CONV_HALO:, :] = u3
        if not carry:
            cs_ref[s * bt:(s + 1) * bt, :, sl] = u3[:, tt - halo_rows:, :]
        elif s == ns - 1:
            cs_ref[:, :, sl] = u3[:, tt - halo_rows:, :]

    def conv_part(s, slot, off):
        sl = slice(off, off + FF_CHUNK)
        b = buf(s, slot)
        y = (cb_ref[:, :, sl]
             + cw_ref[0:1, :, sl] * b[:, CONV_HALO - 2:CONV_HALO - 2 + tt, :]
             + cw_ref[1:2, :, sl] * b[:, CONV_HALO - 1:CONV_HALO - 1 + tt, :]
             + cw_ref[2:3, :, sl] * b[:, CONV_HALO:, :])
        return y.reshape(m, FF_CHUNK)

    def up_chunk(s, c):
        up_part(s, 2 * (c % 2), c * FF_CHUNK)
        up_part(s, 2 * (c % 2) + 1, D_FF + c * FF_CHUNK)

    f = {s: jnp.zeros((m, D_MODEL), F32) for s in streams}
    for s in streams:
        up_chunk(s, 0)
    for c in range(n_chunks):
        if c + 1 < n_chunks:
            for s in streams:
                up_chunk(s, c + 1)
        for s in streams:
            gate = conv_part(s, 2 * (c % 2), c * FF_CHUNK)
            val = conv_part(s, 2 * (c % 2) + 1, D_FF + c * FF_CHUNK)
            act = (_gelu_tanh(gate) * val).astype(BF16)
            f[s] = f[s] + jnp.dot(act, w_down_ref[c * FF_CHUNK:(c + 1) * FF_CHUNK, :],
                                  preferred_element_type=F32)
    if carry:
        carry_ref[...] = cs_ref[...]
    for s in streams:
        y_ref[rows(s)] = x1[s] + mod_row(s, 5) * _rms(f[s].reshape(bt, tt, D_MODEL), gpo_ref[...])


def _ffn(x, yp, attn, mod, g_post_mix, g_pre_ffn, g_post_ffn, w_out, w_up, conv_w, conv_b, w_down, prefix):
    nb, t, _ = x.shape
    carry = prefix is None
    ns = FFN_STREAMS
    if carry:
        bt, tt = 1, ROW_TILE
        grid = (nb, t // (ns * tt))
        tile = lambda w: pl.BlockSpec((1, ns * tt, w), lambda b, i: (b, i, 0))
        per_b = lambda r, w: pl.BlockSpec((1, r, w), lambda b, i: (b, 0, 0))
        sem = ("arbitrary", "arbitrary")
    else:
        bt, tt = ROW_TILE // t, t
        grid = (nb // (ns * bt),)
        tile = lambda w: pl.BlockSpec((ns * bt, tt, w), lambda i: (i, 0, 0))
        per_b = lambda r, w: pl.BlockSpec((ns * bt, r, w), lambda i: (i, 0, 0))
        sem = ("arbitrary",)
    gspec = _const_spec((1, 1, D_MODEL))
    in_specs = [tile(D_MODEL), tile(POOL_WIDTH), tile(ATTN_WIDTH), per_b(N_MOD, D_MODEL),
                gspec, gspec, gspec,
                _const_spec((D_MODEL, D_MODEL)), _const_spec((D_MODEL, 2 * D_FF)),
                _const_spec((CONV_WIDTH, 1, 2 * D_FF)), _const_spec((1, 1, 2 * D_FF)),
                _const_spec((D_FF, D_MODEL))]
    args = [x, yp, attn, mod, g_post_mix.reshape(1, 1, -1), g_pre_ffn.reshape(1, 1, -1),
            g_post_ffn.reshape(1, 1, -1), w_out, w_up, conv_w.reshape(CONV_WIDTH, 1, -1),
            conv_b.reshape(1, 1, -1), w_down]
    scratch = [pltpu.VMEM((bt, CONV_HALO + tt, FF_CHUNK), F32) for _ in range(4 * ns)]
    if carry:
        halo_rows = CONV_HALO
        scratch.append(pltpu.VMEM((1, halo_rows, 2 * D_FF), F32))
    else:
        halo_rows = prefix.shape[1]
        in_specs.append(per_b(halo_rows, 2 * D_FF))
        args.append(prefix)
    return pl.pallas_call(
        functools.partial(_ffn_kernel, carry=carry, ns=ns, bt=bt, tt=tt),
        out_shape=(jax.ShapeDtypeStruct((nb, t, D_MODEL), F32),
                   jax.ShapeDtypeStruct((nb, halo_rows, 2 * D_FF), F32)),
        grid=grid, in_specs=in_specs,
        out_specs=(tile(D_MODEL), per_b(halo_rows, 2 * D_FF)),
        scratch_shapes=scratch,
        compiler_params=pltpu.CompilerParams(dimension_semantics=sem, vmem_limit_bytes=VMEM_LIMIT),
        name="ffn_prompt" if carry else "ffn_sample",
    )(*args)


def kernel(x_prompt, x_sample, c_prompt, c_sample, cache_k, cache_v, page_table, state_pool, state_conv,
           w_ada, b_ada, g_pre_mix, g_post_mix, g_pre_ffn, g_post_ffn, w_in, w_out, w_pool, pool_scale,
           lam_q1, lam_k1, lam_q2, lam_k2, g_head, rel_bias, w_up, conv_w, conv_b, w_down):
    nbp, seq, _ = x_prompt.shape
    nbs, dec_seq, _ = x_sample.shape
    past_len = page_table.shape[1] * cache_k.shape[2]
    l = 0

    lam, bias_p, bias_s, b31s = _setup(rel_bias, lam_q1[l:l + 1], lam_k1[l:l + 1],
                                             lam_q2[l:l + 1], lam_k2[l:l + 1])
    mod = _modulation(jnp.concatenate([c_prompt, c_sample], axis=0), w_ada[l], b_ada[l])
    mod = mod.reshape(nbp + nbs, N_MOD, D_MODEL)
    mod_p, mod_s = mod[:nbp], mod[nbp:]

    w_in_b = w_in[l].astype(BF16)
    w_kt_b = w_in[l][:, POOL_WIDTH + QK_WIDTH:POOL_WIDTH + 2 * QK_WIDTH].T.astype(BF16)
    w_pool_b = w_pool[l].astype(BF16)
    w_out_b = w_out[l].astype(BF16)
    w_up_b = w_up[l].astype(BF16)
    w_down_b = w_down[l].astype(BF16)

    pool_prefix = jnp.pad(state_pool[l], ((0, 0), (POOL_HALO - POOL_STATE, 0), (0, 0)))

    q_s, k_s, v_s, yp_s, pstate_s = _inproj(
        x_sample, mod_s, g_pre_mix[l], w_in_b, w_kt_b, w_pool_b, pool_scale[l], pool_prefix, pos0=past_len)

    n_phys, page = cache_k.shape[1], cache_k.shape[2]
    ck = jnp.transpose(cache_k[l], (0, 2, 3, 4, 1)).reshape(n_phys, QK_WIDTH, page)
    cv = cache_v[l].reshape(n_phys, page * N_HEADS, V_HEAD_DIM)
    sample = (page_table.shape[1], q_s, k_s, v_s, bias_s, b31s, lam, g_head[l], ck, cv)

    n_inproj_steps = nbp * (seq // ROW_TILE)
    n_attn_steps = nbp * N_HEADS
    seqs_in_attn = nbs - n_inproj_steps
    assert seqs_in_attn % (n_attn_steps * SAMPLE_SLOTS) == 0
    q_p, kt_p, v_p, kb_p, vb_p, yp_p, pstate_p, attn_s2 = _inproj(
        x_prompt, mod_p, g_pre_mix[l], w_in_b, w_kt_b, w_pool_b, pool_scale[l], None, pos0=0,
        page_table=page_table, sample=sample, seq0=seqs_in_attn, n_seq=1)
    attn_p, attn_s1 = _attention(q_p, kb_p, vb_p, bias_p, lam, g_head[l], page_table, sample,
                                 seq0=0, n_seq=seqs_in_attn // n_attn_steps)
    attn_s = jnp.concatenate([attn_s1, attn_s2], axis=0)
    k_p = jnp.transpose(kt_p.reshape(nbp, N_HEADS, 2, HEAD_DIM, seq), (0, 4, 1, 2, 3))

    outs = []
    for (x, md, yp, attn, k, v, pstate, cprefix) in (
            (x_prompt, mod_p, yp_p, attn_p, k_p, v_p, pstate_p, None),
            (x_sample, mod_s, yp_s, attn_s, k_s, v_s, pstate_s, state_conv[l])):
        nb, t = x.shape[0], x.shape[1]
        y, cstate = _ffn(x, yp, attn, md, g_post_mix[l], g_pre_ffn[l], g_post_ffn[l],
                         w_out_b, w_up_b, conv_w[l], conv_b[l], w_down_b, cprefix)
        outs.append((y,
                     k.reshape(1, nb, t, N_HEADS, 2, HEAD_DIM),
                     v.reshape(1, nb, t, N_HEADS, V_HEAD_DIM),
                     pstate[None, :, POOL_HALO - POOL_STATE:, :],
                     cstate[None, :, cstate.shape[1] - (CONV_WIDTH - 1):, :]))
    (yp_, kp, vp, pp, cp), (ys_, ks, vs, ps, cs) = outs
    return (yp_, ys_, kp, vp, pp, cp, ks, vs, ps, cs)
```

```python
import functools
import math

import jax
import jax.numpy as jnp
from jax import lax
from jax.experimental import pallas as pl
from jax.experimental.pallas import tpu as pltpu

F32 = jnp.float32
BF16 = jnp.bfloat16

D_MODEL = 1024
POOL_WIDTH = 512
POOL_WINDOWS = (2, 4, 8, 16)
POOL_GROUP_DIM = 128
POOL_STATE = 15
POOL_HALO = 16
N_HEADS = 4
HEAD_DIM = 64
V_HEAD_DIM = 128
QK_WIDTH = 512
ATTN_WIDTH = 512
PROJ_WIDTH = 2048
D_FF = 2816
CONV_WIDTH = 3
CONV_HALO = 8
N_BUCKETS = 32
MAX_EXACT = N_BUCKETS // 2
MAX_DISTANCE = 128
N_MOD = 6
EPS = 1e-6
LAM_INIT = 0.8 - 0.6 * math.exp(-0.3 * 0)
NEG_BIG = -1e30
LOG2E = math.log2(math.e)
SUM_ROWS = 16

LANES = 128
MXU_TILE = 256
VMEM_LIMIT = 56 * 1024 * 1024

ATTN_TQ = 256
ATTN_TK = 256
ATTN_CHAIN_GAP = 4
ATTN_LOOKAHEAD = 2
ROW_TILE = 256
FF_CHUNK = 256
SAMPLE_SLOTS = 2
FFN_STREAMS = 2


def _const_spec(shape):
    nd = len(shape)
    return pl.BlockSpec(shape, lambda *_: (0,) * nd, pipeline_mode=pl.Buffered(1))


def _rms(x, g):
    return x * lax.rsqrt(jnp.mean(x * x, axis=-1, keepdims=True) + EPS) * g


def _gelu_tanh(x):
    c0 = -2.0 * math.sqrt(2.0 / math.pi) * LOG2E
    return x / (1.0 + jnp.exp2(x * (c0 + (c0 * 0.044715) * (x * x))))


def _bucket(dist):
    d = jnp.maximum(dist, 1).astype(F32)
    large = MAX_EXACT + jnp.floor(jnp.log(d / MAX_EXACT) / math.log(MAX_DISTANCE / MAX_EXACT)
                                  * (N_BUCKETS - MAX_EXACT)).astype(jnp.int32)
    large = jnp.minimum(large, N_BUCKETS - 1)
    return jnp.where(dist < MAX_EXACT, dist, large)


def _setup_kernel(rb_ref, lq1_ref, lk1_ref, lq2_ref, lk2_ref,
                  lam_ref, bp_ref, bs_ref, b31s_ref):
    lam = (jnp.exp(jnp.sum(lq1_ref[...] * lk1_ref[...], axis=-1, keepdims=True))
           - jnp.exp(jnp.sum(lq2_ref[...] * lk2_ref[...], axis=-1, keepdims=True)) + LAM_INIT)
    lam_ref[...] = jnp.broadcast_to(lam, lam_ref.shape)

    def lookup(bucket, h):
        out = jnp.full(bucket.shape, rb_ref[N_BUCKETS - 1, h], F32)
        for b in range(N_BUCKETS - 1):
            out = jnp.where(bucket == b, rb_ref[b, h], out)
        return out

    kk = lax.broadcasted_iota(jnp.int32, (ATTN_TK, ATTN_TQ), 0)
    r = lax.broadcasted_iota(jnp.int32, (ATTN_TK, ATTN_TQ), 1)
    for typ in range(2):
        dist = typ * ATTN_TK + r - kk
        bucket = _bucket(jnp.maximum(dist, 0))
        for h in range(N_HEADS):
            shifted = (lookup(bucket, h) - rb_ref[N_BUCKETS - 1, h]) * LOG2E
            bp_ref[h, typ] = jnp.where(dist >= 0, shifted, NEG_BIG)

    rows = 2 * 8
    t = lax.broadcasted_iota(jnp.int32, (rows, 2 * LANES), 0) % 8
    col = lax.broadcasted_iota(jnp.int32, (rows, 2 * LANES), 1)
    dist = jnp.where(col < LANES, LANES + t - col, t - (col - LANES))
    bucket = _bucket(jnp.maximum(dist, 0))
    for h in range(N_HEADS):
        bs_ref[h * rows:(h + 1) * rows, :] = jnp.where(dist >= 0, lookup(bucket, h), NEG_BIG)
        b31s_ref[h * rows:(h + 1) * rows, :] = jnp.full((rows, LANES), rb_ref[N_BUCKETS - 1, h], F32)


def _setup(rel_bias, lq1, lk1, lq2, lk2):
    vspec = pl.BlockSpec(memory_space=pltpu.VMEM)
    return pl.pallas_call(
        _setup_kernel,
        out_shape=(jax.ShapeDtypeStruct((8, LANES), F32),
                   jax.ShapeDtypeStruct((N_HEADS, 2, ATTN_TK, ATTN_TQ), F32),
                   jax.ShapeDtypeStruct((N_HEADS * 16, 2 * LANES), F32),
                   jax.ShapeDtypeStruct((N_HEADS * 16, LANES), F32)),
        in_specs=[pl.BlockSpec(memory_space=pltpu.SMEM), vspec, vspec, vspec, vspec],
        out_specs=(vspec, vspec, vspec, vspec),
        name="setup",
    )(rel_bias, lq1, lk1, lq2, lk2)


def _mod_kernel(c_ref, w_ref, b_ref, o_ref):
    c = c_ref[...]
    s = (c * jax.nn.sigmoid(c)).astype(BF16)
    o_ref[...] = jnp.dot(s, w_ref[...].astype(BF16), preferred_element_type=F32) + b_ref[...]


def _modulation(c_all, w_ada, b_ada):
    n = c_all.shape[0]
    tn = 1024
    return pl.pallas_call(
        _mod_kernel,
        out_shape=jax.ShapeDtypeStruct((n, N_MOD * D_MODEL), F32),
        grid=(N_MOD * D_MODEL // tn,),
        in_specs=[pl.BlockSpec((n, D_MODEL), lambda j: (0, 0)),
                  pl.BlockSpec((D_MODEL, tn), lambda j: (0, j)),
                  pl.BlockSpec((1, tn), lambda j: (0, j))],
        out_specs=pl.BlockSpec((n, tn), lambda j: (0, j)),
        compiler_params=pltpu.CompilerParams(dimension_semantics=("arbitrary",),
                                             vmem_limit_bytes=VMEM_LIMIT),
        name="modulation",
    )(c_all, w_ada, b_ada.reshape(1, -1))


def _inproj_kernel(*refs, carry, pos0, bt, tt, share=None):
    sample_seq = None
    if carry and share is not None:
        pt_ref, refs = refs[0], refs[1:]
        share_in, os_ref = refs[7:7 + N_SHARE_IN], refs[7 + N_SHARE_IN + 7]
        share_scratch = refs[-N_SHARE_SCRATCH:]
        refs = refs[:7] + refs[7 + N_SHARE_IN:7 + N_SHARE_IN + 7] + refs[7 + N_SHARE_IN + 8:-N_SHARE_SCRATCH]
        step = pl.program_id(0) * pl.num_programs(1) + pl.program_id(1)
        n_steps = pl.num_programs(0) * pl.num_programs(1)
        sample_seq = _sample_share(step, n_steps, pt_ref, *share_in, os_ref, *share_scratch, **share)
    if carry:
        (x_ref, mod_ref, g_ref, w_in_ref, w_pool_ref, ps_ref, wkt_ref,
         q_ref, k_ref, v_ref, kb_ref, vb_ref, yp_ref, st_ref, ext_ref, carry_ref) = refs
    else:
        (x_ref, mod_ref, g_ref, w_in_ref, w_pool_ref, ps_ref, prefix_ref,
         q_ref, k_ref, v_ref, yp_ref, st_ref, ext_ref) = refs
    m = bt * tt
    x = x_ref[...]
    h = _rms(x, g_ref[...]) * (1.0 + mod_ref[:, 1:2, :]) + mod_ref[:, 0:1, :]
    h2 = h.reshape(m, D_MODEL)

    def proj(off, width):
        return jnp.dot(h2, w_in_ref[:, off:off + width], preferred_element_type=F32)

    u3 = proj(0, POOL_WIDTH).reshape(bt, tt, POOL_WIDTH)
    if carry:
        @pl.when(pl.program_id(1) == 0)
        def _():
            carry_ref[...] = jnp.zeros_like(carry_ref)
        ext_ref[:, 0:POOL_HALO, :] = carry_ref[...]
    else:
        ext_ref[:, 0:POOL_HALO, :] = prefix_ref[...]
    ext_ref[:, POOL_HALO:, :] = u3
    if carry:
        carry_ref[...] = u3[:, tt - POOL_HALO:, :]
        t_base = pl.program_id(1) * tt
    else:
        t_base = 0
    st_ref[...] = ext_ref[:, tt:tt + POOL_HALO, :]

    if sample_seq is not None:
        for r in range(os_ref.shape[0]):
            sample_seq(r)

    zq = proj(POOL_WIDTH, QK_WIDTH)
    q_scale = HEAD_DIM ** -0.5 * (LOG2E if carry else 1.0)
    q_ref[...] = (zq * q_scale).astype(BF16).reshape(bt, tt, QK_WIDTH)
    zv = proj(POOL_WIDTH + 2 * QK_WIDTH, ATTN_WIDTH)
    if carry:
        vb_ref[...] = zv.astype(BF16).reshape(bt, tt, ATTN_WIDTH)
        for hd in range(N_HEADS):
            v_ref[0, pl.ds(hd, tt, stride=N_HEADS), :] = zv[:, hd * V_HEAD_DIM:(hd + 1) * V_HEAD_DIM]
    else:
        v_ref[...] = zv.reshape(bt, tt, ATTN_WIDTH)

    pos = pos0 + t_base + lax.broadcasted_iota(jnp.int32, (bt, tt, POOL_GROUP_DIM), 1)
    for g, w in enumerate(POOL_WINDOWS):
        sl = slice(g * POOL_GROUP_DIM, (g + 1) * POOL_GROUP_DIM)
        tok = ext_ref[:, POOL_HALO:, sl]
        acc = tok
        for j in range(1, w):
            acc = acc + ext_ref[:, POOL_HALO - j:POOL_HALO - j + tt, sl]
        cnt = jnp.minimum(pos + 1, w).astype(F32)
        d = (acc / cnt - tok).reshape(m, POOL_GROUP_DIM).astype(BF16)
        y = jnp.dot(d, w_pool_ref[g], preferred_element_type=F32) * ps_ref[:, sl]
        yp_ref[:, :, sl] = y.astype(BF16).reshape(bt, tt, POOL_GROUP_DIM)

    if carry:
        zkt = lax.dot_general(wkt_ref[...], h2, (((1,), (1,)), ((), ())), preferred_element_type=F32)
        k_ref[0] = zkt
        kb_ref[...] = zkt.T.astype(BF16).reshape(bt, tt, QK_WIDTH)
    else:
        k_ref[...] = proj(POOL_WIDTH + QK_WIDTH, QK_WIDTH).reshape(bt, tt, QK_WIDTH)


def _inproj(x, mod, g_pre, w_in, w_kt, w_pool, pool_scale, prefix, *, pos0, page_table=None, sample=None,
            seq0=0, n_seq=0):
    nb, t, _ = x.shape
    carry = prefix is None
    if carry:
        bt, tt = 1, ROW_TILE
        grid = (nb, t // tt)
        tile = lambda w: pl.BlockSpec((bt, tt, w), lambda b, i, *_: (b, i, 0))
        per_b = lambda r, w: pl.BlockSpec((bt, r, w), lambda b, i, *_: (b, 0, 0))
        sem = ("arbitrary", "arbitrary")
    else:
        bt, tt = ROW_TILE // t, t
        grid = (nb // bt,)
        tile = lambda w: pl.BlockSpec((bt, tt, w), lambda i: (i, 0, 0))
        per_b = lambda r, w: pl.BlockSpec((bt, r, w), lambda i: (i, 0, 0))
        sem = ("arbitrary",)
    in_specs = [tile(D_MODEL), per_b(N_MOD, D_MODEL), _const_spec((1, 1, D_MODEL)),
                _const_spec((D_MODEL, PROJ_WIDTH)),
                _const_spec((len(POOL_WINDOWS), POOL_GROUP_DIM, POOL_GROUP_DIM)),
                _const_spec((1, POOL_WIDTH))]
    args = [x, mod, g_pre.reshape(1, 1, D_MODEL), w_in, w_pool, pool_scale.reshape(1, POOL_WIDTH)]
    scratch = [pltpu.VMEM((bt, POOL_HALO + tt, POOL_WIDTH), F32)]
    st_shape = jax.ShapeDtypeStruct((nb, POOL_HALO, POOL_WIDTH), F32)
    act = lambda w, dt: jax.ShapeDtypeStruct((nb, t, w), dt)
    if carry:
        scratch.append(pltpu.VMEM((1, POOL_HALO, POOL_WIDTH), F32))
        in_specs.append(_const_spec((QK_WIDTH, D_MODEL)))
        args.append(w_kt)
        out_shape = (act(QK_WIDTH, BF16),
                     jax.ShapeDtypeStruct((nb, QK_WIDTH, t), F32),
                     jax.ShapeDtypeStruct((nb, t * N_HEADS, V_HEAD_DIM), F32), act(QK_WIDTH, BF16),
                     act(ATTN_WIDTH, BF16), act(POOL_WIDTH, BF16), st_shape)
        out_specs = (tile(QK_WIDTH),
                     pl.BlockSpec((1, QK_WIDTH, tt), lambda b, i, *_: (b, 0, i)),
                     pl.BlockSpec((1, tt * N_HEADS, V_HEAD_DIM), lambda b, i, *_: (b, i, 0)), tile(QK_WIDTH),
                     tile(ATTN_WIDTH), tile(POOL_WIDTH), per_b(POOL_HALO, POOL_WIDTH))
        if sample is not None:
            steps_per_b = t // tt
            sh_in, sh_args, sh_out, sh_out_spec, sh_scratch, sh_static = _share_plumbing(
                sample, lambda b, i: b * steps_per_b + i, seq0, n_seq, nb * steps_per_b)
            return pl.pallas_call(
                functools.partial(_inproj_kernel, carry=carry, pos0=pos0, bt=bt, tt=tt, share=sh_static),
                out_shape=out_shape + (sh_out,),
                grid_spec=pltpu.PrefetchScalarGridSpec(
                    num_scalar_prefetch=1, grid=grid, in_specs=in_specs + sh_in,
                    out_specs=out_specs + (sh_out_spec,), scratch_shapes=scratch + sh_scratch),
                compiler_params=pltpu.CompilerParams(dimension_semantics=sem, vmem_limit_bytes=VMEM_LIMIT),
                name="inproj_prompt",
            )(page_table, *args, *sh_args)
    else:
        in_specs.append(per_b(POOL_HALO, POOL_WIDTH))
        args.append(prefix)
        out_shape = (act(QK_WIDTH, BF16), act(QK_WIDTH, F32), act(ATTN_WIDTH, F32),
                     act(POOL_WIDTH, BF16), st_shape)
        out_specs = (tile(QK_WIDTH), tile(QK_WIDTH), tile(ATTN_WIDTH), tile(POOL_WIDTH),
                     per_b(POOL_HALO, POOL_WIDTH))
    return pl.pallas_call(
        functools.partial(_inproj_kernel, carry=carry, pos0=pos0, bt=bt, tt=tt),
        out_shape=out_shape, grid=grid, in_specs=in_specs, out_specs=out_specs,
        scratch_shapes=scratch,
        compiler_params=pltpu.CompilerParams(dimension_semantics=sem, vmem_limit_bytes=VMEM_LIMIT),
        name="inproj_prompt" if carry else "inproj_sample",
    )(*args)


def _attn_unit_order(nt):
    left = {i: i + 1 for i in range(nt)}
    last = {i: -ATTN_CHAIN_GAP for i in range(nt)}
    order = []
    while any(left.values()):
        n = len(order)
        cands = [i for i in range(nt) if left[i]]
        spaced = [i for i in cands if n - last[i] >= ATTN_CHAIN_GAP]
        i = max(spaced, key=lambda t: (left[t], t)) if spaced else max(cands, key=lambda t: n - last[t])
        order.append((i, i + 1 - left[i]))
        left[i] -= 1
        last[i] = n
    return order


def _attn_prompt_head(q_ref, kb_ref, vb_ref, bp_ref, lam_ref, gh_ref, o_ref,
                      w_ref, vt_ref, m_ref, acc_ref, after_pair):
    tq, tk = ATTN_TQ, ATTN_TK
    nt = q_ref.shape[1] // tq
    zero = jnp.zeros((HEAD_DIM, tq), BF16)
    for i in range(nt):
        qt = q_ref[0, i * tq:(i + 1) * tq, :].astype(F32).T.astype(BF16)
        w_ref[i] = jnp.concatenate([jnp.concatenate([qt[:HEAD_DIM], zero], axis=1),
                                    jnp.concatenate([zero, qt[HEAD_DIM:]], axis=1)], axis=0)
    lam = lam_ref[0:1, 0:1]

    def scores(i, j):
        return jnp.dot(kb_ref[0, j * tk:(j + 1) * tk, :], w_ref[i], preferred_element_type=F32)

    for j in range(nt):
        vt_ref[j] = jnp.concatenate([vb_ref[0, j * tk:(j + 1) * tk, :].astype(F32).T,
                                     jnp.ones((SUM_ROWS, tk), F32)], axis=0).astype(BF16)

    units = _attn_unit_order(nt)
    pending = [scores(*u) for u in units[:ATTN_LOOKAHEAD]]
    for n, (i, j) in enumerate(units):
        st = pending.pop(0)
        if n + ATTN_LOOKAHEAD < len(units):
            pending.append(scores(*units[n + ATTN_LOOKAHEAD]))
        vt = vt_ref[j]
        if i - j < 2:
            b = bp_ref[0, i - j]
            st = jnp.concatenate([b, b], axis=1) + st
        mx = jnp.max(st, axis=0, keepdims=True)
        if j == 0:
            p = jnp.exp2(st - mx)
            acc_ref[i] = jnp.dot(vt, p.astype(BF16), preferred_element_type=F32)
            m_ref[i] = mx
        else:
            m_old = m_ref[i]
            m_new = jnp.maximum(m_old, mx)
            alpha = jnp.exp2(m_old - m_new)
            p = jnp.exp2(st - m_new)
            acc_ref[i] = alpha * acc_ref[i] + jnp.dot(vt, p.astype(BF16), preferred_element_type=F32)
            m_ref[i] = m_new
        if i == j:
            acc = acc_ref[j]
            on = acc[:V_HEAD_DIM] / acc[V_HEAD_DIM:V_HEAD_DIM + 1]
            ot = on[:, :tq] - lam * on[:, tq:]
            ot = ot * lax.rsqrt(jnp.mean(ot * ot, axis=0, keepdims=True) + EPS)
            o_ref[0, j * tq:(j + 1) * tq, :] = (ot.T * gh_ref[...] * (1.0 - LAM_INIT)).astype(BF16)
        after_pair(n, len(units))


N_SHARE_IN, N_SHARE_SCRATCH = 9, 4


def _attn_kernel(pt_ref, q_ref, kb_ref, vb_ref, bp_ref, lam_ref, gh_ref, *rest, share):
    share_in = rest[:N_SHARE_IN]
    o_ref, os_ref, w_ref, vt_ref, m_ref, acc_ref = rest[N_SHARE_IN:N_SHARE_IN + 6]
    share_scratch = rest[N_SHARE_IN + 6:]
    step = pl.program_id(0) * pl.num_programs(1) + pl.program_id(1)
    n_steps = pl.num_programs(0) * pl.num_programs(1)
    sample_seq = _sample_share(step, n_steps, pt_ref, *share_in, os_ref, *share_scratch, **share)
    n_seq = os_ref.shape[0]

    def after_pair(n, total):
        for r in range(n_seq):
            if n == (2 * r + 1) * total // (2 * n_seq):
                sample_seq(r)

    _attn_prompt_head(q_ref, kb_ref, vb_ref, bp_ref, lam_ref, gh_ref, o_ref,
                      w_ref, vt_ref, m_ref, acc_ref, after_pair)


def _share_plumbing(sample, step_index, seq0, n_seq, n_steps):
    n_pages, q_s, k_s, v_s, bias_s, b31s, lam, g_head, ck, cv = sample
    tnew = q_s.shape[1]
    page = ck.shape[2]
    past = n_pages * page
    assert seq0 % n_seq == 0
    seqs = lambda w: pl.BlockSpec((n_seq, tnew, w), lambda *g: (seq0 // n_seq + step_index(*g[:-1]), 0, 0))
    out_spec = pl.BlockSpec((n_seq, tnew, ATTN_WIDTH), lambda *g: (step_index(*g[:-1]), 0, 0))
    const = lambda a: pl.BlockSpec(a.shape, lambda *g: (0,) * a.ndim)
    hbm = pl.BlockSpec(memory_space=pl.ANY)
    gh = g_head.reshape(1, V_HEAD_DIM)
    in_specs = [seqs(QK_WIDTH), seqs(QK_WIDTH), seqs(ATTN_WIDTH), const(bias_s), const(b31s), const(lam),
                const(gh), hbm, hbm]
    args = [q_s, k_s, v_s, bias_s, b31s, lam, gh, ck, cv]
    out_shape = jax.ShapeDtypeStruct((n_steps * n_seq, tnew, ATTN_WIDTH), BF16)
    scratch = [pltpu.VMEM((SAMPLE_SLOTS, n_pages, QK_WIDTH, page), F32),
               pltpu.VMEM((SAMPLE_SLOTS, n_pages, page * N_HEADS, V_HEAD_DIM), F32),
               pltpu.VMEM((past + LANES, ATTN_WIDTH), BF16),
               pltpu.SemaphoreType.DMA((SAMPLE_SLOTS,))]
    assert len(in_specs) == N_SHARE_IN and len(scratch) == N_SHARE_SCRATCH
    return in_specs, args, out_shape, out_spec, scratch, dict(n_pages=n_pages, page=page, seq0=seq0)


def _sample_share(step, n_steps, pt_ref, qs_ref, kn_ref, vn_ref, bs_ref, b31s_ref, lam_ref, gh_ref,
                  ck_hbm, cv_hbm, os_ref, kraw_ref, vraw_ref, vbf_ref, sem_ref,
                  *, n_pages, page, seq0):
    n_seq = qs_ref.shape[0]
    first = seq0 + step * n_seq

    def page_copies(seq, slot):
        cps = []
        for j in range(n_pages):
            pg = pt_ref[seq, j]
            cps.append(pltpu.make_async_copy(ck_hbm.at[pg], kraw_ref.at[slot, j], sem_ref.at[slot]))
            cps.append(pltpu.make_async_copy(cv_hbm.at[pg], vraw_ref.at[slot, j], sem_ref.at[slot]))
        return cps

    def start(seq):
        for cp in page_copies(seq, seq & (SAMPLE_SLOTS - 1)):
            cp.start()

    def wait(seq):
        for cp in page_copies(seq, seq & (SAMPLE_SLOTS - 1)):
            cp.wait()

    @pl.when(step == 0)
    def _():
        start(first)

    def sample_seq(r):
        seq = first + r
        if r + 1 < n_seq:
            start(seq + 1)
        else:
            @pl.when(step + 1 < n_steps)
            def _():
                start(seq + 1)
        wait(seq)
        _attn_sample_one(r, seq & (SAMPLE_SLOTS - 1), qs_ref, kn_ref, vn_ref, bs_ref, b31s_ref, lam_ref, gh_ref,
                         os_ref, kraw_ref, vraw_ref, vbf_ref, n_pages=n_pages, page=page)

    return sample_seq


def _attention(q, kb, vb, bias_p, lam, g_head, page_table, sample, *, seq0, n_seq):
    nb, t, _ = q.shape
    tq = ATTN_TQ
    nt = t // tq
    hw = V_HEAD_DIM
    n_steps = nb * N_HEADS
    sh_in, sh_args, sh_out, sh_out_spec, sh_scratch, sh_static = _share_plumbing(
        sample, lambda b, h: b * N_HEADS + h, seq0, n_seq, n_steps)
    head = pl.BlockSpec((1, t, hw), lambda b, h, pt: (b, 0, h))
    const = lambda shape: pl.BlockSpec(shape, lambda b, h, pt: (0,) * len(shape))
    return pl.pallas_call(
        functools.partial(_attn_kernel, share=sh_static),
        out_shape=(jax.ShapeDtypeStruct((nb, t, ATTN_WIDTH), BF16), sh_out),
        grid_spec=pltpu.PrefetchScalarGridSpec(
            num_scalar_prefetch=1, grid=(nb, N_HEADS),
            in_specs=[head, head, head,
                      pl.BlockSpec((1, 2, ATTN_TK, ATTN_TQ), lambda b, h, pt: (h, 0, 0, 0)),
                      const((8, LANES)), const((1, hw))] + sh_in,
            out_specs=(head, sh_out_spec),
            scratch_shapes=[pltpu.VMEM((nt, 2 * HEAD_DIM, 2 * tq), BF16),
                            pltpu.VMEM((nt, hw + SUM_ROWS, ATTN_TK), BF16),
                            pltpu.VMEM((nt, 1, 2 * tq), F32),
                            pltpu.VMEM((nt, hw + SUM_ROWS, 2 * tq), F32)] + sh_scratch),
        compiler_params=pltpu.CompilerParams(
            dimension_semantics=("arbitrary", "arbitrary"), vmem_limit_bytes=VMEM_LIMIT),
        name="attention",
    )(page_table, q, kb, vb, bias_p, lam, g_head.reshape(1, hw), *sh_args)


def _attn_sample_one(row, slot, q_ref, kn_ref, vn_ref, bs_ref, b31s_ref, lam_ref, gh_ref, o_ref,
                     kraw_ref, vraw_ref, vbf_ref, *, n_pages, page):
    past = n_pages * page
    tnew = q_ref.shape[1]
    n_maps = 2 * N_HEADS

    q = q_ref[row].astype(F32)
    qt = jnp.concatenate([q] * n_maps, axis=0)
    rg = lax.broadcasted_iota(jnp.int32, qt.shape, 0) // tnew
    cg = lax.broadcasted_iota(jnp.int32, qt.shape, 1) // HEAD_DIM
    qbd = jnp.where(rg == cg, qt, 0.0)

    for j in range(n_pages):
        for h in range(N_HEADS):
            vbf_ref[j * page:(j + 1) * page, h * V_HEAD_DIM:(h + 1) * V_HEAD_DIM] = (
                vraw_ref[slot, j, pl.ds(h, page, stride=N_HEADS), :].astype(BF16))
    pad = jnp.zeros((LANES - tnew, QK_WIDTH), F32)
    k_new = jnp.concatenate([kn_ref[row], pad], axis=0)
    vbf_ref[past:, :] = jnp.concatenate([vn_ref[row], pad], axis=0).astype(BF16)

    k_past = jnp.concatenate([kraw_ref[slot, j] for j in range(n_pages)], axis=1)
    s_past = jnp.dot(qbd, k_past, preferred_element_type=F32)
    s_new = lax.dot_general(qbd, k_new, (((1,), (1,)), ((), ())), preferred_element_type=F32)
    near = past - page
    s = jnp.concatenate([s_past[:, :near] + b31s_ref[:, 0:1],
                         jnp.concatenate([s_past[:, near:], s_new], axis=1) + bs_ref[...]], axis=1)
    mx = jnp.max(s, axis=-1, keepdims=True)
    p = jnp.exp(s - mx)
    pn = p / jnp.sum(p, axis=-1, keepdims=True)
    lam = lam_ref[0:1, 0:1]
    a = jnp.concatenate(
        [pn[2 * h * tnew:(2 * h + 1) * tnew] - lam * pn[(2 * h + 1) * tnew:(2 * h + 2) * tnew]
         for h in range(N_HEADS)], axis=0).astype(BF16)
    r = jnp.dot(a, vbf_ref[...], preferred_element_type=F32)
    outs = []
    for h in range(N_HEADS):
        o = r[h * tnew:(h + 1) * tnew, h * V_HEAD_DIM:(h + 1) * V_HEAD_DIM]
        outs.append(_rms(o, gh_ref[...]) * (1.0 - LAM_INIT))
    o_ref[row] = jnp.concatenate(outs, axis=1).astype(BF16)


def _ffn_kernel(*refs, carry, ns, bt, tt):
    if carry:
        (x_ref, yp_ref, at_ref, mod_ref, gpm_ref, gpf_ref, gpo_ref, w_out_ref, w_up_ref, cw_ref, cb_ref,
         w_down_ref, y_ref, cs_ref, *buf_refs, carry_ref) = refs
    else:
        (x_ref, yp_ref, at_ref, mod_ref, gpm_ref, gpf_ref, gpo_ref, w_out_ref, w_up_ref, cw_ref, cb_ref,
         w_down_ref, prefix_ref, y_ref, cs_ref, *buf_refs) = refs
    m = bt * tt
    n_chunks = D_FF // FF_CHUNK
    streams = range(ns)

    def rows(s):
        return (slice(None), slice(s * tt, (s + 1) * tt)) if carry else (slice(s * bt, (s + 1) * bt),)

    def mod_row(s, i):
        return mod_ref[:, i:i + 1, :] if carry else mod_ref[s * bt:(s + 1) * bt, i:i + 1, :]

    def buf(s, slot):
        return buf_refs[4 * s + slot]

    if carry:
        @pl.when(pl.program_id(1) == 0)
        def _():
            carry_ref[...] = jnp.zeros_like(carry_ref)
    halo_rows = CONV_HALO if carry else prefix_ref.shape[1]

    x1, h2 = {}, {}
    for s in streams:
        mix_in = jnp.concatenate([yp_ref[rows(s)], at_ref[rows(s)]], axis=-1).reshape(m, D_MODEL)
        mix = jnp.dot(mix_in, w_out_ref[...], preferred_element_type=F32).reshape(bt, tt, D_MODEL)
        x1[s] = x_ref[rows(s)] + mod_row(s, 2) * _rms(mix, gpm_ref[...])
        h = _rms(x1[s], gpf_ref[...]) * (1.0 + mod_row(s, 4)) + mod_row(s, 3)
        h2[s] = h.reshape(m, D_MODEL).astype(BF16)

    def up_part(s, slot, off):
        sl = slice(off, off + FF_CHUNK)
        u3 = jnp.dot(h2[s], w_up_ref[:, sl], preferred_element_type=F32).reshape(bt, tt, FF_CHUNK)
        if not carry:
            halo = prefix_ref[s * bt:(s + 1) * bt, :, sl]
        elif s == 0:
            halo = carry_ref[:, :, sl]
        else:
            halo = buf(s - 1, slot)[:, tt:tt + CONV_HALO, :]
        buf(s, slot)[:, CONV_HALO - halo_rows:CONV_HALO, :] = halo
        buf(s, slot)[:, CONV_HALO:, :] = u3
        if not carry:
            cs_ref[s * bt:(s + 1) * bt, :, sl] = u3[:, tt - halo_rows:, :]
        elif s == ns - 1:
            cs_ref[:, :, sl] = u3[:, tt - halo_rows:, :]

    def conv_part(s, slot, off):
        sl = slice(off, off + FF_CHUNK)
        b = buf(s, slot)
        y = (cb_ref[:, :, sl]
             + cw_ref[0:1, :, sl] * b[:, CONV_HALO - 2:CONV_HALO - 2 + tt, :]
             + cw_ref[1:2, :, sl] * b[:, CONV_HALO - 1:CONV_HALO - 1 + tt, :]
             + cw_ref[2:3, :, sl] * b[:, CONV_HALO:, :])
        return y.reshape(m, FF_CHUNK)

    def up_chunk(s, c):
        up_part(s, 2 * (c % 2), c * FF_CHUNK)
        up_part(s, 2 * (c % 2) + 1, D_FF + c * FF_CHUNK)

    f = {s: jnp.zeros((m, D_MODEL), F32) for s in streams}
    for s in streams:
        up_chunk(s, 0)
    for c in range(n_chunks):
        if c + 1 < n_chunks:
            for s in streams:
                up_chunk(s, c + 1)
        for s in streams:
            gate = conv_part(s, 2 * (c % 2), c * FF_CHUNK)
            val = conv_part(s, 2 * (c % 2) + 1, D_FF + c * FF_CHUNK)
            act = (_gelu_tanh(gate) * val).astype(BF16)
            f[s] = f[s] + jnp.dot(act, w_down_ref[c * FF_CHUNK:(c + 1) * FF_CHUNK, :],
                                  preferred_element_type=F32)
    if carry:
        carry_ref[...] = cs_ref[...]
    for s in streams:
        y_ref[rows(s)] = x1[s] + mod_row(s, 5) * _rms(f[s].reshape(bt, tt, D_MODEL), gpo_ref[...])


def _ffn(x, yp, attn, mod, g_post_mix, g_pre_ffn, g_post_ffn, w_out, w_up, conv_w, conv_b, w_down, prefix):
    nb, t, _ = x.shape
    carry = prefix is None
    ns = FFN_STREAMS
    if carry:
        bt, tt = 1, ROW_TILE
        grid = (nb, t // (ns * tt))
        tile = lambda w: pl.BlockSpec((1, ns * tt, w), lambda b, i: (b, i, 0))
        per_b = lambda r, w: pl.BlockSpec((1, r, w), lambda b, i: (b, 0, 0))
        sem = ("arbitrary", "arbitrary")
    else:
        bt, tt = ROW_TILE // t, t
        grid = (nb // (ns * bt),)
        tile = lambda w: pl.BlockSpec((ns * bt, tt, w), lambda i: (i, 0, 0))
        per_b = lambda r, w: pl.BlockSpec((ns * bt, r, w), lambda i: (i, 0, 0))
        sem = ("arbitrary",)
    gspec = _const_spec((1, 1, D_MODEL))
    in_specs = [tile(D_MODEL), tile(POOL_WIDTH), tile(ATTN_WIDTH), per_b(N_MOD, D_MODEL),
                gspec, gspec, gspec,
                _const_spec((D_MODEL, D_MODEL)), _const_spec((D_MODEL, 2 * D_FF)),
                _const_spec((CONV_WIDTH, 1, 2 * D_FF)), _const_spec((1, 1, 2 * D_FF)),
                _const_spec((D_FF, D_MODEL))]
    args = [x, yp, attn, mod, g_post_mix.reshape(1, 1, -1), g_pre_ffn.reshape(1, 1, -1),
            g_post_ffn.reshape(1, 1, -1), w_out, w_up, conv_w.reshape(CONV_WIDTH, 1, -1),
            conv_b.reshape(1, 1, -1), w_down]
    scratch = [pltpu.VMEM((bt, CONV_HALO + tt, FF_CHUNK), F32) for _ in range(4 * ns)]
    if carry:
        halo_rows = CONV_HALO
        scratch.append(pltpu.VMEM((1, halo_rows, 2 * D_FF), F32))
    else:
        halo_rows = prefix.shape[1]
        in_specs.append(per_b(halo_rows, 2 * D_FF))
        args.append(prefix)
    return pl.pallas_call(
        functools.partial(_ffn_kernel, carry=carry, ns=ns, bt=bt, tt=tt),
        out_shape=(jax.ShapeDtypeStruct((nb, t, D_MODEL), F32),
                   jax.ShapeDtypeStruct((nb, halo_rows, 2 * D_FF), F32)),
        grid=grid, in_specs=in_specs,
        out_specs=(tile(D_MODEL), per_b(halo_rows, 2 * D_FF)),
        scratch_shapes=scratch,
        compiler_params=pltpu.CompilerParams(dimension_semantics=sem, vmem_limit_bytes=VMEM_LIMIT),
        name="ffn_prompt" if carry else "ffn_sample",
    )(*args)


def kernel(x_prompt, x_sample, c_prompt, c_sample, cache_k, cache_v, page_table, state_pool, state_conv,
           w_ada, b_ada, g_pre_mix, g_post_mix, g_pre_ffn, g_post_ffn, w_in, w_out, w_pool, pool_scale,
           lam_q1, lam_k1, lam_q2, lam_k2, g_head, rel_bias, w_up, conv_w, conv_b, w_down):
    nbp, seq, _ = x_prompt.shape
    nbs, dec_seq, _ = x_sample.shape
    past_len = page_table.shape[1] * cache_k.shape[2]
    l = 0

    lam, bias_p, bias_s, b31s = _setup(rel_bias, lam_q1[l:l + 1], lam_k1[l:l + 1],
                                             lam_q2[l:l + 1], lam_k2[l:l + 1])
    mod = _modulation(jnp.concatenate([c_prompt, c_sample], axis=0), w_ada[l], b_ada[l])
    mod = mod.reshape(nbp + nbs, N_MOD, D_MODEL)
    mod_p, mod_s = mod[:nbp], mod[nbp:]

    w_in_b = w_in[l]
    w_kt_b = w_in[l][:, POOL_WIDTH + QK_WIDTH:POOL_WIDTH + 2 * QK_WIDTH].T
    w_pool_b = w_pool[l].astype(BF16)
    w_out_b = w_out[l].astype(BF16)
    w_up_b = w_up[l].astype(BF16)
    w_down_b = w_down[l].astype(BF16)

    pool_prefix = jnp.pad(state_pool[l], ((0, 0), (POOL_HALO - POOL_STATE, 0), (0, 0)))

    q_s, k_s, v_s, yp_s, pstate_s = _inproj(
        x_sample, mod_s, g_pre_mix[l], w_in_b, w_kt_b, w_pool_b, pool_scale[l], pool_prefix, pos0=past_len)

    n_phys, page = cache_k.shape[1], cache_k.shape[2]
    ck = jnp.transpose(cache_k[l], (0, 2, 3, 4, 1)).reshape(n_phys, QK_WIDTH, page)
    cv = cache_v[l].reshape(n_phys, page * N_HEADS, V_HEAD_DIM)
    sample = (page_table.shape[1], q_s, k_s, v_s, bias_s, b31s, lam, g_head[l], ck, cv)

    n_inproj_steps = nbp * (seq // ROW_TILE)
    n_attn_steps = nbp * N_HEADS
    seqs_in_attn = nbs - n_inproj_steps
    assert seqs_in_attn % (n_attn_steps * SAMPLE_SLOTS) == 0
    q_p, kt_p, v_p, kb_p, vb_p, yp_p, pstate_p, attn_s2 = _inproj(
        x_prompt, mod_p, g_pre_mix[l], w_in_b, w_kt_b, w_pool_b, pool_scale[l], None, pos0=0,
        page_table=page_table, sample=sample, seq0=seqs_in_attn, n_seq=1)
    attn_p, attn_s1 = _attention(q_p, kb_p, vb_p, bias_p, lam, g_head[l], page_table, sample,
                                 seq0=0, n_seq=seqs_in_attn // n_attn_steps)
    attn_s = jnp.concatenate([attn_s1, attn_s2], axis=0)
    k_p = jnp.transpose(kt_p.reshape(nbp, N_HEADS, 2, HEAD_DIM, seq), (0, 4, 1, 2, 3))

    outs = []
    for (x, md, yp, attn, k, v, pstate, cprefix) in (
            (x_prompt, mod_p, yp_p, attn_p, k_p, v_p, pstate_p, None),
            (x_sample, mod_s, yp_s, attn_s, k_s, v_s, pstate_s, state_conv[l])):
        nb, t = x.shape[0], x.shape[1]
        y, cstate = _ffn(x, yp, attn, md, g_post_mix[l], g_pre_ffn[l], g_post_ffn[l],
                         w_out_b, w_up_b, conv_w[l], conv_b[l], w_down_b, cprefix)
        outs.append((y,
                     k.reshape(1, nb, t, N_HEADS, 2, HEAD_DIM),
                     v.reshape(1, nb, t, N_HEADS, V_HEAD_DIM),
                     pstate[None, :, POOL_HALO - POOL_STATE:, :],
                     cstate[None, :, cstate.shape[1] - (CONV_WIDTH - 1):, :]))
    (yp_, kp, vp, pp, cp), (ys_, ks, vs, ps, cs) = outs
    return (yp_, ys_, kp, vp, pp, cp, ks, vs, ps, cs)
```
